```python
import math
import jax, jax.numpy as jnp
from jax import lax
import numpy as np

D_MODEL = 1024
BATCH = 8
SEQ = 8192
DEPTH = 2

N_META = 16
D_FF = 2816
NORM_EPS = 1e-6

RET_HEADS = 4
RET_DK = 128
RET_DV = 128
RET_CHUNK = 128
GDN_HEADS = 4
GDN_DK = 128
GDN_DV = 128
GDN_CONV = 4
GDN_CHUNK = 64
GDN_QKV = 2 * GDN_HEADS * GDN_DK + GDN_HEADS * GDN_DV
SWA_HEADS = 8
SWA_KV_HEADS = 2
SWA_DH = 64
SWA_WINDOW = 128
SB_HEADS = 8
SB_DH = 64
ATT_BLOCK = 128

AB_WIDTHS = (RET_HEADS * RET_DK, RET_HEADS * RET_DK, RET_HEADS * RET_DV, RET_HEADS * RET_DV,
             GDN_QKV, GDN_HEADS * GDN_DV, GDN_HEADS, GDN_HEADS)
AB_IN = 2 * RET_HEADS * RET_DK + 2 * RET_HEADS * RET_DV + GDN_QKV + GDN_HEADS * GDN_DV + 2 * GDN_HEADS
AB_OUT = RET_HEADS * RET_DV + GDN_HEADS * GDN_DV
CD_WIDTHS = (SWA_HEADS * SWA_DH, SWA_KV_HEADS * SWA_DH, SWA_KV_HEADS * SWA_DH,
             SB_HEADS * SB_DH, SB_HEADS * SB_DH, SB_HEADS * SB_DH)
CD_IN = SWA_HEADS * SWA_DH + 2 * SWA_KV_HEADS * SWA_DH + 3 * SB_HEADS * SB_DH
CD_OUT = SWA_HEADS * SWA_DH + SB_HEADS * SB_DH

kernel_name = "hybrid_retention_gdn_swa_stickbreak_macaron"


def rms_norm(x, g):
    xf = x.astype(jnp.float32)
    y = xf * lax.rsqrt(jnp.mean(xf * xf, axis=-1, keepdims=True) + NORM_EPS)
    return (y * g.astype(jnp.float32)).astype(x.dtype)


def l2_normalize(x):
    return x * lax.rsqrt(jnp.sum(x * x, axis=-1, keepdims=True) + NORM_EPS)


def swiglu(x, w_gate, w_up, w_down):
    return (jax.nn.silu(x @ w_gate) * (x @ w_up)) @ w_down


def split_cols(t, widths):
    return jnp.split(t, np.cumsum(widths)[:-1].tolist(), axis=-1)


def pad_front(t, n):
    return jnp.pad(t, [(0, 0), (n, 0)] + [(0, 0)] * (t.ndim - 2))


def rotate_pairs(x, pos):
    d = x.shape[-1]
    inv_freq = 1.0 / (10000.0 ** jnp.linspace(0.0, 1.0, d // 2, dtype=jnp.float32))
    ang = pos[:, None] * inv_freq[None, :]
    cos = jnp.cos(ang)[None, :, None, :]
    sin = jnp.sin(ang)[None, :, None, :]
    x1, x2 = x[..., 0::2], x[..., 1::2]
    return jnp.stack([x1 * cos - x2 * sin, x2 * cos + x1 * sin], axis=-1).reshape(x.shape)


def causal_depthwise_conv(x, w):
    K, ch = w.shape
    return lax.conv_general_dilated(x, w[:, None, :], window_strides=(1,), padding=[(K - 1, 0)],
                                    dimension_numbers=('NWC', 'WIO', 'NWC'), feature_group_count=ch)


def retention_chunked(q, k, v):
    B, Lp, H, dk = q.shape
    dv = v.shape[-1]
    C = RET_CHUNK
    N = Lp // C
    q = q.reshape(B, N, C, H, dk)
    k = k.reshape(B, N, C, H, dk)
    v = v.reshape(B, N, C, H, dv)
    log_gamma = jnp.log1p(-jnp.exp2(-5.0 - jnp.arange(H, dtype=jnp.float32)))
    idx = jnp.arange(C, dtype=jnp.float32)
    diff = idx[:, None] - idx[None, :]
    intra_decay = jnp.where(diff >= 0, jnp.exp(jnp.maximum(diff, 0.0) * log_gamma[:, None, None]), 0.0)
    scores = jnp.einsum('bnihd,bnjhd->bnhij', q, k) * intra_decay
    intra = jnp.einsum('bnhij,bnjhe->bnihe', scores, v)
    zeta = jnp.exp((C - 1.0 - idx)[:, None] * log_gamma[None, :])
    xi = jnp.exp((idx + 1.0)[:, None] * log_gamma[None, :])
    kv = jnp.einsum('bnjhd,bnjhe->bnhde', k * zeta[..., None], v)
    chunk_decay = jnp.exp(C * log_gamma)[:, None, None]

    def step(S, kv_n):
        return S * chunk_decay + kv_n, S

    _, s_prev = lax.scan(step, jnp.zeros_like(kv[:, 0]), jnp.moveaxis(kv, 1, 0))
    s_prev = jnp.moveaxis(s_prev, 0, 1)
    cross = jnp.einsum('bnihd,bnhde->bnihe', q * xi[..., None], s_prev)
    return (intra + cross).reshape(B, Lp, H, dv)


def gated_delta_chunked(q, k, v, g, beta):
    B, Lp, H, dk = q.shape
    dv = v.shape[-1]
    C = GDN_CHUNK
    N = Lp // C

    def chunks(t):
        return jnp.moveaxis(t.reshape((B, N, C, H) + t.shape[3:]), 2, 3)

    q = chunks(q) * dk ** -0.5
    k = chunks(k)
    v = chunks(v)
    g = chunks(g)
    beta = chunks(beta)
    g_cum = jnp.cumsum(g, axis=-1)
    idx = jnp.arange(C)
    incl = idx[:, None] >= idx[None, :]
    strict = idx[:, None] > idx[None, :]
    gdiff = g_cum[..., :, None] - g_cum[..., None, :]
    decay = jnp.where(incl, jnp.exp(jnp.where(incl, gdiff, 0.0)), 0.0)
    k_beta = k * beta[..., None]
    a_mat = jnp.where(strict, jnp.einsum('bnhid,bnhjd->bnhij', k_beta, k) * decay, 0.0)
    system = a_mat + jnp.eye(C, dtype=a_mat.dtype)
    rhs = jnp.concatenate([v * beta[..., None], k_beta * jnp.exp(g_cum)[..., None]], axis=-1)
    sol = lax.linalg.triangular_solve(system, rhs, left_side=True, lower=True, unit_diagonal=True)
    u_vec, w_vec = sol[..., :dv], sol[..., dv:]
    qk = jnp.where(incl, jnp.einsum('bnhid,bnhjd->bnhij', q, k) * decay, 0.0)
    q_dec = q * jnp.exp(g_cum)[..., None]
    g_last = g_cum[..., -1]
    k_tail = k * jnp.exp(g_last[..., None] - g_cum)[..., None]

    def step(S, xs):
        u_n, w_n, qk_n, qd_n, kt_n, gl_n = xs
        v_new = u_n - jnp.einsum('bhcd,bhde->bhce', w_n, S)
        o_n = jnp.einsum('bhcd,bhde->bhce', qd_n, S) + jnp.einsum('bhij,bhje->bhie', qk_n, v_new)
        S = S * jnp.exp(gl_n)[..., None, None] + jnp.einsum('bhcd,bhce->bhde', kt_n, v_new)
        return S, o_n

    xs = tuple(jnp.moveaxis(t, 1, 0) for t in (u_vec, w_vec, qk, q_dec, k_tail, g_last))
    _, o = lax.scan(step, jnp.zeros((B, H, dk, dv), q.dtype), xs)
    return jnp.moveaxis(o, 0, 1).transpose(0, 1, 3, 2, 4).reshape(B, Lp, H, dv)


def swa_sink_attention(q, k, v, sinks):
    B, Lp, HQ, dh = q.shape
    HKV = k.shape[2]
    G = HQ // HKV
    BLK = ATT_BLOCK
    NB = Lp // BLK
    pad = BLK - N_META
    qb = q.reshape(B, NB, BLK, HKV, G, dh) * dh ** -0.5
    kb = k.reshape(B, NB, BLK, HKV, dh)
    vb = v.reshape(B, NB, BLK, HKV, dh)
    shift = ((0, 0), (1, 0), (0, 0), (0, 0), (0, 0))
    k_prev = jnp.pad(kb, shift)[:, :-1]
    v_prev = jnp.pad(vb, shift)[:, :-1]
    meta_k = jnp.broadcast_to(k[:, None, pad:BLK], (B, NB, N_META, HKV, dh))
    meta_v = jnp.broadcast_to(v[:, None, pad:BLK], (B, NB, N_META, HKV, dh))
    keys = jnp.concatenate([meta_k, k_prev, kb], axis=2)
    vals = jnp.concatenate([meta_v, v_prev, vb], axis=2)
    blocks = jnp.arange(NB)
    qpos = blocks[:, None] * BLK + jnp.arange(BLK)[None, :]
    kpos = (blocks[:, None] - 1) * BLK + jnp.arange(2 * BLK)[None, :]
    dpos = qpos[:, :, None] - kpos[:, None, :]
    band = (dpos >= 0) & (dpos < SWA_WINDOW) & (kpos[:, None, :] >= BLK)
    meta_vis = (pad + jnp.arange(N_META))[None, None, :] <= qpos[:, :, None]
    mask = jnp.concatenate([jnp.broadcast_to(meta_vis, (NB, BLK, N_META)), band], axis=-1)
    s = jnp.einsum('bnqhgd,bnkhd->bnhgqk', qb, keys)
    s = jnp.where(mask[None, :, None, None], s, -jnp.inf)
    sink = jnp.broadcast_to(sinks.astype(s.dtype).reshape(1, 1, HKV, G, 1, 1), s.shape[:-1] + (1,))
    p = jax.nn.softmax(jnp.concatenate([s, sink], axis=-1), axis=-1)[..., :-1]
    o = jnp.einsum('bnhgqk,bnkhd->bnqhgd', p, vals)
    return o.reshape(B, Lp, HQ, dh)


def stick_breaking_attention(q, k, v):
    B, Lp, H, dh = q.shape
    BLK = ATT_BLOCK
    NB = Lp // BLK
    pad = BLK - N_META
    kpos = jnp.arange(Lp)
    qb = jnp.moveaxis(q.reshape(B, NB, BLK, H, dh), 1, 0)
    scale = dh ** -0.5

    def block(args):
        q_blk, n = args
        qpos = n * BLK + jnp.arange(BLK)
        valid = (kpos[None, :] < qpos[:, None]) & (kpos[None, :] >= pad)
        z = jnp.einsum('bqhd,bkhd->bhqk', q_blk, k) * scale
        log_beta = jax.nn.log_sigmoid(z)
        log_1m_beta = jnp.where(valid, jax.nn.log_sigmoid(-z), 0.0)
        log_stick = lax.cumsum(log_1m_beta, axis=3, reverse=True) - log_1m_beta
        a = jnp.where(valid, jnp.exp(log_beta + log_stick), 0.0)
        return jnp.einsum('bhqk,bkhd->bqhd', a, v)

    o = lax.map(block, (qb, jnp.arange(NB)))
    return jnp.moveaxis(o, 0, 1).reshape(B, Lp, H, dh)


def mixer_ab(u, w_in, conv_w, a_log, dt_bias, out_norm, w_out):
    B, L, _ = u.shape
    f32 = jnp.float32
    rq, rk, rv, rg, gqkv, gz, gb, ga = split_cols(u @ w_in, AB_WIDTHS)
    pos = jnp.arange(L, dtype=f32)
    rq = rotate_pairs(rq.astype(f32).reshape(B, L, RET_HEADS, RET_DK), pos)
    rk = rotate_pairs(rk.astype(f32).reshape(B, L, RET_HEADS, RET_DK), pos) * RET_DK ** -0.5
    rv = rv.astype(f32).reshape(B, L, RET_HEADS, RET_DV)
    rpad = RET_CHUNK - N_META
    ret = retention_chunked(pad_front(rq, rpad), pad_front(rk, rpad), pad_front(rv, rpad))[:, rpad:]
    mu = jnp.mean(ret, axis=-1, keepdims=True)
    var = jnp.mean(jnp.square(ret - mu), axis=-1, keepdims=True)
    ret = ((ret - mu) * lax.rsqrt(var + NORM_EPS)).reshape(B, L, -1) * jax.nn.silu(rg.astype(f32))
    qkv = jax.nn.silu(causal_depthwise_conv(gqkv.astype(f32), conv_w.astype(f32)))
    gq, gk, gv = split_cols(qkv, (GDN_HEADS * GDN_DK, GDN_HEADS * GDN_DK, GDN_HEADS * GDN_DV))
    gq = l2_normalize(gq.reshape(B, L, GDN_HEADS, GDN_DK))
    gk = l2_normalize(gk.reshape(B, L, GDN_HEADS, GDN_DK))
    gv = gv.reshape(B, L, GDN_HEADS, GDN_DV)
    beta = jax.nn.sigmoid(gb.astype(f32))
    g = -jnp.exp(a_log.astype(f32)) * jax.nn.softplus(ga.astype(f32) + dt_bias.astype(f32))
    gpad = GDN_CHUNK - N_META
    o = gated_delta_chunked(*(pad_front(t, gpad) for t in (gq, gk, gv, g, beta)))[:, gpad:]
    o = rms_norm(o, out_norm) * jax.nn.silu(gz.astype(f32).reshape(B, L, GDN_HEADS, GDN_DV))
    mixed = jnp.concatenate([ret, o.reshape(B, L, -1)], axis=-1).astype(u.dtype)
    return mixed @ w_out


def mixer_cd(u, w_in, sinks, w_out):
    B, L, _ = u.shape
    pad = ATT_BLOCK - N_META
    Lp = L + pad
    cq, ck, cv, sq, sk, sv = [pad_front(t.astype(jnp.float32), pad) for t in split_cols(u @ w_in, CD_WIDTHS)]
    swa = swa_sink_attention(cq.reshape(B, Lp, SWA_HEADS, SWA_DH), ck.reshape(B, Lp, SWA_KV_HEADS, SWA_DH),
                             cv.reshape(B, Lp, SWA_KV_HEADS, SWA_DH), sinks)
    sb = stick_breaking_attention(sq.reshape(B, Lp, SB_HEADS, SB_DH), sk.reshape(B, Lp, SB_HEADS, SB_DH),
                                  sv.reshape(B, Lp, SB_HEADS, SB_DH))
    mixed = jnp.concatenate([swa.reshape(B, Lp, -1), sb.reshape(B, Lp, -1)], axis=-1)[:, pad:]
    return mixed.astype(u.dtype) @ w_out


def _fwd_setup_inputs(seed: int = 0) -> dict:
    key = jax.random.key(seed)
    ks = jax.random.split(key, 16)
    f32 = jnp.float32
    n_even = (DEPTH + 1) // 2
    n_odd = DEPTH // 2

    def dense(k, shape, fan_in):
        return jax.random.normal(k, shape, f32) * fan_in ** -0.5

    x = jax.random.normal(ks[0], (BATCH, SEQ, D_MODEL), f32)
    meta_tokens = jax.random.normal(ks[1], (N_META, D_MODEL), f32)
    norm_gains = 1.0 + 0.01 * jax.random.normal(ks[2], (DEPTH, 6, D_MODEL), f32)
    ffn_w_gate = dense(ks[3], (DEPTH, 2, D_MODEL, D_FF), D_MODEL)
    ffn_w_up = dense(ks[4], (DEPTH, 2, D_MODEL, D_FF), D_MODEL)
    ffn_w_down = dense(ks[5], (DEPTH, 2, D_FF, D_MODEL), D_FF)
    ab_w_in = dense(ks[6], (n_even, D_MODEL, AB_IN), D_MODEL)
    ab_conv_w = dense(ks[7], (n_even, GDN_CONV, GDN_QKV), GDN_CONV)
    ab_a_log = jnp.log(jax.random.uniform(ks[8], (n_even, GDN_HEADS), f32, 1.0, 16.0))
    dt = jnp.exp(jax.random.uniform(ks[9], (n_even, GDN_HEADS), f32, math.log(1e-3), math.log(1e-1)))
    ab_dt_bias = dt + jnp.log(-jnp.expm1(-dt))
    ab_out_norm = 1.0 + 0.01 * jax.random.normal(ks[10], (n_even, GDN_DV), f32)
    ab_w_out = dense(ks[11], (n_even, AB_OUT, D_MODEL), AB_OUT)
    cd_w_in = dense(ks[12], (n_odd, D_MODEL, CD_IN), D_MODEL)
    cd_sinks = jax.random.normal(ks[13], (n_odd, SWA_HEADS), f32)
    cd_w_out = dense(ks[14], (n_odd, CD_OUT, D_MODEL), CD_OUT)
    return {"x": x, "meta_tokens": meta_tokens, "norm_gains": norm_gains,
            "ffn_w_gate": ffn_w_gate, "ffn_w_up": ffn_w_up, "ffn_w_down": ffn_w_down,
            "ab_w_in": ab_w_in, "ab_conv_w": ab_conv_w, "ab_a_log": ab_a_log, "ab_dt_bias": ab_dt_bias,
            "ab_out_norm": ab_out_norm, "ab_w_out": ab_w_out,
            "cd_w_in": cd_w_in, "cd_sinks": cd_sinks, "cd_w_out": cd_w_out}


def _fwd_reference(x, meta_tokens, norm_gains, ffn_w_gate, ffn_w_up, ffn_w_down,
              ab_w_in, ab_conv_w, ab_a_log, ab_dt_bias, ab_out_norm, ab_w_out,
              cd_w_in, cd_sinks, cd_w_out):
    B = x.shape[0]
    meta = jnp.broadcast_to(meta_tokens[None].astype(x.dtype), (B, N_META, D_MODEL))
    h = jnp.concatenate([meta, x], axis=1)
    for i in range(DEPTH):
        g = norm_gains[i]
        j = i // 2
        y = swiglu(rms_norm(h, g[0]), ffn_w_gate[i, 0], ffn_w_up[i, 0], ffn_w_down[i, 0])
        h = h + 0.5 * rms_norm(y, g[1])
        u = rms_norm(h, g[2])
        if i % 2 == 0:
            y = mixer_ab(u, ab_w_in[j], ab_conv_w[j], ab_a_log[j], ab_dt_bias[j], ab_out_norm[j], ab_w_out[j])
        else:
            y = mixer_cd(u, cd_w_in[j], cd_sinks[j], cd_w_out[j])
        h = h + rms_norm(y, g[3])
        y = swiglu(rms_norm(h, g[4]), ffn_w_gate[i, 1], ffn_w_up[i, 1], ffn_w_down[i, 1])
        h = h + 0.5 * rms_norm(y, g[5])
    return h[:, N_META:]


import jax as _jax
import jax.numpy as _jnp

TWIN_FORMAT = 'train_step'
FWD_PARAMS = ['x', 'meta_tokens', 'norm_gains', 'ffn_w_gate', 'ffn_w_up', 'ffn_w_down', 'ab_w_in', 'ab_conv_w', 'ab_a_log', 'ab_dt_bias', 'ab_out_norm', 'ab_w_out', 'cd_w_in', 'cd_sinks', 'cd_w_out']
TWIN_WEIGHTS = ['meta_tokens', 'norm_gains', 'ffn_w_gate', 'ffn_w_up', 'ffn_w_down', 'ab_w_in', 'ab_conv_w', 'ab_a_log', 'ab_dt_bias', 'ab_out_norm', 'ab_w_out', 'cd_w_in', 'cd_sinks', 'cd_w_out']
TWIN_DIFF_INPUT = 'x'
TWIN_INPUTS = ['x', 'meta_tokens', 'norm_gains', 'ffn_w_gate', 'ffn_w_up', 'ffn_w_down', 'ab_w_in', 'ab_conv_w', 'ab_a_log', 'ab_dt_bias', 'ab_out_norm', 'ab_w_out', 'cd_w_in', 'cd_sinks', 'cd_w_out', 'loss_target', 'm_meta_tokens', 'm_norm_gains', 'm_ffn_w_gate', 'm_ffn_w_up', 'm_ffn_w_down', 'm_ab_w_in', 'm_ab_conv_w', 'm_ab_a_log', 'm_ab_dt_bias', 'm_ab_out_norm', 'm_ab_w_out', 'm_cd_w_in', 'm_cd_sinks', 'm_cd_w_out', 'v_meta_tokens', 'v_norm_gains', 'v_ffn_w_gate', 'v_ffn_w_up', 'v_ffn_w_down', 'v_ab_w_in', 'v_ab_conv_w', 'v_ab_a_log', 'v_ab_dt_bias', 'v_ab_out_norm', 'v_ab_w_out', 'v_cd_w_in', 'v_cd_sinks', 'v_cd_w_out']
TWIN_OUTPUTS = ['loss', 'grad_x', 'grad_meta_tokens', 'grad_norm_gains', 'grad_ffn_w_gate', 'grad_ffn_w_up', 'grad_ffn_w_down', 'grad_ab_w_in', 'grad_ab_conv_w', 'grad_ab_a_log', 'grad_ab_dt_bias', 'grad_ab_out_norm', 'grad_ab_w_out', 'grad_cd_w_in', 'grad_cd_sinks', 'grad_cd_w_out', 'delta_meta_tokens', 'delta_norm_gains', 'delta_ffn_w_gate', 'delta_ffn_w_up', 'delta_ffn_w_down', 'delta_ab_w_in', 'delta_ab_conv_w', 'delta_ab_a_log', 'delta_ab_dt_bias', 'delta_ab_out_norm', 'delta_ab_w_out', 'delta_cd_w_in', 'delta_cd_sinks', 'delta_cd_w_out', 'new_m_meta_tokens', 'new_m_norm_gains', 'new_m_ffn_w_gate', 'new_m_ffn_w_up', 'new_m_ffn_w_down', 'new_m_ab_w_in', 'new_m_ab_conv_w', 'new_m_ab_a_log', 'new_m_ab_dt_bias', 'new_m_ab_out_norm', 'new_m_ab_w_out', 'new_m_cd_w_in', 'new_m_cd_sinks', 'new_m_cd_w_out', 'new_v_meta_tokens', 'new_v_norm_gains', 'new_v_ffn_w_gate', 'new_v_ffn_w_up', 'new_v_ffn_w_down', 'new_v_ab_w_in', 'new_v_ab_conv_w', 'new_v_ab_a_log', 'new_v_ab_dt_bias', 'new_v_ab_out_norm', 'new_v_ab_w_out', 'new_v_cd_w_in', 'new_v_cd_sinks', 'new_v_cd_w_out']
TWIN_LEAF_KINDS = {'loss': 'loss', 'grad_x': 'grad_x', 'grad_meta_tokens': 'grad_w', 'grad_norm_gains': 'grad_w', 'grad_ffn_w_gate': 'grad_w', 'grad_ffn_w_up': 'grad_w', 'grad_ffn_w_down': 'grad_w', 'grad_ab_w_in': 'grad_w', 'grad_ab_conv_w': 'grad_w', 'grad_ab_a_log': 'grad_w', 'grad_ab_dt_bias': 'grad_w', 'grad_ab_out_norm': 'grad_w', 'grad_ab_w_out': 'grad_w', 'grad_cd_w_in': 'grad_w', 'grad_cd_sinks': 'grad_w', 'grad_cd_w_out': 'grad_w', 'delta_meta_tokens': 'delta_w', 'delta_norm_gains': 'delta_w', 'delta_ffn_w_gate': 'delta_w', 'delta_ffn_w_up': 'delta_w', 'delta_ffn_w_down': 'delta_w', 'delta_ab_w_in': 'delta_w', 'delta_ab_conv_w': 'delta_w', 'delta_ab_a_log': 'delta_w', 'delta_ab_dt_bias': 'delta_w', 'delta_ab_out_norm': 'delta_w', 'delta_ab_w_out': 'delta_w', 'delta_cd_w_in': 'delta_w', 'delta_cd_sinks': 'delta_w', 'delta_cd_w_out': 'delta_w', 'new_m_meta_tokens': 'new_m', 'new_m_norm_gains': 'new_m', 'new_m_ffn_w_gate': 'new_m', 'new_m_ffn_w_up': 'new_m', 'new_m_ffn_w_down': 'new_m', 'new_m_ab_w_in': 'new_m', 'new_m_ab_conv_w': 'new_m', 'new_m_ab_a_log': 'new_m', 'new_m_ab_dt_bias': 'new_m', 'new_m_ab_out_norm': 'new_m', 'new_m_ab_w_out': 'new_m', 'new_m_cd_w_in': 'new_m', 'new_m_cd_sinks': 'new_m', 'new_m_cd_w_out': 'new_m', 'new_v_meta_tokens': 'new_v', 'new_v_norm_gains': 'new_v', 'new_v_ffn_w_gate': 'new_v', 'new_v_ffn_w_up': 'new_v', 'new_v_ffn_w_down': 'new_v', 'new_v_ab_w_in': 'new_v', 'new_v_ab_conv_w': 'new_v', 'new_v_ab_a_log': 'new_v', 'new_v_ab_dt_bias': 'new_v', 'new_v_ab_out_norm': 'new_v', 'new_v_ab_w_out': 'new_v', 'new_v_cd_w_in': 'new_v', 'new_v_cd_sinks': 'new_v', 'new_v_cd_w_out': 'new_v'}


def _forward(args):
    return _fwd_reference(*[args[k] for k in FWD_PARAMS])


def _output_shape():
    def fwd():
        inp = _fwd_setup_inputs(0)
        return _fwd_reference(*[inp[k] for k in FWD_PARAMS])
    out = _jax.eval_shape(fwd)
    return out.shape, out.dtype

N_MICROBATCH = 1
ADAM_LR = 0.001
ADAM_B1 = 0.9
ADAM_B2 = 0.999
ADAM_EPS = 1e-08
ADAM_WD = 0.01
ADAM_STEP = 10
PER_EXAMPLE_BATCH_AXIS = {'x': 0, 'loss_target': 0}
SHARED_INPUTS = []
_WEIGHT_DTYPES = {'meta_tokens': _jnp.float32, 'norm_gains': _jnp.float32, 'ffn_w_gate': _jnp.float32, 'ffn_w_up': _jnp.float32, 'ffn_w_down': _jnp.float32, 'ab_w_in': _jnp.float32, 'ab_conv_w': _jnp.float32, 'ab_a_log': _jnp.float32, 'ab_dt_bias': _jnp.float32, 'ab_out_norm': _jnp.float32, 'ab_w_out': _jnp.float32, 'cd_w_in': _jnp.float32, 'cd_sinks': _jnp.float32, 'cd_w_out': _jnp.float32}
MOMENT_SCALE = {'meta_tokens': 1.206357e-01, 'norm_gains': 2.755201e+01, 'ffn_w_gate': 3.489586e-01, 'ffn_w_up': 4.216887e-01, 'ffn_w_down': 7.015839e-01, 'ab_w_in': 8.506991e-01, 'ab_conv_w': 1.553124e+00, 'ab_a_log': 4.295214e+00, 'ab_dt_bias': 4.220004e+00, 'ab_out_norm': 7.440229e+00, 'ab_w_out': 2.623510e+00, 'cd_w_in': 1.193406e+00, 'cd_sinks': 1.305893e-01, 'cd_w_out': 1.811937e+00}


def _to_microbatches(a, axis):
    t = _jnp.moveaxis(a, axis, 0)
    t = t.reshape((N_MICROBATCH, t.shape[0] // N_MICROBATCH) + t.shape[1:])
    return _jnp.moveaxis(t, 1, axis + 1)


def setup_inputs(seed: int = 0) -> dict:
    inp = _fwd_setup_inputs(seed)
    key = _jax.random.fold_in(_jax.random.key(seed), 7919)
    shape, _ = _output_shape()
    out = dict(inp)
    out["loss_target"] = _jax.random.normal(_jax.random.fold_in(key, 0), shape, _jnp.float32)
    for i, name in enumerate(TWIN_WEIGHTS):
        w = inp[name].astype(_jnp.float32)
        if MOMENT_SCALE is None:
            s = _jnp.sqrt(_jnp.mean(_jnp.square(w)) + 1e-30)
        else:
            s = MOMENT_SCALE[name]
        km, kv = _jax.random.split(_jax.random.fold_in(key, i + 1))
        out[name] = w
        out["m_" + name] = s * _jax.random.normal(km, w.shape, _jnp.float32)
        out["v_" + name] = (s * s) * _jax.random.uniform(kv, w.shape, _jnp.float32, 0.5, 1.5)
    if N_MICROBATCH > 1:
        for name, axis in PER_EXAMPLE_BATCH_AXIS.items():
            out[name] = _to_microbatches(out[name], axis)
    return {'x': out['x'], 'meta_tokens': out['meta_tokens'], 'norm_gains': out['norm_gains'], 'ffn_w_gate': out['ffn_w_gate'], 'ffn_w_up': out['ffn_w_up'], 'ffn_w_down': out['ffn_w_down'], 'ab_w_in': out['ab_w_in'], 'ab_conv_w': out['ab_conv_w'], 'ab_a_log': out['ab_a_log'], 'ab_dt_bias': out['ab_dt_bias'], 'ab_out_norm': out['ab_out_norm'], 'ab_w_out': out['ab_w_out'], 'cd_w_in': out['cd_w_in'], 'cd_sinks': out['cd_sinks'], 'cd_w_out': out['cd_w_out'], 'loss_target': out['loss_target'], 'm_meta_tokens': out['m_meta_tokens'], 'm_norm_gains': out['m_norm_gains'], 'm_ffn_w_gate': out['m_ffn_w_gate'], 'm_ffn_w_up': out['m_ffn_w_up'], 'm_ffn_w_down': out['m_ffn_w_down'], 'm_ab_w_in': out['m_ab_w_in'], 'm_ab_conv_w': out['m_ab_conv_w'], 'm_ab_a_log': out['m_ab_a_log'], 'm_ab_dt_bias': out['m_ab_dt_bias'], 'm_ab_out_norm': out['m_ab_out_norm'], 'm_ab_w_out': out['m_ab_w_out'], 'm_cd_w_in': out['m_cd_w_in'], 'm_cd_sinks': out['m_cd_sinks'], 'm_cd_w_out': out['m_cd_w_out'], 'v_meta_tokens': out['v_meta_tokens'], 'v_norm_gains': out['v_norm_gains'], 'v_ffn_w_gate': out['v_ffn_w_gate'], 'v_ffn_w_up': out['v_ffn_w_up'], 'v_ffn_w_down': out['v_ffn_w_down'], 'v_ab_w_in': out['v_ab_w_in'], 'v_ab_conv_w': out['v_ab_conv_w'], 'v_ab_a_log': out['v_ab_a_log'], 'v_ab_dt_bias': out['v_ab_dt_bias'], 'v_ab_out_norm': out['v_ab_out_norm'], 'v_ab_w_out': out['v_ab_w_out'], 'v_cd_w_in': out['v_cd_w_in'], 'v_cd_sinks': out['v_cd_sinks'], 'v_cd_w_out': out['v_cd_w_out']}


def _loss(weights, diff, rest, loss_target):
    with _jax.named_scope("forward"):
        args = {**rest, TWIN_DIFF_INPUT: diff, **{k: w.astype(_WEIGHT_DTYPES[k]) for k, w in weights.items()}}
        y = _forward(args)
    with _jax.named_scope("loss_head"):
        err = _jnp.square(y.astype(_jnp.float32) - loss_target)
        return 0.5 * _jnp.sum(_jnp.mean(err, axis=-1)) if err.ndim else 0.5 * err


def _adamw(w, g, m, v):
    m = ADAM_B1 * m + (1.0 - ADAM_B1) * g
    v = ADAM_B2 * v + (1.0 - ADAM_B2) * _jnp.square(g)
    m_hat = m / (1.0 - ADAM_B1 ** ADAM_STEP)
    v_hat = v / (1.0 - ADAM_B2 ** ADAM_STEP)
    delta = -ADAM_LR * (m_hat / (_jnp.sqrt(v_hat) + ADAM_EPS) + ADAM_WD * w)
    return delta, m, v


def reference(x, meta_tokens, norm_gains, ffn_w_gate, ffn_w_up, ffn_w_down, ab_w_in, ab_conv_w, ab_a_log, ab_dt_bias, ab_out_norm, ab_w_out, cd_w_in, cd_sinks, cd_w_out, loss_target, m_meta_tokens, m_norm_gains, m_ffn_w_gate, m_ffn_w_up, m_ffn_w_down, m_ab_w_in, m_ab_conv_w, m_ab_a_log, m_ab_dt_bias, m_ab_out_norm, m_ab_w_out, m_cd_w_in, m_cd_sinks, m_cd_w_out, v_meta_tokens, v_norm_gains, v_ffn_w_gate, v_ffn_w_up, v_ffn_w_down, v_ab_w_in, v_ab_conv_w, v_ab_a_log, v_ab_dt_bias, v_ab_out_norm, v_ab_w_out, v_cd_w_in, v_cd_sinks, v_cd_w_out):
    given = dict(x=x, meta_tokens=meta_tokens, norm_gains=norm_gains, ffn_w_gate=ffn_w_gate, ffn_w_up=ffn_w_up, ffn_w_down=ffn_w_down, ab_w_in=ab_w_in, ab_conv_w=ab_conv_w, ab_a_log=ab_a_log, ab_dt_bias=ab_dt_bias, ab_out_norm=ab_out_norm, ab_w_out=ab_w_out, cd_w_in=cd_w_in, cd_sinks=cd_sinks, cd_w_out=cd_w_out, loss_target=loss_target, m_meta_tokens=m_meta_tokens, m_norm_gains=m_norm_gains, m_ffn_w_gate=m_ffn_w_gate, m_ffn_w_up=m_ffn_w_up, m_ffn_w_down=m_ffn_w_down, m_ab_w_in=m_ab_w_in, m_ab_conv_w=m_ab_conv_w, m_ab_a_log=m_ab_a_log, m_ab_dt_bias=m_ab_dt_bias, m_ab_out_norm=m_ab_out_norm, m_ab_w_out=m_ab_w_out, m_cd_w_in=m_cd_w_in, m_cd_sinks=m_cd_sinks, m_cd_w_out=m_cd_w_out, v_meta_tokens=v_meta_tokens, v_norm_gains=v_norm_gains, v_ffn_w_gate=v_ffn_w_gate, v_ffn_w_up=v_ffn_w_up, v_ffn_w_down=v_ffn_w_down, v_ab_w_in=v_ab_w_in, v_ab_conv_w=v_ab_conv_w, v_ab_a_log=v_ab_a_log, v_ab_dt_bias=v_ab_dt_bias, v_ab_out_norm=v_ab_out_norm, v_ab_w_out=v_ab_w_out, v_cd_w_in=v_cd_w_in, v_cd_sinks=v_cd_sinks, v_cd_w_out=v_cd_w_out)
    weights = {n: given[n] for n in TWIN_WEIGHTS}
    shared = {n: given[n] for n in SHARED_INPUTS}
    per_example = {n: given[n] for n in ['x']}
    grad_fn = _jax.value_and_grad(_loss, argnums=(0, 1))

    def one_microbatch(ex, loss_target):
        ex = dict(ex)
        diff = ex.pop(TWIN_DIFF_INPUT)
        return grad_fn(weights, diff, {**shared, **ex}, loss_target)

    if N_MICROBATCH == 1:
        loss, (grad_w, grad_x) = one_microbatch(per_example, given["loss_target"])
    else:
        def body(carry, xs):
            loss_sum, grad_sum = carry
            l_k, (gw_k, gx_k) = one_microbatch(xs[0], xs[1])
            with _jax.named_scope("update"):
                return (loss_sum + l_k, _jax.tree.map(_jnp.add, grad_sum, gw_k)), gx_k

        init = (_jnp.zeros((), _jnp.float32), _jax.tree.map(_jnp.zeros_like, weights))
        (loss, grad_w), grad_x = _jax.lax.scan(body, init, (per_example, given["loss_target"]))
    with _jax.named_scope("update"):
        delta_w, new_m, new_v = {}, {}, {}
        for n in TWIN_WEIGHTS:
            delta_w[n], new_m[n], new_v[n] = _adamw(weights[n], grad_w[n], given["m_" + n], given["v_" + n])
    return (loss, grad_x, *[grad_w[n] for n in TWIN_WEIGHTS], *[delta_w[n] for n in TWIN_WEIGHTS],
            *[new_m[n] for n in TWIN_WEIGHTS], *[new_v[n] for n in TWIN_WEIGHTS])
```

```python
import functools

import numpy as np
import jax
import jax.numpy as jnp
from jax import lax
from jax.experimental import pallas as pl
from jax.experimental.pallas import tpu as pltpu

f32 = jnp.float32
bf16 = jnp.bfloat16

EPS = 1e-6
D = 1024
N_META = 16
PAD = 112
BLK = 128
GDN_C = 64
N_HEAD_AB = 4
DH_AB = 128
AB_IN = 4104
AB_IN_P = 4224
CD_IN = 2304
ADAM_LR, ADAM_B1, ADAM_B2, ADAM_EPS, ADAM_WD, ADAM_STEP = 0.001, 0.9, 0.999, 1e-08, 0.01, 10
VMEM_LIMIT = 56 * 1024 * 1024
NEG = -1e30
SB_EXIT = -104.0

_NT = (((1,), (1,)), ((), ()))
_TN = (((0,), (0,)), ((), ()))


def _cparams(**kw):
    return pltpu.CompilerParams(vmem_limit_bytes=VMEM_LIMIT, **kw)


def _dot(a, b):
    return jnp.dot(a.astype(bf16), b.astype(bf16), preferred_element_type=f32)


def _dot_nt(a, b):
    return lax.dot_general(a.astype(bf16), b.astype(bf16), _NT, preferred_element_type=f32)


def _dot_tn(a, b):
    return lax.dot_general(a.astype(bf16), b.astype(bf16), _TN, preferred_element_type=f32)


def _dot_hi(a, b):
    return jnp.dot(a, b, preferred_element_type=f32, precision=lax.Precision.HIGHEST)


def _dot2(a, b01):
    hi = a.astype(bf16)
    lo = (a - hi.astype(f32)).astype(bf16)
    return jnp.dot(hi, b01, preferred_element_type=f32) + jnp.dot(lo, b01, preferred_element_type=f32)


def _row_tile(t):
    for c in (640, 512, 256, 128):
        if t % c == 0:
            return c
    raise ValueError(t)


def _sigmoid(x):
    return 1.0 / (1.0 + jnp.exp(-x))


def _silu(x):
    return x * _sigmoid(x)


def _softplus(x):
    return jnp.maximum(x, 0.0) + jnp.log(1.0 + jnp.exp(-jnp.abs(x)))


def _rms(x, g):
    r = lax.rsqrt(jnp.mean(x * x, axis=-1, keepdims=True) + EPS)
    return x * r * g


def _rms_bwd(x, g, dy):
    r = lax.rsqrt(jnp.mean(x * x, axis=-1, keepdims=True) + EPS)
    xh = x * r
    dg = jnp.sum(dy * xh, axis=0, keepdims=True)
    dxh = dy * g
    dx = r * (dxh - xh * jnp.mean(dxh * xh, axis=-1, keepdims=True))
    return dx, dg


def _zero_pad_rows(v, i, tr):
    rows = i * tr + lax.broadcasted_iota(jnp.int32, (tr, 1), 0)
    return jnp.where(rows >= PAD, v, 0.0)


def _acc8(ref, row, first):
    @pl.when(first)
    def _():
        ref[...] = jnp.zeros_like(ref)
    ref[...] += jnp.broadcast_to(row, ref.shape)


def ffn_up(h, g, wg, wu):
    T, Dm = h.shape
    S, _, F = wg.shape
    tm = _row_tile(T)

    def body(h_ref, g_ref, wg_ref, wu_ref, u_ref, G_ref, U_ref, a_ref):
        @pl.when(pl.program_id(1) == 0)
        def _():
            u_ref[...] = _rms(h_ref[...], g_ref[...]).astype(u_ref.dtype)
        u = u_ref[...]
        G = _dot(u, wg_ref[...])
        U = _dot(u, wu_ref[...])
        G_ref[...] = G.astype(G_ref.dtype)
        U_ref[...] = U.astype(U_ref.dtype)
        a_ref[...] = (_silu(G) * U).astype(a_ref.dtype)

    act = jax.ShapeDtypeStruct((S, T, F), bf16)
    wspec = pl.BlockSpec((None, Dm, F), lambda i, s: (s, 0, 0))
    aspec = pl.BlockSpec((None, tm, F), lambda i, s: (s, i, 0))
    return pl.pallas_call(
        body, name="ffn_up", grid=(T // tm, S),
        in_specs=[pl.BlockSpec((tm, Dm), lambda i, s: (i, 0)), pl.BlockSpec((1, Dm), lambda i, s: (0, 0)), wspec, wspec],
        out_specs=[pl.BlockSpec((tm, Dm), lambda i, s: (i, 0)), aspec, aspec, aspec],
        out_shape=[jax.ShapeDtypeStruct((T, Dm), bf16), act, act, act],
        compiler_params=_cparams(dimension_semantics=("arbitrary", "arbitrary")),
    )(h, g, wg, wu)


def norm_proj(h, g, w, tn):
    T, Dm = h.shape
    N = w.shape[1]
    tm = _row_tile(T)

    def body(h_ref, g_ref, w_ref, u_ref, p_ref):
        @pl.when(pl.program_id(1) == 0)
        def _():
            u_ref[...] = _rms(h_ref[...], g_ref[...]).astype(u_ref.dtype)
        p_ref[...] = _dot(u_ref[...], w_ref[...])

    return pl.pallas_call(
        body, name="norm_proj", grid=(T // tm, N // tn),
        in_specs=[pl.BlockSpec((tm, Dm), lambda i, j: (i, 0)), pl.BlockSpec((1, Dm), lambda i, j: (0, 0)),
                  pl.BlockSpec((Dm, tn), lambda i, j: (0, j))],
        out_specs=[pl.BlockSpec((tm, Dm), lambda i, j: (i, 0)), pl.BlockSpec((tm, tn), lambda i, j: (i, j))],
        out_shape=[jax.ShapeDtypeStruct((T, Dm), bf16), jax.ShapeDtypeStruct((T, N), f32)],
        compiler_params=_cparams(dimension_semantics=("arbitrary", "arbitrary")),
    )(h, g, w)


def proj_norm_res(a, w, h, g, coef):
    S, T, F = a.shape
    Dm = w.shape[2]
    tm = _row_tile(T)

    def body(a_ref, w_ref, h_ref, g_ref, y_ref, o_ref, acc):
        s = pl.program_id(1)

        @pl.when(s == 0)
        def _():
            acc[...] = jnp.zeros_like(acc)
        acc[...] += _dot(a_ref[...], w_ref[...])

        @pl.when(s == S - 1)
        def _():
            y = acc[...]
            y_ref[...] = y
            o_ref[...] = h_ref[...] + coef * _rms(y, g_ref[...])

    row = pl.BlockSpec((tm, Dm), lambda i, s: (i, 0))
    return pl.pallas_call(
        body, name="proj_norm_res", grid=(T // tm, S),
        in_specs=[pl.BlockSpec((None, tm, F), lambda i, s: (s, i, 0)), pl.BlockSpec((None, F, Dm), lambda i, s: (s, 0, 0)),
                  row, pl.BlockSpec((1, Dm), lambda i, s: (0, 0))],
        out_specs=[row, row],
        out_shape=[jax.ShapeDtypeStruct((T, Dm), f32), jax.ShapeDtypeStruct((T, Dm), f32)],
        scratch_shapes=[pltpu.VMEM((tm, Dm), f32)],
        compiler_params=_cparams(dimension_semantics=("arbitrary", "arbitrary")),
    )(a, w, h, g)


def post_norm_bwd(y, g, dz, coef):
    T, Dm = y.shape
    tm = _row_tile(T)

    def body(y_ref, g_ref, dz_ref, dy_ref, dg_ref):
        dy, dg = _rms_bwd(y_ref[...], g_ref[...], coef * dz_ref[...])
        dy_ref[...] = _zero_pad_rows(dy, pl.program_id(0), tm).astype(dy_ref.dtype)
        _acc8(dg_ref, dg, pl.program_id(0) == 0)

    row = pl.BlockSpec((tm, Dm), lambda i: (i, 0))
    return pl.pallas_call(
        body, name="post_norm_bwd", grid=(T // tm,),
        in_specs=[row, pl.BlockSpec((1, Dm), lambda i: (0, 0)), row],
        out_specs=[row, pl.BlockSpec((8, Dm), lambda i: (0, 0))],
        out_shape=[jax.ShapeDtypeStruct((T, Dm), bf16), jax.ShapeDtypeStruct((8, Dm), f32)],
        compiler_params=_cparams(dimension_semantics=("arbitrary",)),
    )(y, g, dz)


def ffn_bwd_act(dy, wd, G, U):
    T, Dm = dy.shape
    S, F, _ = wd.shape
    tm = _row_tile(T)

    def body(dy_ref, w_ref, G_ref, U_ref, dG_ref, dU_ref):
        da = _dot_nt(dy_ref[...], w_ref[...])
        Gv = G_ref[...].astype(f32)
        Uv = U_ref[...].astype(f32)
        sg = _sigmoid(Gv)
        dU_ref[...] = (da * Gv * sg).astype(dU_ref.dtype)
        dG_ref[...] = (da * Uv * sg * (1.0 + Gv * (1.0 - sg))).astype(dG_ref.dtype)

    aspec = pl.BlockSpec((None, tm, F), lambda i, s: (s, i, 0))
    act = jax.ShapeDtypeStruct((S, T, F), bf16)
    return pl.pallas_call(
        body, name="ffn_bwd_act", grid=(T // tm, S),
        in_specs=[pl.BlockSpec((tm, Dm), lambda i, s: (i, 0)), pl.BlockSpec((None, F, Dm), lambda i, s: (s, 0, 0)), aspec, aspec],
        out_specs=[aspec, aspec], out_shape=[act, act],
        compiler_params=_cparams(dimension_semantics=("arbitrary", "arbitrary")),
    )(dy, wd, G, U)


def mm_nt_norm_bwd(pairs, h, g, dres, ksplit=1):
    S, T, K = pairs[0][0].shape
    assert S == 1 or ksplit == 1
    steps = S * ksplit
    tk = K // ksplit
    Dm = h.shape[1]
    tm = _row_tile(T)
    n = len(pairs)

    def body(*refs):
        ab = refs[:2 * n]
        h_ref, g_ref, dres_ref, dh_ref, dg_ref, acc = refs[2 * n:]
        i, s = pl.program_id(0), pl.program_id(1)

        @pl.when(s == 0)
        def _():
            acc[...] = jnp.zeros_like(acc)
        for p in range(n):
            acc[...] += _dot_nt(ab[2 * p][...], ab[2 * p + 1][...])

        @pl.when(s == steps - 1)
        def _():
            dx, dg = _rms_bwd(h_ref[...], g_ref[...], acc[...])
            dh_ref[...] = _zero_pad_rows(dres_ref[...] + dx, i, tm)
            _acc8(dg_ref, dg, i == 0)

    row = pl.BlockSpec((tm, Dm), lambda i, s: (i, 0))
    if S > 1:
        amap, bmap = (lambda i, s: (s, i, 0)), (lambda i, s: (s, 0, 0))
    else:
        amap, bmap = (lambda i, s: (0, i, s)), (lambda i, s: (0, 0, s))
    in_specs, args = [], []
    for a, b in pairs:
        in_specs += [pl.BlockSpec((None, tm, tk), amap), pl.BlockSpec((None, Dm, tk), bmap)]
        args += [a, b]
    return pl.pallas_call(
        body, name="mm_nt_norm_bwd", grid=(T // tm, steps),
        in_specs=in_specs + [row, pl.BlockSpec((1, Dm), lambda i, s: (0, 0)), row],
        out_specs=[row, pl.BlockSpec((8, Dm), lambda i, s: (0, 0))],
        out_shape=[jax.ShapeDtypeStruct((T, Dm), f32), jax.ShapeDtypeStruct((8, Dm), f32)],
        scratch_shapes=[pltpu.VMEM((tm, Dm), f32)],
        compiler_params=_cparams(dimension_semantics=("arbitrary", "arbitrary")),
    )(*args, h, g, dres)


def mm_nt(a, b):
    T, K = a.shape
    N = b.shape[0]
    tm = _row_tile(T)

    def body(a_ref, b_ref, o_ref):
        o_ref[...] = _dot_nt(a_ref[...], b_ref[...])

    return pl.pallas_call(
        body, name="mm_nt", grid=(T // tm,),
        in_specs=[pl.BlockSpec((tm, K), lambda i: (i, 0)), pl.BlockSpec((N, K), lambda i: (0, 0))],
        out_specs=pl.BlockSpec((tm, N), lambda i: (i, 0)),
        out_shape=jax.ShapeDtypeStruct((T, N), f32),
        compiler_params=_cparams(dimension_semantics=("arbitrary",)),
    )(a, b)


def mm_tn(a, b, tn):
    Sa, T, M = a.shape
    Sb, _, N = b.shape
    S = max(Sa, Sb)
    tk = _row_tile(T)

    def body(a_ref, b_ref, o_ref):
        @pl.when(pl.program_id(2) == 0)
        def _():
            o_ref[...] = jnp.zeros_like(o_ref)
        o_ref[...] += _dot_tn(a_ref[...], b_ref[...])

    return pl.pallas_call(
        body, name="mm_tn", grid=(S, N // tn, T // tk),
        in_specs=[pl.BlockSpec((None, tk, M), (lambda s, j, k: (s, k, 0)) if Sa > 1 else (lambda s, j, k: (0, k, 0))),
                  pl.BlockSpec((None, tk, tn), (lambda s, j, k: (s, k, j)) if Sb > 1 else (lambda s, j, k: (0, k, j)))],
        out_specs=pl.BlockSpec((None, M, tn), lambda s, j, k: (s, 0, j)),
        out_shape=jax.ShapeDtypeStruct((S, M, N), f32),
        compiler_params=_cparams(dimension_semantics=("arbitrary", "arbitrary", "arbitrary")),
    )(a, b)


def adamw(w, g, m, v):
    R, C = w.shape
    tr = 512 if R % 512 == 0 else (256 if R % 256 == 0 else R)
    c1 = np.float32(1.0 - ADAM_B1 ** ADAM_STEP)
    c2 = np.float32(1.0 - ADAM_B2 ** ADAM_STEP)

    def body(w_ref, g_ref, m_ref, v_ref, d_ref, mo_ref, vo_ref):
        gv = g_ref[...]
        mn = ADAM_B1 * m_ref[...] + (1.0 - ADAM_B1) * gv
        vn = ADAM_B2 * v_ref[...] + (1.0 - ADAM_B2) * (gv * gv)
        mo_ref[...] = mn
        vo_ref[...] = vn
        d_ref[...] = -ADAM_LR * ((mn / c1) / (jnp.sqrt(vn / c2) + ADAM_EPS) + ADAM_WD * w_ref[...])

    spec = pl.BlockSpec((tr, C), lambda i: (i, 0))
    sh = jax.ShapeDtypeStruct((R, C), f32)
    return pl.pallas_call(
        body, name="adamw", grid=(R // tr,), in_specs=[spec] * 4, out_specs=[spec] * 3, out_shape=[sh] * 3,
        compiler_params=_cparams(dimension_semantics=("arbitrary",)),
    )(w, g, m, v)


_RET_LOG_GAMMA = [float(v) for v in np.log1p(-np.exp2(-5.0 - np.arange(N_HEAD_AB, dtype=np.float32))).astype(np.float32)]


def rot_tables(T):
    pos = jnp.arange(T, dtype=f32) - float(PAD)
    inv_freq = 1.0 / (10000.0 ** jnp.linspace(0.0, 1.0, DH_AB // 2, dtype=f32))
    ang = pos[:, None] * inv_freq[None, :]
    cos, sin = jnp.cos(ang), jnp.sin(ang)
    return jnp.repeat(cos, 2, axis=1), jnp.stack([-sin, sin], axis=-1).reshape(T, DH_AB)


def _swap_pairs(x):
    lane = lax.broadcasted_iota(jnp.int32, x.shape, 1)
    return jnp.where(lane % 2 == 0, pltpu.roll(x, x.shape[1] - 1, 1), pltpu.roll(x, 1, 1))


def _rot(x, c, s):
    return x * c + _swap_pairs(x) * s


def _rot_bwd(d, c, s):
    return d * c + _swap_pairs(d * s)


def _ret_mats(lg):
    i = lax.broadcasted_iota(jnp.int32, (BLK, BLK), 0).astype(f32)
    j = lax.broadcasted_iota(jnp.int32, (BLK, BLK), 1).astype(f32)
    diff = i - j
    decay = jnp.where(diff >= 0, jnp.exp(jnp.maximum(diff, 0.0) * lg), 0.0)
    xi = jnp.exp((i + 1.0) * lg)
    zeta = jnp.exp((BLK - 1.0 - i) * lg)
    return decay, xi, zeta, float(np.exp(np.float32(BLK * lg)))


def _ret_head(q_ref, k_ref, v_ref, cos, sin, h, Sp):
    sl = slice(DH_AB * h, DH_AB * (h + 1))
    decay, xi, zeta, gc = _ret_mats(_RET_LOG_GAMMA[h])
    q = _rot(q_ref[:, sl], cos, sin)
    k = _rot(k_ref[:, sl], cos, sin) * DH_AB ** -0.5
    v = v_ref[:, sl]
    P = _dot_nt(q, k) * decay
    ret = _dot(P, v) + _dot(q * xi, Sp)
    return sl, q, k, v, P, ret, decay, xi, zeta, gc


def _proj_spec(rows, width, col, rev=None):
    if rev is None:
        return pl.BlockSpec((rows, width), lambda n: (n, col))
    return pl.BlockSpec((rows, width), lambda n: (rev - n, col))


def ret_fwd(proj, cos, sin):
    T = proj.shape[0]
    NC = T // BLK
    W = N_HEAD_AB * DH_AB

    def body(q_ref, k_ref, v_ref, g_ref, cos_ref, sin_ref, o_ref, sall_ref, S):
        @pl.when(pl.program_id(0) == 0)
        def _():
            S[...] = jnp.zeros_like(S)
        cos_v, sin_v = cos_ref[...], sin_ref[...]
        for h in range(N_HEAD_AB):
            Sp = S[h]
            sall_ref[0, h] = Sp
            sl, q, k, v, P, ret, decay, xi, zeta, gc = _ret_head(q_ref, k_ref, v_ref, cos_v, sin_v, h, Sp)
            S[h] = Sp * gc + _dot_tn(k * zeta, v)
            mu = jnp.mean(ret, axis=-1, keepdims=True)
            cen = ret - mu
            y = cen * lax.rsqrt(jnp.mean(cen * cen, axis=-1, keepdims=True) + EPS)
            o_ref[:, sl] = (y * _silu(g_ref[:, sl])).astype(o_ref.dtype)

    tab = pl.BlockSpec((BLK, DH_AB), lambda n: (n, 0))
    return pl.pallas_call(
        body, name="ret_fwd", grid=(NC,),
        in_specs=[_proj_spec(BLK, W, 0), _proj_spec(BLK, W, 1), _proj_spec(BLK, W, 2), _proj_spec(BLK, W, 3), tab, tab],
        out_specs=[pl.BlockSpec((BLK, W), lambda n: (n, 0)), pl.BlockSpec((1, N_HEAD_AB, DH_AB, DH_AB), lambda n: (n, 0, 0, 0))],
        out_shape=[jax.ShapeDtypeStruct((T, W), bf16), jax.ShapeDtypeStruct((NC, N_HEAD_AB, DH_AB, DH_AB), f32)],
        scratch_shapes=[pltpu.VMEM((N_HEAD_AB, DH_AB, DH_AB), f32)],
        compiler_params=_cparams(dimension_semantics=("arbitrary",)),
    )(proj, proj, proj, proj, cos, sin)


def ret_bwd(proj, cos, sin, sall, dmix):
    T = proj.shape[0]
    NC = T // BLK
    W = N_HEAD_AB * DH_AB
    L = NC - 1

    def body(q_ref, k_ref, v_ref, g_ref, cos_ref, sin_ref, sall_ref, do_ref, dq_ref, dk_ref, dv_ref, dg_ref, dS):
        @pl.when(pl.program_id(0) == 0)
        def _():
            dS[...] = jnp.zeros_like(dS)
        cos_v, sin_v = cos_ref[...], sin_ref[...]
        for h in range(N_HEAD_AB):
            Sp = sall_ref[0, h]
            sl, q, k, v, P, ret, decay, xi, zeta, gc = _ret_head(q_ref, k_ref, v_ref, cos_v, sin_v, h, Sp)
            mu = jnp.mean(ret, axis=-1, keepdims=True)
            cen = ret - mu
            r = lax.rsqrt(jnp.mean(cen * cen, axis=-1, keepdims=True) + EPS)
            y = cen * r
            gate = g_ref[:, sl]
            sg = _sigmoid(gate)
            dout = do_ref[:, sl]
            dg_ref[:, sl] = dout * y * (sg * (1.0 + gate * (1.0 - sg)))
            dy = dout * (gate * sg)
            dO = r * (dy - jnp.mean(dy, axis=-1, keepdims=True) - y * jnp.mean(dy * y, axis=-1, keepdims=True))
            dSn = dS[h]
            dv_ref[:, sl] = _dot_tn(P, dO) + _dot(k * zeta, dSn)
            dP = _dot_nt(dO, v) * decay
            dq = _dot(dP, k) + _dot_nt(dO, Sp) * xi
            dk = _dot_tn(dP, q) + _dot_nt(v, dSn) * zeta
            dS[h] = dSn * gc + _dot_tn(q * xi, dO)
            dq_ref[:, sl] = _rot_bwd(dq, cos_v, sin_v)
            dk_ref[:, sl] = _rot_bwd(dk * DH_AB ** -0.5, cos_v, sin_v)

    tab = pl.BlockSpec((BLK, DH_AB), lambda n: (L - n, 0))
    out = pl.BlockSpec((BLK, W), lambda n: (L - n, 0))
    sh = jax.ShapeDtypeStruct((T, W), f32)
    return pl.pallas_call(
        body, name="ret_bwd", grid=(NC,),
        in_specs=[_proj_spec(BLK, W, 0, L), _proj_spec(BLK, W, 1, L), _proj_spec(BLK, W, 2, L), _proj_spec(BLK, W, 3, L), tab, tab,
                  pl.BlockSpec((1, N_HEAD_AB, DH_AB, DH_AB), lambda n: (L - n, 0, 0, 0)), _proj_spec(BLK, W, 0, L)],
        out_specs=[out] * 4, out_shape=[sh] * 4,
        scratch_shapes=[pltpu.VMEM((N_HEAD_AB, DH_AB, DH_AB), f32)],
        compiler_params=_cparams(dimension_semantics=("arbitrary",)),
    )(proj, proj, proj, proj, cos, sin, sall, dmix)


HALO = 8
GDN_K = 4
W_AB = N_HEAD_AB * DH_AB


def _gdn_rowwise(cq, ck, cv, gblk, alog, dtb, rmask):
    def l2n(x):
        return [x[:, DH_AB * h:DH_AB * (h + 1)] for h in range(N_HEAD_AB)]

    def norm(x):
        return x * lax.rsqrt(jnp.sum(x * x, axis=-1, keepdims=True) + EPS)

    qs = [norm(x) for x in l2n(_silu(cq))]
    ks = [norm(x) for x in l2n(_silu(ck))]
    lane = lax.broadcasted_iota(jnp.int32, gblk.shape, 1)
    beta = _sigmoid(gblk)
    g = -jnp.exp(alog) * _softplus(gblk + dtb)
    gates = jnp.where(lane < N_HEAD_AB, beta, jnp.where(lane < 2 * N_HEAD_AB, g, 0.0)) * rmask
    return qs, ks, _silu(cv), gates


def _row_mask(i, tr):
    rows = i * tr + lax.broadcasted_iota(jnp.int32, (tr, 1), 0)
    return (rows >= PAD).astype(f32)


def _conv_specs(tr, cols, nt, nxt=False):
    tiles = [pl.BlockSpec((tr, W_AB), functools.partial(lambda i, c: (i, c), c=c)) for c in cols]
    r = tr // HALO
    if nxt:
        halos = [pl.BlockSpec((HALO, W_AB), functools.partial(lambda i, c: (jnp.minimum((i + 1) * r, nt * r - 1), c), c=c)) for c in cols]
    else:
        halos = [pl.BlockSpec((HALO, W_AB), functools.partial(lambda i, c: (jnp.maximum(i * r - 1, 0), c), c=c)) for c in cols]
    return tiles, halos


def gdn_prep_fwd(proj, conv_w, alog, dtb):
    T = proj.shape[0]
    tr = BLK
    NT = T // tr

    def body(xq, xk, xv, hq, hk, hv, gb_ref, w_ref, al_ref, dt_ref, cq_o, ck_o, cv_o, q_o, k_o, v_o, gates_o, buf):
        i = pl.program_id(0)
        cs = []
        for p, (x_ref, h_ref, c_o) in enumerate(((xq, hq, cq_o), (xk, hk, ck_o), (xv, hv, cv_o))):
            buf[0:HALO, :] = jnp.where(i > 0, h_ref[...], 0.0)
            buf[HALO:, :] = x_ref[...]
            c = jnp.zeros((tr, W_AB), f32)
            for k in range(GDN_K):
                c = c + w_ref[k:k + 1, W_AB * p:W_AB * (p + 1)] * buf[pl.ds(HALO - GDN_K + 1 + k, tr), :]
            c_o[...] = c
            cs.append(c)
        qs, ks, v, gates = _gdn_rowwise(cs[0], cs[1], cs[2], gb_ref[...], al_ref[...], dt_ref[...], _row_mask(i, tr))
        for h in range(N_HEAD_AB):
            q_o[:, DH_AB * h:DH_AB * (h + 1)] = qs[h]
            k_o[:, DH_AB * h:DH_AB * (h + 1)] = ks[h]
        v_o[...] = v
        gates_o[...] = gates

    tiles, halos = _conv_specs(tr, (4, 5, 6), NT)
    vec = pl.BlockSpec((1, BLK), lambda i: (0, 0))
    wide = pl.BlockSpec((tr, W_AB), lambda i: (i, 0))
    sh = jax.ShapeDtypeStruct((T, W_AB), f32)
    return pl.pallas_call(
        body, name="gdn_prep_fwd", grid=(NT,),
        in_specs=tiles + halos + [pl.BlockSpec((tr, BLK), lambda i: (i, AB_IN_P // BLK - 1)),
                                  pl.BlockSpec((GDN_K, 3 * W_AB), lambda i: (0, 0)), vec, vec],
        out_specs=[wide] * 6 + [pl.BlockSpec((tr, BLK), lambda i: (i, 0))],
        out_shape=[sh] * 6 + [jax.ShapeDtypeStruct((T, BLK), f32)],
        scratch_shapes=[pltpu.VMEM((tr + HALO, W_AB), f32)],
        compiler_params=_cparams(dimension_semantics=("arbitrary",)),
    )(proj, proj, proj, proj, proj, proj, proj, conv_w, alog, dtb)


def gdn_prep_bwd(cq, ck, cv, proj, alog, dtb, dq, dk, dv, dgates):
    T = cq.shape[0]
    tr = BLK
    NT = T // tr

    def body(cq_r, ck_r, cv_r, gb_ref, al_ref, dt_ref, dq_r, dk_r, dv_r, dg_r, dcq_o, dck_o, dcv_o, dgb_o, dal_o, ddt_o):
        i = pl.program_id(0)
        mask = _row_mask(i, tr)
        _, vjp = jax.vjp(lambda a, b, c, d, e, f: _gdn_rowwise(a, b, c, d, e, f, mask),
                         cq_r[...], ck_r[...], cv_r[...], gb_ref[...], al_ref[...], dt_ref[...])
        heads = lambda r: [r[:, DH_AB * h:DH_AB * (h + 1)] for h in range(N_HEAD_AB)]
        dcq, dck, dcv, dgb, dal, ddt = vjp((heads(dq_r), heads(dk_r), dv_r[...], dg_r[...]))
        dcq_o[...] = dcq
        dck_o[...] = dck
        dcv_o[...] = dcv
        dgb_o[...] = dgb
        _acc8(dal_o, dal, i == 0)
        _acc8(ddt_o, ddt, i == 0)

    vec = pl.BlockSpec((1, BLK), lambda i: (0, 0))
    wide = pl.BlockSpec((tr, W_AB), lambda i: (i, 0))
    narrow = pl.BlockSpec((tr, BLK), lambda i: (i, 0))
    acc = pl.BlockSpec((8, BLK), lambda i: (0, 0))
    sh = jax.ShapeDtypeStruct((T, W_AB), f32)
    return pl.pallas_call(
        body, name="gdn_prep_bwd", grid=(NT,),
        in_specs=[wide] * 3 + [pl.BlockSpec((tr, BLK), lambda i: (i, AB_IN_P // BLK - 1)), vec, vec] + [wide] * 3 + [narrow],
        out_specs=[wide] * 3 + [narrow, acc, acc],
        out_shape=[sh] * 3 + [jax.ShapeDtypeStruct((T, BLK), f32)] + [jax.ShapeDtypeStruct((8, BLK), f32)] * 2,
        compiler_params=_cparams(dimension_semantics=("arbitrary",)),
    )(cq, ck, cv, proj, alog, dtb, dq, dk, dv, dgates)


def gdn_conv_bwd(dcq, dck, dcv, proj, conv_w):
    T = dcq.shape[0]
    tr = BLK
    NT = T // tr

    def body(dq_r, dk_r, dv_r, nq, nk, nv, xq, xk, xv, hq, hk, hv, w_ref, dxq_o, dxk_o, dxv_o, dw_o, bufd, bufx):
        i = pl.program_id(0)

        @pl.when(i == 0)
        def _():
            dw_o[...] = jnp.zeros_like(dw_o)
        parts = ((dq_r, nq, xq, hq, dxq_o), (dk_r, nk, xk, hk, dxk_o), (dv_r, nv, xv, hv, dxv_o))
        for p, (dc_r, n_r, x_r, h_r, dx_o) in enumerate(parts):
            dc = dc_r[...]
            bufd[0:tr, :] = dc
            bufd[tr:, :] = jnp.where(i < NT - 1, n_r[...], 0.0)
            bufx[0:HALO, :] = jnp.where(i > 0, h_r[...], 0.0)
            bufx[HALO:, :] = x_r[...]
            dx = jnp.zeros((tr, W_AB), f32)
            rows = []
            for k in range(GDN_K):
                dx = dx + w_ref[k:k + 1, W_AB * p:W_AB * (p + 1)] * bufd[pl.ds(GDN_K - 1 - k, tr), :]
                rows.append(jnp.sum(dc * bufx[pl.ds(HALO - GDN_K + 1 + k, tr), :], axis=0, keepdims=True))
            dx_o[...] = dx
            dw_o[:, W_AB * p:W_AB * (p + 1)] += jnp.concatenate(rows + [jnp.zeros((8 - GDN_K, W_AB), f32)], axis=0)

    wide = pl.BlockSpec((tr, W_AB), lambda i: (i, 0))
    r = tr // HALO
    nxt = pl.BlockSpec((HALO, W_AB), lambda i: (jnp.minimum((i + 1) * r, NT * r - 1), 0))
    tiles, halos = _conv_specs(tr, (4, 5, 6), NT)
    sh = jax.ShapeDtypeStruct((T, W_AB), f32)
    return pl.pallas_call(
        body, name="gdn_conv_bwd", grid=(NT,),
        in_specs=[wide] * 3 + [nxt] * 3 + tiles + halos + [pl.BlockSpec((GDN_K, 3 * W_AB), lambda i: (0, 0))],
        out_specs=[wide] * 3 + [pl.BlockSpec((8, 3 * W_AB), lambda i: (0, 0))],
        out_shape=[sh] * 3 + [jax.ShapeDtypeStruct((8, 3 * W_AB), f32)],
        scratch_shapes=[pltpu.VMEM((tr + HALO, W_AB), f32), pltpu.VMEM((tr + HALO, W_AB), f32)],
        compiler_params=_cparams(dimension_semantics=("arbitrary",)),
    )(dcq, dck, dcv, dcq, dck, dcv, proj, proj, proj, proj, proj, proj, conv_w)


def _dot_hi_nt(a, b):
    return lax.dot_general(a, b, _NT, preferred_element_type=f32, precision=lax.Precision.HIGHEST)


def _dot_hi_tn(a, b):
    return lax.dot_general(a, b, _TN, preferred_element_type=f32, precision=lax.Precision.HIGHEST)


@jax.custom_vjp
def _unit_lower_inv(a):
    n = a.shape[0]
    eye = (lax.broadcasted_iota(jnp.int32, (n, n), 0) == lax.broadcasted_iota(jnp.int32, (n, n), 1)).astype(f32)
    b = -a
    x = eye + b
    p = b
    for _ in range(int(np.log2(n)) - 1):
        p = _dot_hi(p, p)
        x = x + _dot_hi(x, p)
    return x


def _unit_lower_inv_fwd(a):
    t = _unit_lower_inv(a)
    return t, t


def _unit_lower_inv_bwd(t, dt):
    return (-_dot_hi_nt(_dot_hi_tn(t, dt), t),)


_unit_lower_inv.defvjp(_unit_lower_inv_fwd, _unit_lower_inv_bwd)


def _gdn_chunk(qs, ks, vs, gates, zs, onorm, Ss):
    C = gates.shape[0]
    ri = lax.broadcasted_iota(jnp.int32, (C, C), 0)
    ci = lax.broadcasted_iota(jnp.int32, (C, C), 1)
    incl, strict = ri >= ci, ri > ci
    gcum = _dot_hi(incl.astype(f32), gates)
    gcum_t = gcum.T
    lane = lax.broadcasted_iota(jnp.int32, gates.shape, 1)
    sub = lax.broadcasted_iota(jnp.int32, gcum_t.shape, 0)
    last = lax.broadcasted_iota(jnp.int32, (C, 1), 0) == C - 1
    outs, nxt = [], []
    for h in range(N_HEAD_AB):
        bcol = jnp.sum(jnp.where(lane == h, gates, 0.0), axis=1, keepdims=True)
        gcol = jnp.sum(jnp.where(lane == N_HEAD_AB + h, gcum, 0.0), axis=1, keepdims=True)
        grow = jnp.sum(jnp.where(sub == N_HEAD_AB + h, gcum_t, 0.0), axis=0, keepdims=True)
        gl = jnp.sum(jnp.where(last, gcol, 0.0), axis=0, keepdims=True)
        decay = jnp.where(incl, jnp.exp(jnp.where(incl, gcol - grow, 0.0)), 0.0)
        q = qs[h] * DH_AB ** -0.5
        k, v, S = ks[h], vs[h], Ss[h]
        kb = k * bcol
        a = jnp.where(strict, _dot_nt(kb, k) * decay, 0.0)
        t = _unit_lower_inv(a)
        eg = jnp.exp(gcol)
        u = _dot(t, v * bcol)
        w = _dot(t, kb * eg)
        qk = jnp.where(incl, _dot_nt(q, k) * decay, 0.0)
        v_new = u - _dot(w, S)
        o = _dot(q * eg, S) + _dot(qk, v_new)
        nxt.append(S * jnp.exp(gl) + _dot_tn(k * jnp.exp(gl - gcol), v_new))
        outs.append(_rms(o, onorm) * _silu(zs[h]))
    return outs, nxt


def _heads(ref):
    return [ref[:, DH_AB * h:DH_AB * (h + 1)] for h in range(N_HEAD_AB)]


def gdn_chunk_fwd(q, k, v, gates, proj, onorm):
    T = q.shape[0]
    C = GDN_C
    NC = T // C

    def body(q_r, k_r, v_r, g_r, z_r, on_r, o_ref, sall_ref, S):
        @pl.when(pl.program_id(0) == 0)
        def _():
            S[...] = jnp.zeros_like(S)
        Ss = [S[h] for h in range(N_HEAD_AB)]
        for h in range(N_HEAD_AB):
            sall_ref[0, h] = Ss[h]
        outs, nxt = _gdn_chunk(_heads(q_r), _heads(k_r), _heads(v_r), g_r[...], _heads(z_r), on_r[...], Ss)
        for h in range(N_HEAD_AB):
            o_ref[:, DH_AB * h:DH_AB * (h + 1)] = outs[h].astype(o_ref.dtype)
            S[h] = nxt[h]

    wide = pl.BlockSpec((C, W_AB), lambda n: (n, 0))
    return pl.pallas_call(
        body, name="gdn_chunk_fwd", grid=(NC,),
        in_specs=[wide] * 3 + [pl.BlockSpec((C, BLK), lambda n: (n, 0)), pl.BlockSpec((C, W_AB), lambda n: (n, 7)),
                               pl.BlockSpec((1, DH_AB), lambda n: (0, 0))],
        out_specs=[wide, pl.BlockSpec((1, N_HEAD_AB, DH_AB, DH_AB), lambda n: (n, 0, 0, 0))],
        out_shape=[jax.ShapeDtypeStruct((T, W_AB), bf16), jax.ShapeDtypeStruct((NC, N_HEAD_AB, DH_AB, DH_AB), f32)],
        scratch_shapes=[pltpu.VMEM((N_HEAD_AB, DH_AB, DH_AB), f32)],
        compiler_params=_cparams(dimension_semantics=("arbitrary",)),
    )(q, k, v, gates, proj, onorm)


def gdn_chunk_bwd(q, k, v, gates, proj, onorm, sall, dmix):
    T = q.shape[0]
    C = GDN_C
    NC = T // C
    L = NC - 1

    def body(q_r, k_r, v_r, g_r, z_r, on_r, sall_r, do_r, dq_o, dk_o, dv_o, dz_o, dg_o, don_o, dS):
        @pl.when(pl.program_id(0) == 0)
        def _():
            dS[...] = jnp.zeros_like(dS)
        Ss = [sall_r[0, h] for h in range(N_HEAD_AB)]
        _, vjp = jax.vjp(_gdn_chunk, _heads(q_r), _heads(k_r), _heads(v_r), g_r[...], _heads(z_r), on_r[...], Ss)
        dqs, dks, dvs, dg, dzs, don, dSs = vjp((_heads(do_r), [dS[h] for h in range(N_HEAD_AB)]))
        for h in range(N_HEAD_AB):
            sl = slice(DH_AB * h, DH_AB * (h + 1))
            dq_o[:, sl] = dqs[h]
            dk_o[:, sl] = dks[h]
            dv_o[:, sl] = dvs[h]
            dz_o[:, sl] = dzs[h]
            dS[h] = dSs[h]
        dg_o[...] = dg
        _acc8(don_o, don, pl.program_id(0) == 0)

    wide = pl.BlockSpec((C, W_AB), lambda n: (L - n, 0))
    sh = jax.ShapeDtypeStruct((T, W_AB), f32)
    return pl.pallas_call(
        body, name="gdn_chunk_bwd", grid=(NC,),
        in_specs=[wide] * 3 + [pl.BlockSpec((C, BLK), lambda n: (L - n, 0)), pl.BlockSpec((C, W_AB), lambda n: (L - n, 7)),
                               pl.BlockSpec((1, DH_AB), lambda n: (0, 0)),
                               pl.BlockSpec((1, N_HEAD_AB, DH_AB, DH_AB), lambda n: (L - n, 0, 0, 0)),
                               pl.BlockSpec((C, W_AB), lambda n: (L - n, 1))],
        out_specs=[wide] * 4 + [pl.BlockSpec((C, BLK), lambda n: (L - n, 0)), pl.BlockSpec((8, DH_AB), lambda n: (0, 0))],
        out_shape=[sh] * 4 + [jax.ShapeDtypeStruct((T, BLK), f32), jax.ShapeDtypeStruct((8, DH_AB), f32)],
        scratch_shapes=[pltpu.VMEM((N_HEAD_AB, DH_AB, DH_AB), f32)],
        compiler_params=_cparams(dimension_semantics=("arbitrary",)),
    )(q, k, v, gates, proj, onorm, sall, dmix)


DH_CD = 64
SWA_G = 4
SWA_KV = 2
W_CD = 512


def _swa_block(q_ref, km, kp, kc, vm, vp, vc, sinks, g, n):
    scale = DH_CD ** -0.5
    ks = slice(DH_CD * g, DH_CD * (g + 1))
    Q = jnp.concatenate([q_ref[:, DH_CD * (SWA_G * g + j):DH_CD * (SWA_G * g + j + 1)] for j in range(SWA_G)], axis=0) * scale
    K3 = jnp.concatenate([km[:, ks], kp[:, ks], kc[:, ks]], axis=0)
    V3 = jnp.concatenate([vm[:, ks], vp[:, ks], vc[:, ks]], axis=0)
    s = _dot_nt(Q, K3)
    shp = s.shape
    row = lax.broadcasted_iota(jnp.int32, shp, 0)
    col = lax.broadcasted_iota(jnp.int32, shp, 1)
    i, part, j = row % BLK, col // BLK, col % BLK
    meta = (part == 0) & (j >= PAD) & ((j <= i) | (n > 0))
    prev = (part == 1) & (j > i) & (n >= 2)
    cur = (part == 2) & (j <= i) & (n >= 1)
    valid = meta | prev | cur
    grp = lax.broadcasted_iota(jnp.int32, (shp[0], 1), 0) // BLK
    sink = jnp.zeros((shp[0], 1), f32)
    for jj in range(SWA_G):
        sink = jnp.where(grp == jj, sinks[SWA_G * g + jj], sink)
    m = jnp.maximum(jnp.max(jnp.where(valid, s, NEG), axis=1, keepdims=True), sink)
    p = jnp.where(valid, jnp.exp(jnp.where(valid, s - m, 0.0)), 0.0)
    es = jnp.exp(sink - m)
    denom = jnp.sum(p, axis=1, keepdims=True) + es
    return Q, K3, V3, p / denom, es / denom, grp


def _swa_in_specs(rev=None):
    row = (lambda n: n) if rev is None else (lambda n: rev - n)
    kcol, vcol = 512 // BLK, 640 // BLK
    specs = [pl.BlockSpec((BLK, W_CD), lambda n: (row(n), 0))]
    for col in (kcol, vcol):
        specs += [pl.BlockSpec((BLK, BLK), functools.partial(lambda n, c: (0, c), c=col)),
                  pl.BlockSpec((BLK, BLK), functools.partial(lambda n, c: (jnp.maximum(row(n) - 1, 0), c), c=col)),
                  pl.BlockSpec((BLK, BLK), functools.partial(lambda n, c: (row(n), c), c=col))]
    return specs + [pl.BlockSpec(memory_space=pltpu.SMEM)]


def swa_fwd(proj, sinks):
    T = proj.shape[0]
    NB = T // BLK

    def body(q_ref, km, kp, kc, vm, vp, vc, sinks_ref, o_ref):
        n = pl.program_id(0)
        for g in range(SWA_KV):
            Q, K3, V3, pn, ps, grp = _swa_block(q_ref, km, kp, kc, vm, vp, vc, sinks_ref, g, n)
            o = _dot(pn, V3)
            for j in range(SWA_G):
                hd = SWA_G * g + j
                o_ref[:, DH_CD * hd:DH_CD * (hd + 1)] = o[BLK * j:BLK * (j + 1), :]

    return pl.pallas_call(
        body, name="swa_fwd", grid=(NB,), in_specs=_swa_in_specs(),
        out_specs=pl.BlockSpec((BLK, W_CD), lambda n: (n, 0)), out_shape=jax.ShapeDtypeStruct((T, W_CD), f32),
        compiler_params=_cparams(dimension_semantics=("arbitrary",)),
    )(proj, proj, proj, proj, proj, proj, proj, sinks)


def swa_bwd(proj, sinks, dmix):
    T = proj.shape[0]
    NB = T // BLK
    KV = 2 * SWA_KV * DH_CD

    def body(q_ref, km, kp, kc, vm, vp, vc, sinks_ref, do_ref, dq_ref, cur_ref, prev_ref, meta_ref, ds_ref):
        n = pl.program_id(0)

        @pl.when(n == 0)
        def _():
            meta_ref[...] = jnp.zeros_like(meta_ref)
            ds_ref[...] = jnp.zeros_like(ds_ref)
        rows = []
        for g in range(SWA_KV):
            Q, K3, V3, pn, ps, grp = _swa_block(q_ref, km, kp, kc, vm, vp, vc, sinks_ref, g, n)
            dO = jnp.concatenate([do_ref[:, DH_CD * (SWA_G * g + j):DH_CD * (SWA_G * g + j + 1)] for j in range(SWA_G)], axis=0)
            dP = _dot_nt(dO, V3)
            delta = jnp.sum(pn * dP, axis=1, keepdims=True)
            dS = pn * (dP - delta)
            dQ = _dot(dS, K3) * DH_CD ** -0.5
            dK3 = _dot_tn(dS, Q)
            dV3 = _dot_tn(pn, dO)
            dsk = -ps * delta
            for j in range(SWA_G):
                hd = SWA_G * g + j
                dq_ref[:, DH_CD * hd:DH_CD * (hd + 1)] = dQ[BLK * j:BLK * (j + 1), :]
                rows.append(jnp.broadcast_to(jnp.sum(jnp.where(grp == j, dsk, 0.0), axis=0, keepdims=True), (1, BLK)))
            kcols = slice(DH_CD * g, DH_CD * (g + 1))
            vcols = slice(SWA_KV * DH_CD + DH_CD * g, SWA_KV * DH_CD + DH_CD * (g + 1))
            meta_ref[:, kcols] += dK3[0:BLK]
            meta_ref[:, vcols] += dV3[0:BLK]
            prev_ref[:, kcols] = dK3[BLK:2 * BLK]
            prev_ref[:, vcols] = dV3[BLK:2 * BLK]
            cur_ref[:, kcols] = dK3[2 * BLK:]
            cur_ref[:, vcols] = dV3[2 * BLK:]
        ds_ref[...] += jnp.concatenate(rows, axis=0)

    kv = pl.BlockSpec((BLK, KV), lambda n: (n, 0))
    return pl.pallas_call(
        body, name="swa_bwd", grid=(NB,),
        in_specs=_swa_in_specs() + [pl.BlockSpec((BLK, W_CD), lambda n: (n, 0))],
        out_specs=[pl.BlockSpec((BLK, W_CD), lambda n: (n, 0)), kv, kv, pl.BlockSpec((BLK, KV), lambda n: (0, 0)),
                   pl.BlockSpec((8, BLK), lambda n: (0, 0))],
        out_shape=[jax.ShapeDtypeStruct((T, W_CD), f32), jax.ShapeDtypeStruct((T, KV), f32), jax.ShapeDtypeStruct((T, KV), f32),
                   jax.ShapeDtypeStruct((BLK, KV), f32), jax.ShapeDtypeStruct((8, BLK), f32)],
        compiler_params=_cparams(dimension_semantics=("arbitrary",)),
    )(proj, proj, proj, proj, proj, proj, proj, sinks, dmix)


SB_PAIR = 2


def _sb_consts():
    r = lax.broadcasted_iota(jnp.int32, (BLK, BLK), 0)
    c = lax.broadcasted_iota(jnp.int32, (BLK, BLK), 1)
    return r, c, (r > c).astype(bf16), (r >= c).astype(bf16)


def _sb_block(q, kb, n, m, r, c):
    z = _dot_nt(q, kb)
    valid = ((m * BLK + c) < (n * BLK + r)) & ((m * BLK + c) >= PAD)
    sp = _softplus(z)
    return z, valid, jnp.where(valid, -sp, 0.0), sp


def _sb_specs(T):
    qcol, kcol, vcol = 768 // BLK, 1280 // BLK, 1792 // BLK
    return [pl.BlockSpec((BLK, BLK), lambda hp, n: (n, qcol + hp)),
            pl.BlockSpec((T, BLK), lambda hp, n: (0, kcol + hp)),
            pl.BlockSpec((T, BLK), lambda hp, n: (0, vcol + hp))]


def sb_fwd(proj):
    T = proj.shape[0]
    NB = T // BLK

    def body(q_ref, k_ref, v_ref, o_ref):
        n = pl.program_id(1)
        r, c, m_gt, _ = _sb_consts()
        for hh in range(SB_PAIR):
            cols = slice(DH_CD * hh, DH_CD * (hh + 1))
            q = q_ref[:, cols] * DH_CD ** -0.5

            def cond(carry):
                m, run, _ = carry
                return jnp.logical_and(m >= 0, jnp.max(run) > SB_EXIT)

            def step(carry):
                m, run, acc = carry
                off = pl.multiple_of(m * BLK, BLK)
                kb = k_ref[pl.ds(off, BLK), cols]
                vb = v_ref[pl.ds(off, BLK), cols]
                z, valid, l, sp = _sb_block(q, kb, n, m, r, c)
                e = (z - sp) + _dot2(l, m_gt) + run
                a = jnp.where(valid, jnp.exp(jnp.where(valid, e, 0.0)), 0.0)
                return m - 1, run + jnp.sum(l, axis=1, keepdims=True), acc + _dot(a, vb)

            _, _, acc = lax.while_loop(cond, step, (n, jnp.zeros((BLK, 1), f32), jnp.zeros((BLK, DH_CD), f32)))
            o_ref[:, cols] = acc

    return pl.pallas_call(
        body, name="sb_fwd", grid=(W_CD // BLK, NB), in_specs=_sb_specs(T),
        out_specs=pl.BlockSpec((BLK, BLK), lambda hp, n: (n, hp)), out_shape=jax.ShapeDtypeStruct((T, W_CD), f32),
        compiler_params=_cparams(dimension_semantics=("arbitrary", "arbitrary")),
    )(proj, proj, proj)


def sb_bwd(proj, o, dmix):
    T = proj.shape[0]
    NB = T // BLK

    def body(q_ref, k_ref, v_ref, o_ref, do_ref, dq_ref, dk_ref, dv_ref):
        n = pl.program_id(1)

        @pl.when(n == 0)
        def _():
            dk_ref[...] = jnp.zeros_like(dk_ref)
            dv_ref[...] = jnp.zeros_like(dv_ref)
        r, c, m_gt, m_ge = _sb_consts()
        for hh in range(SB_PAIR):
            cols = slice(DH_CD * hh, DH_CD * (hh + 1))
            q = q_ref[:, cols] * DH_CD ** -0.5
            dO = do_ref[:, cols].astype(bf16)
            delta = jnp.sum(dO.astype(f32) * o_ref[:, cols], axis=1, keepdims=True)

            def cond(carry):
                m, run = carry[0], carry[1]
                return jnp.logical_and(m >= 0, jnp.max(run) > SB_EXIT)

            def step(carry):
                m, run, run_e, dq = carry
                off = pl.multiple_of(m * BLK, BLK)
                kb = k_ref[pl.ds(off, BLK), cols]
                vb = v_ref[pl.ds(off, BLK), cols]
                z, valid, l, sp = _sb_block(q, kb, n, m, r, c)
                e = (z - sp) + _dot2(l, m_gt) + run
                a = jnp.where(valid, jnp.exp(jnp.where(valid, e, 0.0)), 0.0).astype(bf16)
                E = a.astype(f32) * _dot_nt(dO, vb)
                F = delta - run_e - _dot2(E, m_ge)
                sig = jnp.exp(z - sp)
                dz = jnp.where(valid, E * (1.0 - sig) - F * sig, 0.0)
                dk_ref[pl.ds(off, BLK), cols] += _dot_tn(dz, q)
                dv_ref[pl.ds(off, BLK), cols] += _dot_tn(a, dO)
                return (m - 1, run + jnp.sum(l, axis=1, keepdims=True), run_e + jnp.sum(E, axis=1, keepdims=True),
                        dq + _dot(dz, kb))

            zero = jnp.zeros((BLK, 1), f32)
            res = lax.while_loop(cond, step, (n, zero, zero, jnp.zeros((BLK, DH_CD), f32)))
            dq_ref[:, cols] = res[3] * DH_CD ** -0.5

    blk = pl.BlockSpec((BLK, BLK), lambda hp, n: (n, hp))
    full = pl.BlockSpec((T, BLK), lambda hp, n: (0, hp))
    sh = jax.ShapeDtypeStruct((T, W_CD), f32)
    return pl.pallas_call(
        body, name="sb_bwd", grid=(W_CD // BLK, NB),
        in_specs=_sb_specs(T) + [blk, pl.BlockSpec((BLK, BLK), lambda hp, n: (n, W_CD // BLK + hp))],
        out_specs=[blk, full, full], out_shape=[sh] * 3,
        compiler_params=_cparams(dimension_semantics=("arbitrary", "arbitrary")),
    )(proj, proj, proj, o, dmix)


def cd_assemble(dcq, cur, prev, meta, dsq, dsk, dsv):
    T = dcq.shape[0]
    NB = T // BLK
    KV = cur.shape[1]

    def body(dcq_r, cur_r, nxt_r, meta_r, dsq_r, dsk_r, dsv_r, o_ref):
        n = pl.program_id(0)
        kv = cur_r[...] + jnp.where(n < NB - 1, nxt_r[...], 0.0) + jnp.where(n == 0, meta_r[...], 0.0)
        o_ref[:, 0:W_CD] = dcq_r[...].astype(o_ref.dtype)
        o_ref[:, W_CD:W_CD + KV] = kv.astype(o_ref.dtype)
        for j, ref in enumerate((dsq_r, dsk_r, dsv_r)):
            o_ref[:, W_CD + KV + W_CD * j:W_CD + KV + W_CD * (j + 1)] = ref[...].astype(o_ref.dtype)

    wide = pl.BlockSpec((BLK, W_CD), lambda n: (n, 0))
    return pl.pallas_call(
        body, name="cd_assemble", grid=(NB,),
        in_specs=[wide, pl.BlockSpec((BLK, KV), lambda n: (n, 0)), pl.BlockSpec((BLK, KV), lambda n: (jnp.minimum(n + 1, NB - 1), 0)),
                  pl.BlockSpec((BLK, KV), lambda n: (0, 0)), wide, wide, wide],
        out_specs=pl.BlockSpec((BLK, CD_IN), lambda n: (n, 0)), out_shape=jax.ShapeDtypeStruct((T, CD_IN), bf16),
        compiler_params=_cparams(dimension_semantics=("arbitrary",)),
    )(dcq, cur, prev, meta, dsq, dsk, dsv)


def loss_grad(h, target):
    T, Dm = h.shape
    NB = T // BLK

    def body(h_ref, t_ref, dh_ref, l_ref):
        n = pl.program_id(0)

        @pl.when(n == 0)
        def _():
            dh_ref[...] = jnp.zeros_like(dh_ref)
            l_ref[...] = jnp.zeros_like(l_ref)

        @pl.when(n > 0)
        def _():
            err = h_ref[...] - t_ref[...]
            dh_ref[...] = err * (1.0 / Dm)
            part = 0.5 * jnp.sum(jnp.mean(err * err, axis=-1, keepdims=True), axis=0, keepdims=True)
            l_ref[...] += jnp.broadcast_to(part, l_ref.shape)

    row = pl.BlockSpec((BLK, Dm), lambda n: (n, 0))
    return pl.pallas_call(
        body, name="loss_grad", grid=(NB,),
        in_specs=[row, pl.BlockSpec((BLK, Dm), lambda n: (jnp.maximum(n - 1, 0), 0))],
        out_specs=[row, pl.BlockSpec((8, BLK), lambda n: (0, 0))],
        out_shape=[jax.ShapeDtypeStruct((T, Dm), f32), jax.ShapeDtypeStruct((8, BLK), f32)],
        compiler_params=_cparams(dimension_semantics=("arbitrary",)),
    )(h, target)


SUM_ROWS = 256
_MESH = pl.DeviceIdType.MESH
_ANY = pl.BlockSpec(memory_space=pl.ANY)


def _place():
    return lax.axis_index("x"), lax.axis_index("y"), lax.axis_index("c")


def _other_chips(x, y):
    return [(1 - x, y, 2 * (1 - x) + y), (x, 1 - y, 2 * x + 1 - y), (1 - x, 1 - y, 2 * (1 - x) + 1 - y)]


def gather_weights(wbuf, sbuf):
    R = wbuf.shape[0]
    half = R // 2

    def body(w_ref, s_ref, out_ref, outs_ref, send_sems, recv_sems, local_sems):
        x, y, c = _place()
        p = 2 * x + y
        chips = _other_chips(x, y)
        sibling = (x, y, 1 - c)

        def rows(chip, hf):
            return out_ref.at[chip, pl.ds(hf * half, half), :]

        own_w = pltpu.make_async_copy(w_ref, out_ref.at[p], local_sems.at[0])
        own_s = pltpu.make_async_copy(s_ref, outs_ref.at[p], local_sems.at[1])
        own_w.start()
        own_s.start()
        sends = []
        for j, (qx, qy, q) in enumerate(chips):
            sends.append(pltpu.make_async_remote_copy(
                src_ref=w_ref.at[pl.ds(c * half, half), :], dst_ref=rows(p, c), send_sem=send_sems.at[j], recv_sem=recv_sems.at[j],
                device_id=(qx, qy, c), device_id_type=_MESH))
            sends.append(pltpu.make_async_remote_copy(
                src_ref=s_ref, dst_ref=outs_ref.at[p], send_sem=send_sems.at[3 + j], recv_sem=recv_sems.at[3 + j],
                device_id=(qx, qy, c), device_id_type=_MESH))
        for cp in sends:
            cp.start()
        passed = []
        for j, (qx, qy, q) in enumerate(chips):
            pltpu.make_async_remote_copy(src_ref=rows(q, c), dst_ref=rows(q, c), send_sem=send_sems.at[j], recv_sem=recv_sems.at[j],
                                         device_id=(qx, qy, c), device_id_type=_MESH).wait_recv()
            fwd = pltpu.make_async_remote_copy(src_ref=rows(q, c), dst_ref=rows(q, c), send_sem=send_sems.at[6 + j],
                                               recv_sem=recv_sems.at[6 + j], device_id=sibling, device_id_type=_MESH)
            fwd.start()
            passed.append(fwd)
        for j, (qx, qy, q) in enumerate(chips):
            pltpu.make_async_remote_copy(src_ref=s_ref, dst_ref=outs_ref.at[q], send_sem=send_sems.at[3 + j], recv_sem=recv_sems.at[3 + j],
                                         device_id=(qx, qy, c), device_id_type=_MESH).wait_recv()
            pltpu.make_async_remote_copy(src_ref=rows(q, 1 - c), dst_ref=rows(q, 1 - c), send_sem=send_sems.at[6 + j],
                                         recv_sem=recv_sems.at[6 + j], device_id=sibling, device_id_type=_MESH).wait_recv()
        for cp in sends + passed:
            cp.wait_send()
        own_w.wait()
        own_s.wait()

    return pl.pallas_call(
        body, name="gather_weights", in_specs=[_ANY, _ANY], out_specs=[_ANY, _ANY],
        out_shape=[jax.ShapeDtypeStruct((4,) + wbuf.shape, wbuf.dtype), jax.ShapeDtypeStruct((4,) + sbuf.shape, sbuf.dtype)],
        scratch_shapes=[pltpu.SemaphoreType.DMA((9,)), pltpu.SemaphoreType.DMA((9,)), pltpu.SemaphoreType.DMA((2,))],
    )(wbuf, sbuf)


def pair_exchange(g):
    S, _, H, Cw = g.shape

    def body(g_ref, out_ref, send_sem, recv_sem):
        x, y, c = _place()
        cp = pltpu.make_async_remote_copy(src_ref=g_ref.at[:, 1 - c], dst_ref=out_ref, send_sem=send_sem, recv_sem=recv_sem,
                                          device_id=(x, y, 1 - c), device_id_type=_MESH)
        cp.start()
        cp.wait()

    return pl.pallas_call(
        body, name="pair_exchange", in_specs=[_ANY], out_specs=_ANY, out_shape=jax.ShapeDtypeStruct((S, H, Cw), g.dtype),
        scratch_shapes=[pltpu.SemaphoreType.DMA, pltpu.SemaphoreType.DMA],
    )(g)


def pair_sum(g, got, c):
    S, _, H, Cw = g.shape
    tb = 3 * SUM_ROWS if H % (3 * SUM_ROWS) == 0 else SUM_ROWS

    def body(c_ref, a_ref, b_ref, o_ref):
        o_ref[...] = (a_ref[...].astype(f32) + b_ref[...].astype(f32)).astype(o_ref.dtype)

    spec = pl.BlockSpec((None, tb, Cw), lambda s, i, c_ref: (s, i, 0))
    return pl.pallas_call(
        body, name="pair_sum",
        grid_spec=pltpu.PrefetchScalarGridSpec(
            num_scalar_prefetch=1, grid=(S, H // tb),
            in_specs=[pl.BlockSpec((None, None, tb, Cw), lambda s, i, c_ref: (s, c_ref[0], i, 0)), spec], out_specs=spec),
        out_shape=jax.ShapeDtypeStruct((S, H, Cw), g.dtype),
        compiler_params=_cparams(dimension_semantics=("arbitrary", "arbitrary")),
    )(c, g, got)


def chip_exchange(hsum):
    S, H, Cw = hsum.shape

    def body(h_ref, out_ref, send_sems, recv_sems, local_sem):
        x, y, c = _place()
        p = 2 * x + y
        own = pltpu.make_async_copy(h_ref.at[p], out_ref.at[p], local_sem)
        own.start()
        sends = []
        for j, (qx, qy, q) in enumerate(_other_chips(x, y)):
            cp = pltpu.make_async_remote_copy(src_ref=h_ref.at[q], dst_ref=out_ref.at[p], send_sem=send_sems.at[j],
                                              recv_sem=recv_sems.at[j], device_id=(qx, qy, c), device_id_type=_MESH)
            cp.start()
            sends.append(cp)
        for j, (qx, qy, q) in enumerate(_other_chips(x, y)):
            pltpu.make_async_remote_copy(src_ref=h_ref.at[q], dst_ref=out_ref.at[q], send_sem=send_sems.at[j],
                                         recv_sem=recv_sems.at[j], device_id=(qx, qy, c), device_id_type=_MESH).wait_recv()
        for cp in sends:
            cp.wait_send()
        own.wait()

    return pl.pallas_call(
        body, name="chip_exchange", in_specs=[_ANY], out_specs=_ANY, out_shape=jax.ShapeDtypeStruct(hsum.shape, hsum.dtype),
        scratch_shapes=[pltpu.SemaphoreType.DMA((3,)), pltpu.SemaphoreType.DMA((3,)), pltpu.SemaphoreType.DMA],
    )(hsum)


def chip_sum(parts):
    S, H, Cw = parts.shape
    tb = 3 * SUM_ROWS if H % (3 * SUM_ROWS) == 0 else SUM_ROWS

    def body(p_ref, o_ref):
        acc = p_ref[0].astype(f32)
        for s in range(1, S):
            acc = acc + p_ref[s].astype(f32)
        o_ref[...] = acc

    return pl.pallas_call(
        body, name="chip_sum", grid=(H // tb,), in_specs=[pl.BlockSpec((S, tb, Cw), lambda i: (0, i, 0))],
        out_specs=pl.BlockSpec((tb, Cw), lambda i: (i, 0)), out_shape=jax.ShapeDtypeStruct((H, Cw), f32),
        compiler_params=_cparams(dimension_semantics=("arbitrary",)),
    )(parts)


def pair_gather(rsum):
    H, Cw = rsum.shape

    def body(r_ref, out_ref, send_sem, recv_sem, local_sem):
        x, y, c = _place()
        own = pltpu.make_async_copy(r_ref, out_ref.at[c], local_sem)
        own.start()
        cp = pltpu.make_async_remote_copy(src_ref=r_ref, dst_ref=out_ref.at[c], send_sem=send_sem, recv_sem=recv_sem,
                                          device_id=(x, y, 1 - c), device_id_type=_MESH)
        cp.start()
        pltpu.make_async_remote_copy(src_ref=r_ref, dst_ref=out_ref.at[1 - c], send_sem=send_sem, recv_sem=recv_sem,
                                     device_id=(x, y, 1 - c), device_id_type=_MESH).wait_recv()
        cp.wait_send()
        own.wait()

    return pl.pallas_call(
        body, name="pair_gather", in_specs=[_ANY], out_specs=_ANY, out_shape=jax.ShapeDtypeStruct((2, H, Cw), f32),
        scratch_shapes=[pltpu.SemaphoreType.DMA, pltpu.SemaphoreType.DMA, pltpu.SemaphoreType.DMA],
    )(rsum)


def small_reduce(src):
    S, RS, Cw = src.shape

    def body(src_ref, out_ref, recv, send_sems, recv_sems):
        x, y, c = _place()
        me = 4 * x + 2 * y + c
        p = 2 * x + y
        recv[me] = src_ref[p]
        flips = [(fx, fy, fc) for fx in (0, 1) for fy in (0, 1) for fc in (0, 1)][1:]
        sends = []
        for k, (fx, fy, fc) in enumerate(flips):
            tx, ty, tc = (1 - x if fx else x), (1 - y if fy else y), (1 - c if fc else c)
            cp = pltpu.make_async_remote_copy(src_ref=src_ref.at[2 * tx + ty], dst_ref=recv.at[me], send_sem=send_sems.at[k],
                                              recv_sem=recv_sems.at[me], device_id=(tx, ty, tc), device_id_type=_MESH)
            cp.start()
            sends.append(cp)
        for k, (fx, fy, fc) in enumerate(flips):
            tx, ty, tc = (1 - x if fx else x), (1 - y if fy else y), (1 - c if fc else c)
            frm = 4 * tx + 2 * ty + tc
            pltpu.make_async_remote_copy(src_ref=src_ref.at[p], dst_ref=recv.at[frm], send_sem=send_sems.at[k],
                                         recv_sem=recv_sems.at[frm], device_id=(tx, ty, tc), device_id_type=_MESH).wait_recv()
        for cp in sends:
            cp.wait_send()
        acc = recv[0]
        for d in range(1, 8):
            acc = acc + recv[d]
        out_ref[...] = acc

    vm = pl.BlockSpec(memory_space=pltpu.VMEM)
    return pl.pallas_call(
        body, name="small_reduce", in_specs=[vm], out_specs=vm, out_shape=jax.ShapeDtypeStruct((RS, Cw), f32),
        scratch_shapes=[pltpu.VMEM((8, RS, Cw), f32), pltpu.SemaphoreType.DMA((7,)), pltpu.SemaphoreType.DMA((8,))],
    )(src)


def _row(v):
    return v.reshape(1, -1)


def _ffn_fwd(h, g_pre, g_post, wg, wu, wd):
    u, G, U, a = ffn_up(h, _row(g_pre), wg, wu)
    y, h_new = proj_norm_res(a, wd, h, _row(g_post), 0.5)
    return h_new, (h, u, G, U, a, y)


def _ffn_bwd(dh, saved, g_pre, g_post, wg, wu, wd):
    h, u, G, U, a, y = saved
    F = wg.shape[2]
    dy, dg_post = post_norm_bwd(y, _row(g_post), dh, 0.5)
    dG, dU = ffn_bwd_act(dy, wd, G, U)
    dwd = mm_tn(a, dy[None], D)
    dwg = mm_tn(u[None], dG, F)
    dwu = mm_tn(u[None], dU, F)
    dh_new, dg_pre = mm_nt_norm_bwd([(dG, wg), (dU, wu)], h, _row(g_pre), dh)
    return dh_new, dwg, dwu, dwd, dg_pre[0], dg_post[0]


def _lane_vec(v, at):
    return jnp.pad(v, (at, BLK - at - v.shape[0])).reshape(1, BLK)


def _ab_fwd(h, g_pre, g_post, w, tabs):
    u, proj = norm_proj(h, _row(g_pre), w["ab_in"], AB_IN_P // 3)
    ret, sall_r = ret_fwd(proj, *tabs)
    alog, dtb = _lane_vec(w["a_log"], N_HEAD_AB), _lane_vec(w["dt_bias"], N_HEAD_AB)
    cq, ck, cv, q, k, v, gates = gdn_prep_fwd(proj, w["conv"], alog, dtb)
    gdn, sall_g = gdn_chunk_fwd(q, k, v, gates, proj, _row(w["out_norm"]))
    mixed = jnp.concatenate([ret, gdn], axis=1)
    y, h_new = proj_norm_res(mixed[None], w["ab_out"][None], h, _row(g_post), 1.0)
    return h_new, (h, u, proj, sall_r, (cq, ck, cv, q, k, v, gates), sall_g, mixed, y, alog, dtb)


def _ab_bwd(dh, saved, g_pre, g_post, w, tabs):
    h, u, proj, sall_r, (cq, ck, cv, q, k, v, gates), sall_g, mixed, y, alog, dtb = saved
    dy, dg_post = post_norm_bwd(y, _row(g_post), dh, 1.0)
    dmix = mm_nt(dy, w["ab_out"])
    dw_out = mm_tn(mixed[None], dy[None], D)[0]
    drq, drk, drv, drg = ret_bwd(proj, *tabs, sall_r, dmix)
    onorm = _row(w["out_norm"])
    dq, dk, dv, dz, dgates, don = gdn_chunk_bwd(q, k, v, gates, proj, onorm, sall_g, dmix)
    dcq, dck, dcv, dgb, dal, ddt = gdn_prep_bwd(cq, ck, cv, proj, alog, dtb, dq, dk, dv, dgates)
    dxq, dxk, dxv, dconv = gdn_conv_bwd(dcq, dck, dcv, proj, w["conv"])
    dproj = jnp.concatenate([t.astype(bf16) for t in (drq, drk, drv, drg, dxq, dxk, dxv, dz, dgb)], axis=1)
    dw_in = mm_tn(u[None], dproj[None], AB_IN_P // 3)[0]
    dh_new, dg_pre = mm_nt_norm_bwd([(dproj[None], w["ab_in"][None])], h, _row(g_pre), dh, ksplit=3)
    small = dict(a_log=dal[0, N_HEAD_AB:2 * N_HEAD_AB], dt_bias=ddt[0, N_HEAD_AB:2 * N_HEAD_AB], out_norm=don[0], conv=dconv[0:GDN_K])
    return dh_new, dw_in, dw_out, dg_pre[0], dg_post[0], small


def _cd_fwd(h, g_pre, g_post, w):
    u, proj = norm_proj(h, _row(g_pre), w["cd_in"], CD_IN // 3)
    swa = swa_fwd(proj, w["sinks"])
    sb = sb_fwd(proj)
    mixed = jnp.concatenate([swa.astype(bf16), sb.astype(bf16)], axis=1)
    y, h_new = proj_norm_res(mixed[None], w["cd_out"][None], h, _row(g_post), 1.0)
    return h_new, (h, u, proj, sb, mixed, y)


def _cd_bwd(dh, saved, g_pre, g_post, w):
    h, u, proj, sb, mixed, y = saved
    dy, dg_post = post_norm_bwd(y, _row(g_post), dh, 1.0)
    dmix = mm_nt(dy, w["cd_out"])
    dw_out = mm_tn(mixed[None], dy[None], D)[0]
    dcq, cur, prev, meta, dsinks = swa_bwd(proj, w["sinks"], dmix)
    dsq, dsk, dsv = sb_bwd(proj, sb, dmix)
    dproj = cd_assemble(dcq, cur, prev, meta, dsq, dsk, dsv)
    dw_in = mm_tn(u[None], dproj[None], CD_IN // 3)[0]
    dh_new, dg_pre = mm_nt_norm_bwd([(dproj[None], w["cd_in"][None])], h, _row(g_pre), dh, ksplit=3)
    return dh_new, dw_in, dw_out, dg_pre[0], dg_post[0], dsinks[:, 0]


def local_step(x, target, w):
    L = x.shape[0]
    T = PAD + N_META + L
    tabs = rot_tables(T)
    h = jnp.concatenate([jnp.zeros((PAD, D), f32), w["meta"], x], axis=0)
    ng = w["norm_gains"]
    saved = []
    for i in range(2):
        g = ng[i]
        h, s1 = _ffn_fwd(h, g[0], g[1], w["wg"][i, 0], w["wu"][i, 0], w["wd"][i, 0])
        if i == 0:
            h, sm = _ab_fwd(h, g[2], g[3], w, tabs)
        else:
            h, sm = _cd_fwd(h, g[2], g[3], w)
        h, s2 = _ffn_fwd(h, g[4], g[5], w["wg"][i, 1], w["wu"][i, 1], w["wd"][i, 1])
        saved.append((s1, sm, s2))
    dh, lpart = loss_grad(h, target)
    grads = {}
    dng = [[None] * 6 for _ in range(2)]
    dwg = [[None, None], [None, None]]
    dwu = [[None, None], [None, None]]
    dwd = [[None, None], [None, None]]
    for i in (1, 0):
        g = ng[i]
        s1, sm, s2 = saved[i]
        dh, dwg[i][1], dwu[i][1], dwd[i][1], dng[i][4], dng[i][5] = _ffn_bwd(dh, s2, g[4], g[5], w["wg"][i, 1], w["wu"][i, 1], w["wd"][i, 1])
        if i == 0:
            dh, grads["ab_in"], grads["ab_out"], dng[i][2], dng[i][3], small = _ab_bwd(dh, sm, g[2], g[3], w, tabs)
            grads.update(small)
        else:
            dh, grads["cd_in"], grads["cd_out"], dng[i][2], dng[i][3], grads["sinks"] = _cd_bwd(dh, sm, g[2], g[3], w)
        dh, dwg[i][0], dwu[i][0], dwd[i][0], dng[i][0], dng[i][1] = _ffn_bwd(dh, s1, g[0], g[1], w["wg"][i, 0], w["wu"][i, 0], w["wd"][i, 0])
    grads["wg"], grads["wu"], grads["wd"] = dwg, dwu, dwd
    grads["norm_gains"] = jnp.stack([jnp.stack(r) for r in dng])
    grads["meta"] = dh[PAD:PAD + N_META]
    return lpart[0, 0], dh[PAD + N_META:], grads


def _r16(n, mult=16):
    return -(-n // mult) * mult


def _big_layout(F):
    names = ["wg", "wu", "wd", "ab_in", "ab_out", "cd_in", "cd_out"]
    rows = [4 * F, 4 * F, 4 * F, AB_IN // 4, D // 4, CD_IN // 4, D // 4]
    offs, o = {}, 0
    for n, r in zip(names, rows):
        offs[n] = (o, r)
        o += _r16(r)
    return offs, 2 * _r16(-(-o // 2), SUM_ROWS)


def _cat_rows(parts, total, mult=16):
    out = []
    for p in parts:
        pad = _r16(p.shape[-2], mult) - p.shape[-2]
        out.append(jnp.pad(p, [(0, 0)] * (p.ndim - 2) + [(0, pad), (0, 0)]) if pad else p)
    used = sum(o.shape[-2] for o in out)
    if total > used:
        out.append(jnp.zeros(out[0].shape[:-2] + (total - used, out[0].shape[-1]), out[0].dtype))
    return jnp.concatenate(out, axis=-2)


SMALL_ROWS = 72
REPL_ROWS = 8


def _small_rows(meta, ng, conv):
    lead = meta.shape[:-2]
    return _cat_rows([meta.reshape(lead + (32, BLK)), ng.reshape(lead + (24, BLK)), conv.reshape(lead + (12, BLK))], SMALL_ROWS, 8)


def _small_unrows(buf):
    lead = buf.shape[:-2]
    return buf[..., 0:32, :].reshape(lead + (N_META, D // 4)), buf[..., 32:56, :].reshape(lead + (2, 6, D // 4)), \
        buf[..., 56:68, :].reshape(lead + (GDN_K, 3 * W_AB // 4))


def _shard_cols(a, axis):
    shp = a.shape
    a = a.reshape(shp[:axis] + (4, shp[axis] // 4) + shp[axis + 1:])
    return jnp.moveaxis(a, axis, 0)


def _unshard_cols(a, axis):
    a = jnp.moveaxis(a, 0, axis)
    shp = a.shape
    return a.reshape(shp[:axis] + (4 * shp[axis + 1],) + shp[axis + 2:])


def kernel(x, meta_tokens, norm_gains, ffn_w_gate, ffn_w_up, ffn_w_down, ab_w_in, ab_conv_w, ab_a_log, ab_dt_bias, ab_out_norm, ab_w_out, cd_w_in, cd_sinks, cd_w_out, loss_target, m_meta_tokens, m_norm_gains, m_ffn_w_gate, m_ffn_w_up, m_ffn_w_down, m_ab_w_in, m_ab_conv_w, m_ab_a_log, m_ab_dt_bias, m_ab_out_norm, m_ab_w_out, m_cd_w_in, m_cd_sinks, m_cd_w_out, v_meta_tokens, v_norm_gains, v_ffn_w_gate, v_ffn_w_up, v_ffn_w_down, v_ab_w_in, v_ab_conv_w, v_ab_a_log, v_ab_dt_bias, v_ab_out_norm, v_ab_w_out, v_cd_w_in, v_cd_sinks, v_cd_w_out):
    F = ffn_w_gate.shape[-1]
    offs, R = _big_layout(F)
    names = ["wg", "wu", "wd", "ab_in", "ab_out", "cd_in", "cd_out"]
    shard = dict(wg=ffn_w_gate, wu=ffn_w_up, wd=ffn_w_down, ab_in=ab_w_in, ab_out=ab_w_out, cd_in=cd_w_in, cd_out=cd_w_out)

    wbuf = _cat_rows([shard[n].reshape(-1, D).astype(bf16) for n in names], R)
    sbuf = _small_rows(meta_tokens, norm_gains, ab_conv_w[0])
    gw, gs = gather_weights(wbuf, sbuf)

    def part(n):
        o, r = offs[n]
        return gw[:, o:o + r]

    meta_s, ng_s, conv_s = _small_unrows(gs)
    w = dict(
        wg=jnp.transpose(part("wg").reshape(4, 2, 2, D, F), (1, 2, 0, 3, 4)),
        wu=jnp.transpose(part("wu").reshape(4, 2, 2, D, F), (1, 2, 0, 3, 4)),
        wd=jnp.transpose(part("wd").reshape(4, 2, 2, F, D), (1, 2, 0, 3, 4)),
        ab_in=jnp.pad(_unshard_cols(part("ab_in").reshape(4, D, AB_IN // 4), 1), ((0, 0), (0, AB_IN_P - AB_IN))),
        ab_out=part("ab_out").reshape(D, D),
        cd_in=_unshard_cols(part("cd_in").reshape(4, D, CD_IN // 4), 1),
        cd_out=part("cd_out").reshape(D, D),
        meta=_unshard_cols(meta_s, 1), norm_gains=_unshard_cols(ng_s, 2), conv=_unshard_cols(conv_s, 1),
        a_log=ab_a_log[0], dt_bias=ab_dt_bias[0], out_norm=ab_out_norm[0], sinks=cd_sinks[0],
    )

    loss_local, dx, g = local_step(x[0], loss_target[0], w)

    def stack22(t):
        return jnp.stack([jnp.stack(r) for r in t])

    gparts = dict(
        wg=jnp.transpose(stack22(g["wg"]), (2, 0, 1, 3, 4)).reshape(4, 4 * F, D),
        wu=jnp.transpose(stack22(g["wu"]), (2, 0, 1, 3, 4)).reshape(4, 4 * F, D),
        wd=jnp.transpose(stack22(g["wd"]), (2, 0, 1, 3, 4)).reshape(4, 4 * F, D),
        ab_in=_shard_cols(g["ab_in"][:, :AB_IN], 1).reshape(4, AB_IN // 4, D),
        ab_out=g["ab_out"].reshape(4, D // 4, D),
        cd_in=_shard_cols(g["cd_in"], 1).reshape(4, CD_IN // 4, D),
        cd_out=g["cd_out"].reshape(4, D // 4, D),
    )
    gbuf = _cat_rows([gparts[n].astype(bf16) for n in names], R).reshape(4, 2, R // 2, D)
    c = lax.axis_index("c").astype(jnp.int32).reshape(1)
    got = pair_exchange(gbuf)
    hsum = pair_sum(gbuf, got, c)
    parts = chip_exchange(hsum)
    rsum = chip_sum(parts)
    gfull = pair_gather(rsum).reshape(R, D)

    onehot = np.eye(REPL_ROWS, dtype=np.float32)
    repl = sum(onehot[k][:, None] * _lane_vec(g[n], 0) for k, n in enumerate(("a_log", "dt_bias", "out_norm", "sinks")))
    ssrc = jnp.concatenate([_small_rows(_shard_cols(g["meta"], 1), _shard_cols(g["norm_gains"], 2), _shard_cols(g["conv"], 1)),
                            jnp.broadcast_to(repl, (4, REPL_ROWS, BLK))], axis=1)
    sred = small_reduce(ssrc)
    g_meta, g_ng, g_conv = _small_unrows(sred[:SMALL_ROWS])

    def gpart(n, shape):
        o, r = offs[n]
        return gfull[o:o + r].reshape(shape)

    grad = dict(
        meta_tokens=g_meta, norm_gains=g_ng,
        ffn_w_gate=gpart("wg", ffn_w_gate.shape), ffn_w_up=gpart("wu", ffn_w_up.shape), ffn_w_down=gpart("wd", ffn_w_down.shape),
        ab_w_in=gpart("ab_in", ab_w_in.shape), ab_conv_w=g_conv[None],
        ab_a_log=sred[SMALL_ROWS:SMALL_ROWS + 1, 0:N_HEAD_AB], ab_dt_bias=sred[SMALL_ROWS + 1:SMALL_ROWS + 2, 0:N_HEAD_AB],
        ab_out_norm=sred[SMALL_ROWS + 2:SMALL_ROWS + 3, :], ab_w_out=gpart("ab_out", ab_w_out.shape),
        cd_w_in=gpart("cd_in", cd_w_in.shape), cd_sinks=sred[SMALL_ROWS + 3:SMALL_ROWS + 4, 0:2 * SWA_G], cd_w_out=gpart("cd_out", cd_w_out.shape),
    )

    weights = dict(meta_tokens=meta_tokens, norm_gains=norm_gains, ffn_w_gate=ffn_w_gate, ffn_w_up=ffn_w_up, ffn_w_down=ffn_w_down,
                   ab_w_in=ab_w_in, ab_conv_w=ab_conv_w, ab_a_log=ab_a_log, ab_dt_bias=ab_dt_bias, ab_out_norm=ab_out_norm,
                   ab_w_out=ab_w_out, cd_w_in=cd_w_in, cd_sinks=cd_sinks, cd_w_out=cd_w_out)
    ms = dict(meta_tokens=m_meta_tokens, norm_gains=m_norm_gains, ffn_w_gate=m_ffn_w_gate, ffn_w_up=m_ffn_w_up, ffn_w_down=m_ffn_w_down,
              ab_w_in=m_ab_w_in, ab_conv_w=m_ab_conv_w, ab_a_log=m_ab_a_log, ab_dt_bias=m_ab_dt_bias, ab_out_norm=m_ab_out_norm,
              ab_w_out=m_ab_w_out, cd_w_in=m_cd_w_in, cd_sinks=m_cd_sinks, cd_w_out=m_cd_w_out)
    vs = dict(meta_tokens=v_meta_tokens, norm_gains=v_norm_gains, ffn_w_gate=v_ffn_w_gate, ffn_w_up=v_ffn_w_up, ffn_w_down=v_ffn_w_down,
              ab_w_in=v_ab_w_in, ab_conv_w=v_ab_conv_w, ab_a_log=v_ab_a_log, ab_dt_bias=v_ab_dt_bias, ab_out_norm=v_ab_out_norm,
              ab_w_out=v_ab_w_out, cd_w_in=v_cd_w_in, cd_sinks=v_cd_sinks, cd_w_out=v_cd_w_out)
    order = list(weights)
    delta, new_m, new_v = {}, {}, {}
    for n in order:
        shp = weights[n].shape
        two = (-1, shp[-1])
        d, mn, vn = adamw(weights[n].reshape(two), grad[n].reshape(two), ms[n].reshape(two), vs[n].reshape(two))
        delta[n], new_m[n], new_v[n] = d.reshape(shp), mn.reshape(shp), vn.reshape(shp)

    loss = lax.psum(loss_local, ("x", "y", "c"))
    return (loss, dx[None], *[grad[n].reshape(weights[n].shape) for n in order], *[delta[n] for n in order],
            *[new_m[n] for n in order], *[new_v[n] for n in order])
```

```python
import functools

import numpy as np
import jax
import jax.numpy as jnp
from jax import lax
from jax.experimental import pallas as pl
from jax.experimental.pallas import tpu as pltpu

f32 = jnp.float32
bf16 = jnp.bfloat16

EPS = 1e-6
D = 1024
N_META = 16
PAD = 112
BLK = 128
GDN_C = 64
N_HEAD_AB = 4
DH_AB = 128
AB_IN = 4104
AB_IN_P = 4224
CD_IN = 2304
ADAM_LR, ADAM_B1, ADAM_B2, ADAM_EPS, ADAM_WD, ADAM_STEP = 0.001, 0.9, 0.999, 1e-08, 0.01, 10
VMEM_LIMIT = 56 * 1024 * 1024
NEG = -1e30
SB_EXIT = -104.0

_NT = (((1,), (1,)), ((), ()))
_TN = (((0,), (0,)), ((), ()))


def _cparams(**kw):
    return pltpu.CompilerParams(vmem_limit_bytes=VMEM_LIMIT, **kw)


def _dot(a, b):
    return jnp.dot(a.astype(bf16), b.astype(bf16), preferred_element_type=f32)


def _dot_nt(a, b):
    return lax.dot_general(a.astype(bf16), b.astype(bf16), _NT, preferred_element_type=f32)


def _dot_tn(a, b):
    return lax.dot_general(a.astype(bf16), b.astype(bf16), _TN, preferred_element_type=f32)


def _dot2(a, b01):
    hi = a.astype(bf16)
    lo = (a - hi.astype(f32)).astype(bf16)
    return jnp.dot(hi, b01, preferred_element_type=f32) + jnp.dot(lo, b01, preferred_element_type=f32)


def _row_tile(t):
    for c in (640, 512, 256, 128):
        if t % c == 0:
            return c
    raise ValueError(t)


def _sigmoid(x):
    return 1.0 / (1.0 + jnp.exp(-x))


def _silu(x):
    return x * _sigmoid(x)


def _softplus(x):
    return jnp.maximum(x, 0.0) + jnp.log(1.0 + jnp.exp(-jnp.abs(x)))


def _rms(x, g):
    r = lax.rsqrt(jnp.mean(x * x, axis=-1, keepdims=True) + EPS)
    return x * r * g


def _rms_bwd(x, g, dy):
    r = lax.rsqrt(jnp.mean(x * x, axis=-1, keepdims=True) + EPS)
    xh = x * r
    dg = jnp.sum(dy * xh, axis=0, keepdims=True)
    dxh = dy * g
    dx = r * (dxh - xh * jnp.mean(dxh * xh, axis=-1, keepdims=True))
    return dx, dg


def _zero_pad_rows(v, i, tr):
    rows = i * tr + lax.broadcasted_iota(jnp.int32, (tr, 1), 0)
    return jnp.where(rows >= PAD, v, 0.0)


def _acc8(ref, row, first):
    @pl.when(first)
    def _():
        ref[...] = jnp.zeros_like(ref)
    ref[...] += jnp.broadcast_to(row, ref.shape)


def ffn_up(h, g, wg, wu):
    T, Dm = h.shape
    S, _, F = wg.shape
    tm = _row_tile(T)

    def body(h_ref, g_ref, wg_ref, wu_ref, u_ref, G_ref, U_ref, a_ref):
        @pl.when(pl.program_id(1) == 0)
        def _():
            u_ref[...] = _rms(h_ref[...], g_ref[...]).astype(u_ref.dtype)
        u = u_ref[...]
        G = _dot(u, wg_ref[...])
        U = _dot(u, wu_ref[...])
        G_ref[...] = G.astype(G_ref.dtype)
        U_ref[...] = U.astype(U_ref.dtype)
        a_ref[...] = (_silu(G) * U).astype(a_ref.dtype)

    act = jax.ShapeDtypeStruct((S, T, F), bf16)
    wspec = pl.BlockSpec((None, Dm, F), lambda i, s: (s, 0, 0))
    aspec = pl.BlockSpec((None, tm, F), lambda i, s: (s, i, 0))
    return pl.pallas_call(
        body, name="ffn_up", grid=(T // tm, S),
        in_specs=[pl.BlockSpec((tm, Dm), lambda i, s: (i, 0)), pl.BlockSpec((1, Dm), lambda i, s: (0, 0)), wspec, wspec],
        out_specs=[pl.BlockSpec((tm, Dm), lambda i, s: (i, 0)), aspec, aspec, aspec],
        out_shape=[jax.ShapeDtypeStruct((T, Dm), bf16), act, act, act],
        compiler_params=_cparams(dimension_semantics=("arbitrary", "arbitrary")),
    )(h, g, wg, wu)


def norm_proj(h, g, w, tn):
    T, Dm = h.shape
    N = w.shape[1]
    tm = _row_tile(T)

    def body(h_ref, g_ref, w_ref, u_ref, p_ref):
        @pl.when(pl.program_id(1) == 0)
        def _():
            u_ref[...] = _rms(h_ref[...], g_ref[...]).astype(u_ref.dtype)
        p_ref[...] = _dot(u_ref[...], w_ref[...])

    return pl.pallas_call(
        body, name="norm_proj", grid=(T // tm, N // tn),
        in_specs=[pl.BlockSpec((tm, Dm), lambda i, j: (i, 0)), pl.BlockSpec((1, Dm), lambda i, j: (0, 0)),
                  pl.BlockSpec((Dm, tn), lambda i, j: (0, j))],
        out_specs=[pl.BlockSpec((tm, Dm), lambda i, j: (i, 0)), pl.BlockSpec((tm, tn), lambda i, j: (i, j))],
        out_shape=[jax.ShapeDtypeStruct((T, Dm), bf16), jax.ShapeDtypeStruct((T, N), f32)],
        compiler_params=_cparams(dimension_semantics=("arbitrary", "arbitrary")),
    )(h, g, w)


def proj_norm_res(a, w, h, g, coef):
    S, T, F = a.shape
    Dm = w.shape[2]
    tm = _row_tile(T)

    def body(a_ref, w_ref, h_ref, g_ref, y_ref, o_ref, acc):
        s = pl.program_id(1)

        @pl.when(s == 0)
        def _():
            acc[...] = jnp.zeros_like(acc)
        acc[...] += _dot(a_ref[...], w_ref[...])

        @pl.when(s == S - 1)
        def _():
            y = acc[...]
            y_ref[...] = y
            o_ref[...] = h_ref[...] + coef * _rms(y, g_ref[...])

    row = pl.BlockSpec((tm, Dm), lambda i, s: (i, 0))
    return pl.pallas_call(
        body, name="proj_norm_res", grid=(T // tm, S),
        in_specs=[pl.BlockSpec((None, tm, F), lambda i, s: (s, i, 0)), pl.BlockSpec((None, F, Dm), lambda i, s: (s, 0, 0)),
                  row, pl.BlockSpec((1, Dm), lambda i, s: (0, 0))],
        out_specs=[row, row],
        out_shape=[jax.ShapeDtypeStruct((T, Dm), f32), jax.ShapeDtypeStruct((T, Dm), f32)],
        scratch_shapes=[pltpu.VMEM((tm, Dm), f32)],
        compiler_params=_cparams(dimension_semantics=("arbitrary", "arbitrary")),
    )(a, w, h, g)


def post_norm_bwd(y, g, dz, coef):
    T, Dm = y.shape
    tm = _row_tile(T)

    def body(y_ref, g_ref, dz_ref, dy_ref, dg_ref):
        dy, dg = _rms_bwd(y_ref[...], g_ref[...], coef * dz_ref[...])
        dy_ref[...] = _zero_pad_rows(dy, pl.program_id(0), tm).astype(dy_ref.dtype)
        _acc8(dg_ref, dg, pl.program_id(0) == 0)

    row = pl.BlockSpec((tm, Dm), lambda i: (i, 0))
    return pl.pallas_call(
        body, name="post_norm_bwd", grid=(T // tm,),
        in_specs=[row, pl.BlockSpec((1, Dm), lambda i: (0, 0)), row],
        out_specs=[row, pl.BlockSpec((8, Dm), lambda i: (0, 0))],
        out_shape=[jax.ShapeDtypeStruct((T, Dm), bf16), jax.ShapeDtypeStruct((8, Dm), f32)],
        compiler_params=_cparams(dimension_semantics=("arbitrary",)),
    )(y, g, dz)


def ffn_bwd_act(dy, wd, G, U):
    T, Dm = dy.shape
    S, F, _ = wd.shape
    tm = _row_tile(T)

    def body(dy_ref, w_ref, G_ref, U_ref, dG_ref, dU_ref):
        da = _dot_nt(dy_ref[...], w_ref[...])
        Gv = G_ref[...].astype(f32)
        Uv = U_ref[...].astype(f32)
        sg = _sigmoid(Gv)
        dU_ref[...] = (da * Gv * sg).astype(dU_ref.dtype)
        dG_ref[...] = (da * Uv * sg * (1.0 + Gv * (1.0 - sg))).astype(dG_ref.dtype)

    aspec = pl.BlockSpec((None, tm, F), lambda i, s: (s, i, 0))
    act = jax.ShapeDtypeStruct((S, T, F), bf16)
    return pl.pallas_call(
        body, name="ffn_bwd_act", grid=(T // tm, S),
        in_specs=[pl.BlockSpec((tm, Dm), lambda i, s: (i, 0)), pl.BlockSpec((None, F, Dm), lambda i, s: (s, 0, 0)), aspec, aspec],
        out_specs=[aspec, aspec], out_shape=[act, act],
        compiler_params=_cparams(dimension_semantics=("arbitrary", "arbitrary")),
    )(dy, wd, G, U)


def mm_nt_norm_bwd(pairs, h, g, dres, ksplit=1):
    S, T, K = pairs[0][0].shape
    assert S == 1 or ksplit == 1
    steps = S * ksplit
    tk = K // ksplit
    Dm = h.shape[1]
    tm = _row_tile(T)
    n = len(pairs)

    def body(*refs):
        ab = refs[:2 * n]
        h_ref, g_ref, dres_ref, dh_ref, dg_ref, acc = refs[2 * n:]
        i, s = pl.program_id(0), pl.program_id(1)

        @pl.when(s == 0)
        def _():
            acc[...] = jnp.zeros_like(acc)
        for p in range(n):
            acc[...] += _dot_nt(ab[2 * p][...], ab[2 * p + 1][...])

        @pl.when(s == steps - 1)
        def _():
            dx, dg = _rms_bwd(h_ref[...], g_ref[...], acc[...])
            dh_ref[...] = _zero_pad_rows(dres_ref[...] + dx, i, tm)
            _acc8(dg_ref, dg, i == 0)

    row = pl.BlockSpec((tm, Dm), lambda i, s: (i, 0))
    if S > 1:
        amap, bmap = (lambda i, s: (s, i, 0)), (lambda i, s: (s, 0, 0))
    else:
        amap, bmap = (lambda i, s: (0, i, s)), (lambda i, s: (0, 0, s))
    in_specs, args = [], []
    for a, b in pairs:
        in_specs += [pl.BlockSpec((None, tm, tk), amap), pl.BlockSpec((None, Dm, tk), bmap)]
        args += [a, b]
    return pl.pallas_call(
        body, name="mm_nt_norm_bwd", grid=(T // tm, steps),
        in_specs=in_specs + [row, pl.BlockSpec((1, Dm), lambda i, s: (0, 0)), row],
        out_specs=[row, pl.BlockSpec((8, Dm), lambda i, s: (0, 0))],
        out_shape=[jax.ShapeDtypeStruct((T, Dm), f32), jax.ShapeDtypeStruct((8, Dm), f32)],
        scratch_shapes=[pltpu.VMEM((tm, Dm), f32)],
        compiler_params=_cparams(dimension_semantics=("arbitrary", "arbitrary")),
    )(*args, h, g, dres)


def mm_nt(a, b):
    T, K = a.shape
    N = b.shape[0]
    tm = _row_tile(T)

    def body(a_ref, b_ref, o_ref):
        o_ref[...] = _dot_nt(a_ref[...], b_ref[...])

    return pl.pallas_call(
        body, name="mm_nt", grid=(T // tm,),
        in_specs=[pl.BlockSpec((tm, K), lambda i: (i, 0)), pl.BlockSpec((N, K), lambda i: (0, 0))],
        out_specs=pl.BlockSpec((tm, N), lambda i: (i, 0)),
        out_shape=jax.ShapeDtypeStruct((T, N), f32),
        compiler_params=_cparams(dimension_semantics=("arbitrary",)),
    )(a, b)


def mm_tn(a, b, tn):
    Sa, T, M = a.shape
    Sb, _, N = b.shape
    S = max(Sa, Sb)
    tk = _row_tile(T)

    def body(a_ref, b_ref, o_ref):
        @pl.when(pl.program_id(2) == 0)
        def _():
            o_ref[...] = jnp.zeros_like(o_ref)
        o_ref[...] += _dot_tn(a_ref[...], b_ref[...])

    return pl.pallas_call(
        body, name="mm_tn", grid=(S, N // tn, T // tk),
        in_specs=[pl.BlockSpec((None, tk, M), (lambda s, j, k: (s, k, 0)) if Sa > 1 else (lambda s, j, k: (0, k, 0))),
                  pl.BlockSpec((None, tk, tn), (lambda s, j, k: (s, k, j)) if Sb > 1 else (lambda s, j, k: (0, k, j)))],
        out_specs=pl.BlockSpec((None, M, tn), lambda s, j, k: (s, 0, j)),
        out_shape=jax.ShapeDtypeStruct((S, M, N), f32),
        compiler_params=_cparams(dimension_semantics=("arbitrary", "arbitrary", "arbitrary")),
    )(a, b)


def adamw(w, g, m, v):
    R, C = w.shape
    tr = 512 if R % 512 == 0 else (256 if R % 256 == 0 else R)
    c1 = np.float32(1.0 - ADAM_B1 ** ADAM_STEP)
    c2 = np.float32(1.0 - ADAM_B2 ** ADAM_STEP)

    def body(w_ref, g_ref, m_ref, v_ref, d_ref, mo_ref, vo_ref):
        gv = g_ref[...]
        mn = ADAM_B1 * m_ref[...] + (1.0 - ADAM_B1) * gv
        vn = ADAM_B2 * v_ref[...] + (1.0 - ADAM_B2) * (gv * gv)
        mo_ref[...] = mn
        vo_ref[...] = vn
        d_ref[...] = -ADAM_LR * ((mn / c1) / (jnp.sqrt(vn / c2) + ADAM_EPS) + ADAM_WD * w_ref[...])

    spec = pl.BlockSpec((tr, C), lambda i: (i, 0))
    sh = jax.ShapeDtypeStruct((R, C), f32)
    return pl.pallas_call(
        body, name="adamw", grid=(R // tr,), in_specs=[spec] * 4, out_specs=[spec] * 3, out_shape=[sh] * 3,
        compiler_params=_cparams(dimension_semantics=("arbitrary",)),
    )(w, g, m, v)


_RET_LOG_GAMMA = [float(v) for v in np.log1p(-np.exp2(-5.0 - np.arange(N_HEAD_AB, dtype=np.float32))).astype(np.float32)]


def rot_tables(T):
    pos = jnp.arange(T, dtype=f32) - float(PAD)
    inv_freq = 1.0 / (10000.0 ** jnp.linspace(0.0, 1.0, DH_AB // 2, dtype=f32))
    ang = pos[:, None] * inv_freq[None, :]
    cos, sin = jnp.cos(ang), jnp.sin(ang)
    return jnp.repeat(cos, 2, axis=1), jnp.stack([-sin, sin], axis=-1).reshape(T, DH_AB)


def _swap_pairs(x):
    lane = lax.broadcasted_iota(jnp.int32, x.shape, 1)
    return jnp.where(lane % 2 == 0, pltpu.roll(x, x.shape[1] - 1, 1), pltpu.roll(x, 1, 1))


def _rot(x, c, s):
    return x * c + _swap_pairs(x) * s


def _rot_bwd(d, c, s):
    return d * c + _swap_pairs(d * s)


def _ret_mats(lg):
    i = lax.broadcasted_iota(jnp.int32, (BLK, BLK), 0).astype(f32)
    j = lax.broadcasted_iota(jnp.int32, (BLK, BLK), 1).astype(f32)
    diff = i - j
    decay = jnp.where(diff >= 0, jnp.exp(jnp.maximum(diff, 0.0) * lg), 0.0)
    xi = jnp.exp((i + 1.0) * lg)
    zeta = jnp.exp((BLK - 1.0 - i) * lg)
    return decay, xi, zeta, float(np.exp(np.float32(BLK * lg)))


def _ret_head(q_ref, k_ref, v_ref, cos, sin, h, Sp):
    sl = slice(DH_AB * h, DH_AB * (h + 1))
    decay, xi, zeta, gc = _ret_mats(_RET_LOG_GAMMA[h])
    q = _rot(q_ref[:, sl], cos, sin)
    k = _rot(k_ref[:, sl], cos, sin) * DH_AB ** -0.5
    v = v_ref[:, sl]
    P = _dot_nt(q, k) * decay
    ret = _dot(P, v) + _dot(q * xi, Sp)
    return sl, q, k, v, P, ret, decay, xi, zeta, gc


def _proj_spec(rows, width, col, rev=None):
    if rev is None:
        return pl.BlockSpec((rows, width), lambda n: (n, col))
    return pl.BlockSpec((rows, width), lambda n: (rev - n, col))


def ret_fwd(proj, cos, sin):
    T = proj.shape[0]
    NC = T // BLK
    W = N_HEAD_AB * DH_AB

    def body(q_ref, k_ref, v_ref, g_ref, cos_ref, sin_ref, o_ref, sall_ref, S):
        @pl.when(pl.program_id(0) == 0)
        def _():
            S[...] = jnp.zeros_like(S)
        cos_v, sin_v = cos_ref[...], sin_ref[...]
        for h in range(N_HEAD_AB):
            Sp = S[h]
            sall_ref[0, h] = Sp
            sl, q, k, v, P, ret, decay, xi, zeta, gc = _ret_head(q_ref, k_ref, v_ref, cos_v, sin_v, h, Sp)
            S[h] = Sp * gc + _dot_tn(k * zeta, v)
            mu = jnp.mean(ret, axis=-1, keepdims=True)
            cen = ret - mu
            y = cen * lax.rsqrt(jnp.mean(cen * cen, axis=-1, keepdims=True) + EPS)
            o_ref[:, sl] = (y * _silu(g_ref[:, sl])).astype(o_ref.dtype)

    tab = pl.BlockSpec((BLK, DH_AB), lambda n: (n, 0))
    return pl.pallas_call(
        body, name="ret_fwd", grid=(NC,),
        in_specs=[_proj_spec(BLK, W, 0), _proj_spec(BLK, W, 1), _proj_spec(BLK, W, 2), _proj_spec(BLK, W, 3), tab, tab],
        out_specs=[pl.BlockSpec((BLK, W), lambda n: (n, 0)), pl.BlockSpec((1, N_HEAD_AB, DH_AB, DH_AB), lambda n: (n, 0, 0, 0))],
        out_shape=[jax.ShapeDtypeStruct((T, W), bf16), jax.ShapeDtypeStruct((NC, N_HEAD_AB, DH_AB, DH_AB), f32)],
        scratch_shapes=[pltpu.VMEM((N_HEAD_AB, DH_AB, DH_AB), f32)],
        compiler_params=_cparams(dimension_semantics=("arbitrary",)),
    )(proj, proj, proj, proj, cos, sin)


def ret_bwd(proj, cos, sin, sall, dmix):
    T = proj.shape[0]
    NC = T // BLK
    W = N_HEAD_AB * DH_AB
    L = NC - 1

    def body(q_ref, k_ref, v_ref, g_ref, cos_ref, sin_ref, sall_ref, do_ref, dq_ref, dk_ref, dv_ref, dg_ref, dS):
        @pl.when(pl.program_id(0) == 0)
        def _():
            dS[...] = jnp.zeros_like(dS)
        cos_v, sin_v = cos_ref[...], sin_ref[...]
        for h in range(N_HEAD_AB):
            Sp = sall_ref[0, h]
            sl, q, k, v, P, ret, decay, xi, zeta, gc = _ret_head(q_ref, k_ref, v_ref, cos_v, sin_v, h, Sp)
            mu = jnp.mean(ret, axis=-1, keepdims=True)
            cen = ret - mu
            r = lax.rsqrt(jnp.mean(cen * cen, axis=-1, keepdims=True) + EPS)
            y = cen * r
            gate = g_ref[:, sl]
            sg = _sigmoid(gate)
            dout = do_ref[:, sl]
            dg_ref[:, sl] = dout * y * (sg * (1.0 + gate * (1.0 - sg)))
            dy = dout * (gate * sg)
            dO = r * (dy - jnp.mean(dy, axis=-1, keepdims=True) - y * jnp.mean(dy * y, axis=-1, keepdims=True))
            dSn = dS[h]
            dv_ref[:, sl] = _dot_tn(P, dO) + _dot(k * zeta, dSn)
            dP = _dot_nt(dO, v) * decay
            dq = _dot(dP, k) + _dot_nt(dO, Sp) * xi
            dk = _dot_tn(dP, q) + _dot_nt(v, dSn) * zeta
            dS[h] = dSn * gc + _dot_tn(q * xi, dO)
            dq_ref[:, sl] = _rot_bwd(dq, cos_v, sin_v)
            dk_ref[:, sl] = _rot_bwd(dk * DH_AB ** -0.5, cos_v, sin_v)

    tab = pl.BlockSpec((BLK, DH_AB), lambda n: (L - n, 0))
    out = pl.BlockSpec((BLK, W), lambda n: (L - n, 0))
    sh = jax.ShapeDtypeStruct((T, W), f32)
    return pl.pallas_call(
        body, name="ret_bwd", grid=(NC,),
        in_specs=[_proj_spec(BLK, W, 0, L), _proj_spec(BLK, W, 1, L), _proj_spec(BLK, W, 2, L), _proj_spec(BLK, W, 3, L), tab, tab,
                  pl.BlockSpec((1, N_HEAD_AB, DH_AB, DH_AB), lambda n: (L - n, 0, 0, 0)), _proj_spec(BLK, W, 0, L)],
        out_specs=[out] * 4, out_shape=[sh] * 4,
        scratch_shapes=[pltpu.VMEM((N_HEAD_AB, DH_AB, DH_AB), f32)],
        compiler_params=_cparams(dimension_semantics=("arbitrary",)),
    )(proj, proj, proj, proj, cos, sin, sall, dmix)


HALO = 8
GDN_K = 4
W_AB = N_HEAD_AB * DH_AB


def _gdn_rowwise(cq, ck, cv, gblk, alog, dtb, rmask):
    def l2n(x):
        return [x[:, DH_AB * h:DH_AB * (h + 1)] for h in range(N_HEAD_AB)]

    def norm(x):
        return x * lax.rsqrt(jnp.sum(x * x, axis=-1, keepdims=True) + EPS)

    qs = [norm(x) for x in l2n(_silu(cq))]
    ks = [norm(x) for x in l2n(_silu(ck))]
    lane = lax.broadcasted_iota(jnp.int32, gblk.shape, 1)
    beta = _sigmoid(gblk)
    g = -jnp.exp(alog) * _softplus(gblk + dtb)
    gates = jnp.where(lane < N_HEAD_AB, beta, jnp.where(lane < 2 * N_HEAD_AB, g, 0.0)) * rmask
    return qs, ks, _silu(cv), gates


def _row_mask(i, tr):
    rows = i * tr + lax.broadcasted_iota(jnp.int32, (tr, 1), 0)
    return (rows >= PAD).astype(f32)


def _conv_specs(tr, cols, nt, nxt=False):
    tiles = [pl.BlockSpec((tr, W_AB), functools.partial(lambda i, c: (i, c), c=c)) for c in cols]
    r = tr // HALO
    if nxt:
        halos = [pl.BlockSpec((HALO, W_AB), functools.partial(lambda i, c: (jnp.minimum((i + 1) * r, nt * r - 1), c), c=c)) for c in cols]
    else:
        halos = [pl.BlockSpec((HALO, W_AB), functools.partial(lambda i, c: (jnp.maximum(i * r - 1, 0), c), c=c)) for c in cols]
    return tiles, halos


def gdn_prep_fwd(proj, conv_w, alog, dtb):
    T = proj.shape[0]
    tr = BLK
    NT = T // tr

    def body(xq, xk, xv, hq, hk, hv, gb_ref, w_ref, al_ref, dt_ref, cq_o, ck_o, cv_o, q_o, k_o, v_o, gates_o, buf):
        i = pl.program_id(0)
        cs = []
        for p, (x_ref, h_ref, c_o) in enumerate(((xq, hq, cq_o), (xk, hk, ck_o), (xv, hv, cv_o))):
            buf[0:HALO, :] = jnp.where(i > 0, h_ref[...], 0.0)
            buf[HALO:, :] = x_ref[...]
            c = jnp.zeros((tr, W_AB), f32)
            for k in range(GDN_K):
                c = c + w_ref[k:k + 1, W_AB * p:W_AB * (p + 1)] * buf[pl.ds(HALO - GDN_K + 1 + k, tr), :]
            c_o[...] = c
            cs.append(c)
        qs, ks, v, gates = _gdn_rowwise(cs[0], cs[1], cs[2], gb_ref[...], al_ref[...], dt_ref[...], _row_mask(i, tr))
        for h in range(N_HEAD_AB):
            q_o[:, DH_AB * h:DH_AB * (h + 1)] = qs[h]
            k_o[:, DH_AB * h:DH_AB * (h + 1)] = ks[h]
        v_o[...] = v
        gates_o[...] = gates

    tiles, halos = _conv_specs(tr, (4, 5, 6), NT)
    vec = pl.BlockSpec((1, BLK), lambda i: (0, 0))
    wide = pl.BlockSpec((tr, W_AB), lambda i: (i, 0))
    sh = jax.ShapeDtypeStruct((T, W_AB), f32)
    return pl.pallas_call(
        body, name="gdn_prep_fwd", grid=(NT,),
        in_specs=tiles + halos + [pl.BlockSpec((tr, BLK), lambda i: (i, AB_IN_P // BLK - 1)),
                                  pl.BlockSpec((GDN_K, 3 * W_AB), lambda i: (0, 0)), vec, vec],
        out_specs=[wide] * 6 + [pl.BlockSpec((tr, BLK), lambda i: (i, 0))],
        out_shape=[sh] * 6 + [jax.ShapeDtypeStruct((T, BLK), f32)],
        scratch_shapes=[pltpu.VMEM((tr + HALO, W_AB), f32)],
        compiler_params=_cparams(dimension_semantics=("arbitrary",)),
    )(proj, proj, proj, proj, proj, proj, proj, conv_w, alog, dtb)


def gdn_prep_bwd(cq, ck, cv, proj, alog, dtb, dq, dk, dv, dgates):
    T = cq.shape[0]
    tr = BLK
    NT = T // tr

    def body(cq_r, ck_r, cv_r, gb_ref, al_ref, dt_ref, dq_r, dk_r, dv_r, dg_r, dcq_o, dck_o, dcv_o, dgb_o, dal_o, ddt_o):
        i = pl.program_id(0)
        mask = _row_mask(i, tr)
        _, vjp = jax.vjp(lambda a, b, c, d, e, f: _gdn_rowwise(a, b, c, d, e, f, mask),
                         cq_r[...], ck_r[...], cv_r[...], gb_ref[...], al_ref[...], dt_ref[...])
        heads = lambda r: [r[:, DH_AB * h:DH_AB * (h + 1)] for h in range(N_HEAD_AB)]
        dcq, dck, dcv, dgb, dal, ddt = vjp((heads(dq_r), heads(dk_r), dv_r[...], dg_r[...]))
        dcq_o[...] = dcq
        dck_o[...] = dck
        dcv_o[...] = dcv
        dgb_o[...] = dgb
        _acc8(dal_o, dal, i == 0)
        _acc8(ddt_o, ddt, i == 0)

    vec = pl.BlockSpec((1, BLK), lambda i: (0, 0))
    wide = pl.BlockSpec((tr, W_AB), lambda i: (i, 0))
    narrow = pl.BlockSpec((tr, BLK), lambda i: (i, 0))
    acc = pl.BlockSpec((8, BLK), lambda i: (0, 0))
    sh = jax.ShapeDtypeStruct((T, W_AB), f32)
    return pl.pallas_call(
        body, name="gdn_prep_bwd", grid=(NT,),
        in_specs=[wide] * 3 + [pl.BlockSpec((tr, BLK), lambda i: (i, AB_IN_P // BLK - 1)), vec, vec] + [wide] * 3 + [narrow],
        out_specs=[wide] * 3 + [narrow, acc, acc],
        out_shape=[sh] * 3 + [jax.ShapeDtypeStruct((T, BLK), f32)] + [jax.ShapeDtypeStruct((8, BLK), f32)] * 2,
        compiler_params=_cparams(dimension_semantics=("arbitrary",)),
    )(cq, ck, cv, proj, alog, dtb, dq, dk, dv, dgates)


def gdn_conv_bwd(dcq, dck, dcv, proj, conv_w):
    T = dcq.shape[0]
    tr = BLK
    NT = T // tr

    def body(dq_r, dk_r, dv_r, nq, nk, nv, xq, xk, xv, hq, hk, hv, w_ref, dxq_o, dxk_o, dxv_o, dw_o, bufd, bufx):
        i = pl.program_id(0)

        @pl.when(i == 0)
        def _():
            dw_o[...] = jnp.zeros_like(dw_o)
        parts = ((dq_r, nq, xq, hq, dxq_o), (dk_r, nk, xk, hk, dxk_o), (dv_r, nv, xv, hv, dxv_o))
        for p, (dc_r, n_r, x_r, h_r, dx_o) in enumerate(parts):
            dc = dc_r[...]
            bufd[0:tr, :] = dc
            bufd[tr:, :] = jnp.where(i < NT - 1, n_r[...], 0.0)
            bufx[0:HALO, :] = jnp.where(i > 0, h_r[...], 0.0)
            bufx[HALO:, :] = x_r[...]
            dx = jnp.zeros((tr, W_AB), f32)
            rows = []
            for k in range(GDN_K):
                dx = dx + w_ref[k:k + 1, W_AB * p:W_AB * (p + 1)] * bufd[pl.ds(GDN_K - 1 - k, tr), :]
                rows.append(jnp.sum(dc * bufx[pl.ds(HALO - GDN_K + 1 + k, tr), :], axis=0, keepdims=True))
            dx_o[...] = dx
            dw_o[:, W_AB * p:W_AB * (p + 1)] += jnp.concatenate(rows + [jnp.zeros((8 - GDN_K, W_AB), f32)], axis=0)

    wide = pl.BlockSpec((tr, W_AB), lambda i: (i, 0))
    r = tr // HALO
    nxt = pl.BlockSpec((HALO, W_AB), lambda i: (jnp.minimum((i + 1) * r, NT * r - 1), 0))
    tiles, halos = _conv_specs(tr, (4, 5, 6), NT)
    sh = jax.ShapeDtypeStruct((T, W_AB), f32)
    return pl.pallas_call(
        body, name="gdn_conv_bwd", grid=(NT,),
        in_specs=[wide] * 3 + [nxt] * 3 + tiles + halos + [pl.BlockSpec((GDN_K, 3 * W_AB), lambda i: (0, 0))],
        out_specs=[wide] * 3 + [pl.BlockSpec((8, 3 * W_AB), lambda i: (0, 0))],
        out_shape=[sh] * 3 + [jax.ShapeDtypeStruct((8, 3 * W_AB), f32)],
        scratch_shapes=[pltpu.VMEM((tr + HALO, W_AB), f32), pltpu.VMEM((tr + HALO, W_AB), f32)],
        compiler_params=_cparams(dimension_semantics=("arbitrary",)),
    )(dcq, dck, dcv, dcq, dck, dcv, proj, proj, proj, proj, proj, proj, conv_w)


def _tri_sum(x, upper):
    n = x.shape[0]
    r = lax.broadcasted_iota(jnp.int32, (n, n), 0)
    c = lax.broadcasted_iota(jnp.int32, (n, n), 1)
    tri = ((r <= c) if upper else (r >= c)).astype(bf16)
    hi = x.astype(bf16)
    lo = (x - hi.astype(f32)).astype(bf16)
    return jnp.dot(tri, hi, preferred_element_type=f32) + jnp.dot(tri, lo, preferred_element_type=f32)


@jax.custom_vjp
def _cumsum_rows(x):
    return _tri_sum(x, False)


_cumsum_rows.defvjp(lambda x: (_tri_sum(x, False), None), lambda _, g: (_tri_sum(g, True),))


@jax.custom_vjp
def _unit_lower_inv(a):
    n = a.shape[0]
    eye = (lax.broadcasted_iota(jnp.int32, (n, n), 0) == lax.broadcasted_iota(jnp.int32, (n, n), 1)).astype(f32)
    b = -a
    x = eye + b
    p = b
    for _ in range(int(np.log2(n)) - 1):
        p = _dot(p, p)
        x = x + _dot(x, p)
    return x


def _unit_lower_inv_fwd(a):
    t = _unit_lower_inv(a)
    return t, t


def _unit_lower_inv_bwd(t, dt):
    return (-_dot_nt(_dot_tn(t, dt), t),)


_unit_lower_inv.defvjp(_unit_lower_inv_fwd, _unit_lower_inv_bwd)


def _gdn_chunk(qs, ks, vs, gates, zs, onorm, Ss):
    C = gates.shape[0]
    ri = lax.broadcasted_iota(jnp.int32, (C, C), 0)
    ci = lax.broadcasted_iota(jnp.int32, (C, C), 1)
    incl, strict = ri >= ci, ri > ci
    gcum = _cumsum_rows(gates)
    gcum_t = gcum.T
    lane = lax.broadcasted_iota(jnp.int32, gates.shape, 1)
    sub = lax.broadcasted_iota(jnp.int32, gcum_t.shape, 0)
    last = lax.broadcasted_iota(jnp.int32, (C, 1), 0) == C - 1
    outs, nxt = [], []
    for h in range(N_HEAD_AB):
        bcol = jnp.sum(jnp.where(lane == h, gates, 0.0), axis=1, keepdims=True)
        gcol = jnp.sum(jnp.where(lane == N_HEAD_AB + h, gcum, 0.0), axis=1, keepdims=True)
        grow = jnp.sum(jnp.where(sub == N_HEAD_AB + h, gcum_t, 0.0), axis=0, keepdims=True)
        gl = jnp.sum(jnp.where(last, gcol, 0.0), axis=0, keepdims=True)
        decay = jnp.where(incl, jnp.exp(jnp.where(incl, gcol - grow, 0.0)), 0.0)
        q = qs[h] * DH_AB ** -0.5
        k, v, S = ks[h], vs[h], Ss[h]
        kb = k * bcol
        a = jnp.where(strict, _dot_nt(kb, k) * decay, 0.0)
        t = _unit_lower_inv(a)
        eg = jnp.exp(gcol)
        u = _dot(t, v * bcol)
        w = _dot(t, kb * eg)
        qk = jnp.where(incl, _dot_nt(q, k) * decay, 0.0)
        v_new = u - _dot(w, S)
        o = _dot(q * eg, S) + _dot(qk, v_new)
        nxt.append(S * jnp.exp(gl) + _dot_tn(k * jnp.exp(gl - gcol), v_new))
        outs.append(_rms(o, onorm) * _silu(zs[h]))
    return outs, nxt


def _heads(ref, r0=None):
    rows = slice(None) if r0 is None else slice(r0, r0 + GDN_C)
    return [ref[rows, DH_AB * h:DH_AB * (h + 1)] for h in range(N_HEAD_AB)]


GDN_PER_STEP = 2


def gdn_chunk_fwd(q, k, v, gates, proj, onorm):
    T = q.shape[0]
    P = GDN_PER_STEP
    C = GDN_C * P
    NC = T // GDN_C

    def body(q_r, k_r, v_r, g_r, z_r, on_r, o_ref, sall_ref, S):
        @pl.when(pl.program_id(0) == 0)
        def _():
            S[...] = jnp.zeros_like(S)
        Ss = [S[h] for h in range(N_HEAD_AB)]
        for j in range(P):
            r0 = GDN_C * j
            for h in range(N_HEAD_AB):
                sall_ref[j, h] = Ss[h]
            outs, Ss = _gdn_chunk(_heads(q_r, r0), _heads(k_r, r0), _heads(v_r, r0), g_r[r0:r0 + GDN_C, :], _heads(z_r, r0), on_r[...], Ss)
            for h in range(N_HEAD_AB):
                o_ref[r0:r0 + GDN_C, DH_AB * h:DH_AB * (h + 1)] = outs[h].astype(o_ref.dtype)
        for h in range(N_HEAD_AB):
            S[h] = Ss[h]

    wide = pl.BlockSpec((C, W_AB), lambda n: (n, 0))
    return pl.pallas_call(
        body, name="gdn_chunk_fwd", grid=(NC // P,),
        in_specs=[wide] * 3 + [pl.BlockSpec((C, BLK), lambda n: (n, 0)), pl.BlockSpec((C, W_AB), lambda n: (n, 7)),
                               pl.BlockSpec((1, DH_AB), lambda n: (0, 0))],
        out_specs=[wide, pl.BlockSpec((P, N_HEAD_AB, DH_AB, DH_AB), lambda n: (n, 0, 0, 0))],
        out_shape=[jax.ShapeDtypeStruct((T, W_AB), bf16), jax.ShapeDtypeStruct((NC, N_HEAD_AB, DH_AB, DH_AB), f32)],
        scratch_shapes=[pltpu.VMEM((N_HEAD_AB, DH_AB, DH_AB), f32)],
        compiler_params=_cparams(dimension_semantics=("arbitrary",)),
    )(q, k, v, gates, proj, onorm)


def gdn_chunk_bwd(q, k, v, gates, proj, onorm, sall, dmix):
    T = q.shape[0]
    P = GDN_PER_STEP
    C = GDN_C * P
    NC = T // GDN_C
    L = NC // P - 1

    def body(q_r, k_r, v_r, g_r, z_r, on_r, sall_r, do_r, dq_o, dk_o, dv_o, dz_o, dg_o, don_o, dS):
        @pl.when(pl.program_id(0) == 0)
        def _():
            dS[...] = jnp.zeros_like(dS)
        dSs = [dS[h] for h in range(N_HEAD_AB)]
        don_sum = jnp.zeros((1, DH_AB), f32)
        for j in reversed(range(P)):
            r0 = GDN_C * j
            Ss = [sall_r[j, h] for h in range(N_HEAD_AB)]
            _, vjp = jax.vjp(_gdn_chunk, _heads(q_r, r0), _heads(k_r, r0), _heads(v_r, r0), g_r[r0:r0 + GDN_C, :], _heads(z_r, r0),
                             on_r[...], Ss)
            dqs, dks, dvs, dg, dzs, don, dSs = vjp((_heads(do_r, r0), dSs))
            for h in range(N_HEAD_AB):
                sl = slice(DH_AB * h, DH_AB * (h + 1))
                dq_o[r0:r0 + GDN_C, sl] = dqs[h]
                dk_o[r0:r0 + GDN_C, sl] = dks[h]
                dv_o[r0:r0 + GDN_C, sl] = dvs[h]
                dz_o[r0:r0 + GDN_C, sl] = dzs[h]
            dg_o[r0:r0 + GDN_C, :] = dg
            don_sum = don_sum + don
        for h in range(N_HEAD_AB):
            dS[h] = dSs[h]
        _acc8(don_o, don_sum, pl.program_id(0) == 0)

    wide = pl.BlockSpec((C, W_AB), lambda n: (L - n, 0))
    sh = jax.ShapeDtypeStruct((T, W_AB), f32)
    return pl.pallas_call(
        body, name="gdn_chunk_bwd", grid=(NC // P,),
        in_specs=[wide] * 3 + [pl.BlockSpec((C, BLK), lambda n: (L - n, 0)), pl.BlockSpec((C, W_AB), lambda n: (L - n, 7)),
                               pl.BlockSpec((1, DH_AB), lambda n: (0, 0)),
                               pl.BlockSpec((P, N_HEAD_AB, DH_AB, DH_AB), lambda n: (L - n, 0, 0, 0)),
                               pl.BlockSpec((C, W_AB), lambda n: (L - n, 1))],
        out_specs=[wide] * 4 + [pl.BlockSpec((C, BLK), lambda n: (L - n, 0)), pl.BlockSpec((8, DH_AB), lambda n: (0, 0))],
        out_shape=[sh] * 4 + [jax.ShapeDtypeStruct((T, BLK), f32), jax.ShapeDtypeStruct((8, DH_AB), f32)],
        scratch_shapes=[pltpu.VMEM((N_HEAD_AB, DH_AB, DH_AB), f32)],
        compiler_params=_cparams(dimension_semantics=("arbitrary",)),
    )(q, k, v, gates, proj, onorm, sall, dmix)


DH_CD = 64
SWA_G = 4
SWA_KV = 2
W_CD = 512


def _swa_block(q_ref, km, kp, kc, vm, vp, vc, sinks, g, n):
    scale = DH_CD ** -0.5
    ks = slice(DH_CD * g, DH_CD * (g + 1))
    Q = jnp.concatenate([q_ref[:, DH_CD * (SWA_G * g + j):DH_CD * (SWA_G * g + j + 1)] for j in range(SWA_G)], axis=0) * scale
    K3 = jnp.concatenate([km[:, ks], kp[:, ks], kc[:, ks]], axis=0)
    V3 = jnp.concatenate([vm[:, ks], vp[:, ks], vc[:, ks]], axis=0)
    s = _dot_nt(Q, K3)
    shp = s.shape
    row = lax.broadcasted_iota(jnp.int32, shp, 0)
    col = lax.broadcasted_iota(jnp.int32, shp, 1)
    i, part, j = row % BLK, col // BLK, col % BLK
    meta = (part == 0) & (j >= PAD) & ((j <= i) | (n > 0))
    prev = (part == 1) & (j > i) & (n >= 2)
    cur = (part == 2) & (j <= i) & (n >= 1)
    valid = meta | prev | cur
    grp = lax.broadcasted_iota(jnp.int32, (shp[0], 1), 0) // BLK
    sink = jnp.zeros((shp[0], 1), f32)
    for jj in range(SWA_G):
        sink = jnp.where(grp == jj, sinks[SWA_G * g + jj], sink)
    m = jnp.maximum(jnp.max(jnp.where(valid, s, NEG), axis=1, keepdims=True), sink)
    p = jnp.where(valid, jnp.exp(jnp.where(valid, s - m, 0.0)), 0.0)
    es = jnp.exp(sink - m)
    denom = jnp.sum(p, axis=1, keepdims=True) + es
    return Q, K3, V3, p / denom, es / denom, grp


def _swa_in_specs(rev=None):
    row = (lambda n: n) if rev is None else (lambda n: rev - n)
    kcol, vcol = 512 // BLK, 640 // BLK
    specs = [pl.BlockSpec((BLK, W_CD), lambda n: (row(n), 0))]
    for col in (kcol, vcol):
        specs += [pl.BlockSpec((BLK, BLK), functools.partial(lambda n, c: (0, c), c=col)),
                  pl.BlockSpec((BLK, BLK), functools.partial(lambda n, c: (jnp.maximum(row(n) - 1, 0), c), c=col)),
                  pl.BlockSpec((BLK, BLK), functools.partial(lambda n, c: (row(n), c), c=col))]
    return specs + [pl.BlockSpec(memory_space=pltpu.SMEM)]


def swa_fwd(proj, sinks):
    T = proj.shape[0]
    NB = T // BLK

    def body(q_ref, km, kp, kc, vm, vp, vc, sinks_ref, o_ref):
        n = pl.program_id(0)
        for g in range(SWA_KV):
            Q, K3, V3, pn, ps, grp = _swa_block(q_ref, km, kp, kc, vm, vp, vc, sinks_ref, g, n)
            o = _dot(pn, V3)
            for j in range(SWA_G):
                hd = SWA_G * g + j
                o_ref[:, DH_CD * hd:DH_CD * (hd + 1)] = o[BLK * j:BLK * (j + 1), :]

    return pl.pallas_call(
        body, name="swa_fwd", grid=(NB,), in_specs=_swa_in_specs(),
        out_specs=pl.BlockSpec((BLK, W_CD), lambda n: (n, 0)), out_shape=jax.ShapeDtypeStruct((T, W_CD), f32),
        compiler_params=_cparams(dimension_semantics=("arbitrary",)),
    )(proj, proj, proj, proj, proj, proj, proj, sinks)


def swa_bwd(proj, sinks, dmix):
    T = proj.shape[0]
    NB = T // BLK
    KV = 2 * SWA_KV * DH_CD

    def body(q_ref, km, kp, kc, vm, vp, vc, sinks_ref, do_ref, dq_ref, cur_ref, prev_ref, meta_ref, ds_ref):
        n = pl.program_id(0)

        @pl.when(n == 0)
        def _():
            meta_ref[...] = jnp.zeros_like(meta_ref)
            ds_ref[...] = jnp.zeros_like(ds_ref)
        rows = []
        for g in range(SWA_KV):
            Q, K3, V3, pn, ps, grp = _swa_block(q_ref, km, kp, kc, vm, vp, vc, sinks_ref, g, n)
            dO = jnp.concatenate([do_ref[:, DH_CD * (SWA_G * g + j):DH_CD * (SWA_G * g + j + 1)] for j in range(SWA_G)], axis=0)
            dP = _dot_nt(dO, V3)
            delta = jnp.sum(pn * dP, axis=1, keepdims=True)
            dS = pn * (dP - delta)
            dQ = _dot(dS, K3) * DH_CD ** -0.5
            dK3 = _dot_tn(dS, Q)
            dV3 = _dot_tn(pn, dO)
            dsk = -ps * delta
            for j in range(SWA_G):
                hd = SWA_G * g + j
                dq_ref[:, DH_CD * hd:DH_CD * (hd + 1)] = dQ[BLK * j:BLK * (j + 1), :]
                rows.append(jnp.broadcast_to(jnp.sum(jnp.where(grp == j, dsk, 0.0), axis=0, keepdims=True), (1, BLK)))
            kcols = slice(DH_CD * g, DH_CD * (g + 1))
            vcols = slice(SWA_KV * DH_CD + DH_CD * g, SWA_KV * DH_CD + DH_CD * (g + 1))
            meta_ref[:, kcols] += dK3[0:BLK]
            meta_ref[:, vcols] += dV3[0:BLK]
            prev_ref[:, kcols] = dK3[BLK:2 * BLK]
            prev_ref[:, vcols] = dV3[BLK:2 * BLK]
            cur_ref[:, kcols] = dK3[2 * BLK:]
            cur_ref[:, vcols] = dV3[2 * BLK:]
        ds_ref[...] += jnp.concatenate(rows, axis=0)

    kv = pl.BlockSpec((BLK, KV), lambda n: (n, 0))
    return pl.pallas_call(
        body, name="swa_bwd", grid=(NB,),
        in_specs=_swa_in_specs() + [pl.BlockSpec((BLK, W_CD), lambda n: (n, 0))],
        out_specs=[pl.BlockSpec((BLK, W_CD), lambda n: (n, 0)), kv, kv, pl.BlockSpec((BLK, KV), lambda n: (0, 0)),
                   pl.BlockSpec((8, BLK), lambda n: (0, 0))],
        out_shape=[jax.ShapeDtypeStruct((T, W_CD), f32), jax.ShapeDtypeStruct((T, KV), f32), jax.ShapeDtypeStruct((T, KV), f32),
                   jax.ShapeDtypeStruct((BLK, KV), f32), jax.ShapeDtypeStruct((8, BLK), f32)],
        compiler_params=_cparams(dimension_semantics=("arbitrary",)),
    )(proj, proj, proj, proj, proj, proj, proj, sinks, dmix)


SB_PAIR = 2


def _sb_consts():
    r = lax.broadcasted_iota(jnp.int32, (BLK, BLK), 0)
    c = lax.broadcasted_iota(jnp.int32, (BLK, BLK), 1)
    return r, c, (r > c).astype(bf16), (r >= c).astype(bf16)


def _sb_block(q, kb, n, m, r, c):
    z = _dot_nt(q, kb)
    valid = ((m * BLK + c) < (n * BLK + r)) & ((m * BLK + c) >= PAD)
    sp = _softplus(z)
    return z, valid, jnp.where(valid, -sp, 0.0), sp


def _sb_specs(T):
    qcol, kcol, vcol = 768 // BLK, 1280 // BLK, 1792 // BLK
    return [pl.BlockSpec((BLK, BLK), lambda hp, n: (n, qcol + hp)),
            pl.BlockSpec((T, BLK), lambda hp, n: (0, kcol + hp)),
            pl.BlockSpec((T, BLK), lambda hp, n: (0, vcol + hp))]


def sb_fwd(proj):
    T = proj.shape[0]
    NB = T // BLK

    def body(q_ref, k_ref, v_ref, o_ref):
        n = pl.program_id(1)
        r, c, m_gt, _ = _sb_consts()
        heads = [slice(DH_CD * hh, DH_CD * (hh + 1)) for hh in range(SB_PAIR)]
        qs = [q_ref[:, cols] * DH_CD ** -0.5 for cols in heads]

        def cond(carry):
            m, runs, _ = carry
            return jnp.logical_and(m >= 0, jnp.max(jnp.maximum(runs[0], runs[1])) > SB_EXIT)

        def step(carry):
            m, runs, accs = carry
            off = pl.multiple_of(m * BLK, BLK)
            new_runs, new_accs = [], []
            for hh, cols in enumerate(heads):
                kb = k_ref[pl.ds(off, BLK), cols]
                vb = v_ref[pl.ds(off, BLK), cols]
                z, valid, l, sp = _sb_block(qs[hh], kb, n, m, r, c)
                e = (z - sp) + _dot2(l, m_gt) + runs[hh]
                a = jnp.where(valid, jnp.exp(jnp.where(valid, e, 0.0)), 0.0)
                new_runs.append(runs[hh] + jnp.sum(l, axis=1, keepdims=True))
                new_accs.append(accs[hh] + _dot(a, vb))
            return m - 1, tuple(new_runs), tuple(new_accs)

        zero = jnp.zeros((BLK, 1), f32)
        acc0 = jnp.zeros((BLK, DH_CD), f32)
        _, _, accs = lax.while_loop(cond, step, (n, (zero, zero), (acc0, acc0)))
        for hh, cols in enumerate(heads):
            o_ref[:, cols] = accs[hh]

    return pl.pallas_call(
        body, name="sb_fwd", grid=(W_CD // BLK, NB), in_specs=_sb_specs(T),
        out_specs=pl.BlockSpec((BLK, BLK), lambda hp, n: (n, hp)), out_shape=jax.ShapeDtypeStruct((T, W_CD), f32),
        compiler_params=_cparams(dimension_semantics=("arbitrary", "arbitrary")),
    )(proj, proj, proj)


def sb_bwd(proj, o, dmix):
    T = proj.shape[0]
    NB = T // BLK

    def body(q_ref, k_ref, v_ref, o_ref, do_ref, dq_ref, dk_ref, dv_ref):
        n = pl.program_id(1)

        @pl.when(n == 0)
        def _():
            dk_ref[...] = jnp.zeros_like(dk_ref)
            dv_ref[...] = jnp.zeros_like(dv_ref)
        r, c, m_gt, m_ge = _sb_consts()
        heads = [slice(DH_CD * hh, DH_CD * (hh + 1)) for hh in range(SB_PAIR)]
        qs = [q_ref[:, cols] * DH_CD ** -0.5 for cols in heads]
        dOs = [do_ref[:, cols].astype(bf16) for cols in heads]
        deltas = [jnp.sum(dOs[hh].astype(f32) * o_ref[:, cols], axis=1, keepdims=True) for hh, cols in enumerate(heads)]

        def cond(carry):
            m, runs = carry[0], carry[1]
            return jnp.logical_and(m >= 0, jnp.max(jnp.maximum(runs[0], runs[1])) > SB_EXIT)

        def step(carry):
            m, runs, runs_e, dqs = carry
            off = pl.multiple_of(m * BLK, BLK)
            new_runs, new_runs_e, new_dqs, dks, dvs = [], [], [], [], []
            for hh, cols in enumerate(heads):
                kb = k_ref[pl.ds(off, BLK), cols]
                vb = v_ref[pl.ds(off, BLK), cols]
                z, valid, l, sp = _sb_block(qs[hh], kb, n, m, r, c)
                e = (z - sp) + _dot2(l, m_gt) + runs[hh]
                a = jnp.where(valid, jnp.exp(jnp.where(valid, e, 0.0)), 0.0).astype(bf16)
                E = a.astype(f32) * _dot_nt(dOs[hh], vb)
                F = deltas[hh] - runs_e[hh] - _dot2(E, m_ge)
                sig = jnp.exp(z - sp)
                live = jnp.logical_and(valid, jnp.max(runs[hh]) > SB_EXIT)
                dz = jnp.where(live, E * (1.0 - sig) - F * sig, 0.0)
                dks.append(_dot_tn(dz, qs[hh]))
                dvs.append(_dot_tn(a, dOs[hh]))
                new_runs.append(runs[hh] + jnp.sum(l, axis=1, keepdims=True))
                new_runs_e.append(runs_e[hh] + jnp.sum(E, axis=1, keepdims=True))
                new_dqs.append(dqs[hh] + _dot(dz, kb))
            dk_ref[pl.ds(off, BLK), :] += jnp.concatenate(dks, axis=1)
            dv_ref[pl.ds(off, BLK), :] += jnp.concatenate(dvs, axis=1)
            return m - 1, tuple(new_runs), tuple(new_runs_e), tuple(new_dqs)

        zero = jnp.zeros((BLK, 1), f32)
        dq0 = jnp.zeros((BLK, DH_CD), f32)
        res = lax.while_loop(cond, step, (n, (zero, zero), (zero, zero), (dq0, dq0)))
        for hh, cols in enumerate(heads):
            dq_ref[:, cols] = res[3][hh] * DH_CD ** -0.5

    blk = pl.BlockSpec((BLK, BLK), lambda hp, n: (n, hp))
    full = pl.BlockSpec((T, BLK), lambda hp, n: (0, hp))
    sh = jax.ShapeDtypeStruct((T, W_CD), f32)
    return pl.pallas_call(
        body, name="sb_bwd", grid=(W_CD // BLK, NB),
        in_specs=_sb_specs(T) + [blk, pl.BlockSpec((BLK, BLK), lambda hp, n: (n, W_CD // BLK + hp))],
        out_specs=[blk, full, full], out_shape=[sh] * 3,
        compiler_params=_cparams(dimension_semantics=("arbitrary", "arbitrary")),
    )(proj, proj, proj, o, dmix)


def cd_assemble(dcq, cur, prev, meta, dsq, dsk, dsv):
    T = dcq.shape[0]
    NB = T // BLK
    KV = cur.shape[1]

    def body(dcq_r, cur_r, nxt_r, meta_r, dsq_r, dsk_r, dsv_r, o_ref):
        n = pl.program_id(0)
        kv = cur_r[...] + jnp.where(n < NB - 1, nxt_r[...], 0.0) + jnp.where(n == 0, meta_r[...], 0.0)
        o_ref[:, 0:W_CD] = dcq_r[...].astype(o_ref.dtype)
        o_ref[:, W_CD:W_CD + KV] = kv.astype(o_ref.dtype)
        for j, ref in enumerate((dsq_r, dsk_r, dsv_r)):
            o_ref[:, W_CD + KV + W_CD * j:W_CD + KV + W_CD * (j + 1)] = ref[...].astype(o_ref.dtype)

    wide = pl.BlockSpec((BLK, W_CD), lambda n: (n, 0))
    return pl.pallas_call(
        body, name="cd_assemble", grid=(NB,),
        in_specs=[wide, pl.BlockSpec((BLK, KV), lambda n: (n, 0)), pl.BlockSpec((BLK, KV), lambda n: (jnp.minimum(n + 1, NB - 1), 0)),
                  pl.BlockSpec((BLK, KV), lambda n: (0, 0)), wide, wide, wide],
        out_specs=pl.BlockSpec((BLK, CD_IN), lambda n: (n, 0)), out_shape=jax.ShapeDtypeStruct((T, CD_IN), bf16),
        compiler_params=_cparams(dimension_semantics=("arbitrary",)),
    )(dcq, cur, prev, meta, dsq, dsk, dsv)


def loss_grad(h, target):
    T, Dm = h.shape
    NB = T // BLK

    def body(h_ref, t_ref, dh_ref, l_ref):
        n = pl.program_id(0)

        @pl.when(n == 0)
        def _():
            dh_ref[...] = jnp.zeros_like(dh_ref)
            l_ref[...] = jnp.zeros_like(l_ref)

        @pl.when(n > 0)
        def _():
            err = h_ref[...] - t_ref[...]
            dh_ref[...] = err * (1.0 / Dm)
            part = 0.5 * jnp.sum(jnp.mean(err * err, axis=-1, keepdims=True), axis=0, keepdims=True)
            l_ref[...] += jnp.broadcast_to(part, l_ref.shape)

    row = pl.BlockSpec((BLK, Dm), lambda n: (n, 0))
    return pl.pallas_call(
        body, name="loss_grad", grid=(NB,),
        in_specs=[row, pl.BlockSpec((BLK, Dm), lambda n: (jnp.maximum(n - 1, 0), 0))],
        out_specs=[row, pl.BlockSpec((8, BLK), lambda n: (0, 0))],
        out_shape=[jax.ShapeDtypeStruct((T, Dm), f32), jax.ShapeDtypeStruct((8, BLK), f32)],
        compiler_params=_cparams(dimension_semantics=("arbitrary",)),
    )(h, target)


SUM_ROWS = 256
_MESH = pl.DeviceIdType.MESH
_ANY = pl.BlockSpec(memory_space=pl.ANY)


def _place():
    return lax.axis_index("x"), lax.axis_index("y"), lax.axis_index("c")


def _other_chips(x, y):
    return [(1 - x, y, 2 * (1 - x) + y), (x, 1 - y, 2 * x + 1 - y), (1 - x, 1 - y, 2 * (1 - x) + 1 - y)]


def gather_weights(wbuf, sbuf):
    def body(w_ref, s_ref, out_ref, outs_ref, send_sems, recv_sems):
        x, y, c = _place()
        p = 2 * x + y
        chips = _other_chips(x, y)
        sibling = (x, y, 1 - c)

        def copy(k, src, dst, to):
            return pltpu.make_async_remote_copy(src_ref=src, dst_ref=dst, send_sem=send_sems.at[k], recv_sem=recv_sems.at[k],
                                                device_id=to, device_id_type=_MESH)

        sends = [copy(9, w_ref, out_ref.at[p], sibling), copy(10, s_ref, outs_ref.at[p], sibling)]
        for j, (qx, qy, q) in enumerate(chips):
            sends.append(copy(j, w_ref.at[c], out_ref.at[p, c], (qx, qy, c)))
            sends.append(copy(3 + j, s_ref, outs_ref.at[p], (qx, qy, c)))
        for cp in sends:
            cp.start()
        for j, (qx, qy, q) in enumerate(chips):
            copy(j, w_ref.at[c], out_ref.at[q, c], (qx, qy, c)).wait_recv()
            fwd = copy(6 + j, out_ref.at[q, c], out_ref.at[q, c], sibling)
            fwd.start()
            sends.append(fwd)
        for j, (qx, qy, q) in enumerate(chips):
            copy(3 + j, s_ref, outs_ref.at[q], (qx, qy, c)).wait_recv()
            copy(6 + j, out_ref.at[q, 1 - c], out_ref.at[q, 1 - c], sibling).wait_recv()
        copy(9, w_ref, out_ref.at[p], sibling).wait_recv()
        copy(10, s_ref, outs_ref.at[p], sibling).wait_recv()
        for cp in sends:
            cp.wait_send()

    return pl.pallas_call(
        body, name="gather_weights", in_specs=[_ANY, _ANY], out_specs=[_ANY, _ANY],
        out_shape=[jax.ShapeDtypeStruct((4,) + wbuf.shape, wbuf.dtype), jax.ShapeDtypeStruct((4,) + sbuf.shape, sbuf.dtype)],
        scratch_shapes=[pltpu.SemaphoreType.DMA((11,)), pltpu.SemaphoreType.DMA((11,))],
    )(wbuf, sbuf)


def pair_exchange(g):
    S, _, H, Cw = g.shape

    def body(g_ref, out_ref, send_sem, recv_sem):
        x, y, c = _place()
        cp = pltpu.make_async_remote_copy(src_ref=g_ref.at[:, 1 - c], dst_ref=out_ref, send_sem=send_sem, recv_sem=recv_sem,
                                          device_id=(x, y, 1 - c), device_id_type=_MESH)
        cp.start()
        cp.wait()

    return pl.pallas_call(
        body, name="pair_exchange", in_specs=[_ANY], out_specs=_ANY, out_shape=jax.ShapeDtypeStruct((S, H, Cw), g.dtype),
        scratch_shapes=[pltpu.SemaphoreType.DMA, pltpu.SemaphoreType.DMA],
    )(g)


def pair_sum(g, got, c):
    S, _, H, Cw = g.shape
    tb = 3 * SUM_ROWS if H % (3 * SUM_ROWS) == 0 else SUM_ROWS

    def body(c_ref, a_ref, b_ref, o_ref):
        o_ref[...] = (a_ref[...].astype(f32) + b_ref[...].astype(f32)).astype(o_ref.dtype)

    spec = pl.BlockSpec((None, tb, Cw), lambda s, i, c_ref: (s, i, 0))
    return pl.pallas_call(
        body, name="pair_sum",
        grid_spec=pltpu.PrefetchScalarGridSpec(
            num_scalar_prefetch=1, grid=(S, H // tb),
            in_specs=[pl.BlockSpec((None, None, tb, Cw), lambda s, i, c_ref: (s, c_ref[0], i, 0)), spec], out_specs=spec),
        out_shape=jax.ShapeDtypeStruct((S, H, Cw), g.dtype),
        compiler_params=_cparams(dimension_semantics=("arbitrary", "arbitrary")),
    )(c, g, got)


def chip_exchange(hsum):
    S, H, Cw = hsum.shape

    def body(h_ref, out_ref, send_sems, recv_sems):
        x, y, c = _place()
        sends = []
        for j, (qx, qy, q) in enumerate(_other_chips(x, y)):
            cp = pltpu.make_async_remote_copy(src_ref=h_ref.at[q], dst_ref=out_ref.at[j], send_sem=send_sems.at[j],
                                              recv_sem=recv_sems.at[j], device_id=(qx, qy, c), device_id_type=_MESH)
            cp.start()
            sends.append(cp)
        for cp in sends:
            cp.wait()

    return pl.pallas_call(
        body, name="chip_exchange", in_specs=[_ANY], out_specs=_ANY, out_shape=jax.ShapeDtypeStruct((3, H, Cw), hsum.dtype),
        scratch_shapes=[pltpu.SemaphoreType.DMA((3,)), pltpu.SemaphoreType.DMA((3,))],
    )(hsum)


def chip_sum(hsum, parts, p):
    S, H, Cw = parts.shape
    tb = 3 * SUM_ROWS if H % (3 * SUM_ROWS) == 0 else SUM_ROWS

    def body(p_ref, own_ref, parts_ref, o_ref):
        acc = own_ref[...].astype(f32)
        for s in range(S):
            acc = acc + parts_ref[s].astype(f32)
        o_ref[...] = acc

    return pl.pallas_call(
        body, name="chip_sum",
        grid_spec=pltpu.PrefetchScalarGridSpec(
            num_scalar_prefetch=1, grid=(H // tb,),
            in_specs=[pl.BlockSpec((None, tb, Cw), lambda i, p_ref: (p_ref[0], i, 0)), pl.BlockSpec((S, tb, Cw), lambda i, p_ref: (0, i, 0))],
            out_specs=pl.BlockSpec((tb, Cw), lambda i, p_ref: (i, 0))),
        out_shape=jax.ShapeDtypeStruct((H, Cw), f32),
        compiler_params=_cparams(dimension_semantics=("arbitrary",)),
    )(p, hsum, parts)


def pair_gather(rsum):
    def body(r_ref, out_ref, send_sem, recv_sem):
        x, y, c = _place()
        cp = pltpu.make_async_remote_copy(src_ref=r_ref, dst_ref=out_ref, send_sem=send_sem, recv_sem=recv_sem,
                                          device_id=(x, y, 1 - c), device_id_type=_MESH)
        cp.start()
        cp.wait()

    return pl.pallas_call(
        body, name="pair_gather", in_specs=[_ANY], out_specs=_ANY, out_shape=jax.ShapeDtypeStruct(rsum.shape, rsum.dtype),
        scratch_shapes=[pltpu.SemaphoreType.DMA, pltpu.SemaphoreType.DMA],
    )(rsum)


def small_reduce(src):
    S, RS, Cw = src.shape

    def body(src_ref, out_ref, recv, send_sems, recv_sems):
        x, y, c = _place()
        me = 4 * x + 2 * y + c
        p = 2 * x + y
        recv[me] = src_ref[p]
        flips = [(fx, fy, fc) for fx in (0, 1) for fy in (0, 1) for fc in (0, 1)][1:]
        sends = []
        for k, (fx, fy, fc) in enumerate(flips):
            tx, ty, tc = (1 - x if fx else x), (1 - y if fy else y), (1 - c if fc else c)
            cp = pltpu.make_async_remote_copy(src_ref=src_ref.at[2 * tx + ty], dst_ref=recv.at[me], send_sem=send_sems.at[k],
                                              recv_sem=recv_sems.at[me], device_id=(tx, ty, tc), device_id_type=_MESH)
            cp.start()
            sends.append(cp)
        for k, (fx, fy, fc) in enumerate(flips):
            tx, ty, tc = (1 - x if fx else x), (1 - y if fy else y), (1 - c if fc else c)
            frm = 4 * tx + 2 * ty + tc
            pltpu.make_async_remote_copy(src_ref=src_ref.at[p], dst_ref=recv.at[frm], send_sem=send_sems.at[k],
                                         recv_sem=recv_sems.at[frm], device_id=(tx, ty, tc), device_id_type=_MESH).wait_recv()
        for cp in sends:
            cp.wait_send()
        acc = recv[0]
        for d in range(1, 8):
            acc = acc + recv[d]
        out_ref[...] = acc

    vm = pl.BlockSpec(memory_space=pltpu.VMEM)
    return pl.pallas_call(
        body, name="small_reduce", in_specs=[vm], out_specs=vm, out_shape=jax.ShapeDtypeStruct((RS, Cw), f32),
        scratch_shapes=[pltpu.VMEM((8, RS, Cw), f32), pltpu.SemaphoreType.DMA((7,)), pltpu.SemaphoreType.DMA((8,))],
    )(src)


def _row(v):
    return v.reshape(1, -1)


def _ffn_fwd(h, g_pre, g_post, wg, wu, wd):
    u, G, U, a = ffn_up(h, _row(g_pre), wg, wu)
    y, h_new = proj_norm_res(a, wd, h, _row(g_post), 0.5)
    return h_new, (h, u, G, U, a, y)


def _ffn_bwd(dh, saved, g_pre, g_post, wg, wu, wd):
    h, u, G, U, a, y = saved
    F = wg.shape[2]
    dy, dg_post = post_norm_bwd(y, _row(g_post), dh, 0.5)
    dG, dU = ffn_bwd_act(dy, wd, G, U)
    dwd = mm_tn(a, dy[None], D)
    dwg = mm_tn(u[None], dG, F)
    dwu = mm_tn(u[None], dU, F)
    dh_new, dg_pre = mm_nt_norm_bwd([(dG, wg), (dU, wu)], h, _row(g_pre), dh)
    return dh_new, dwg, dwu, dwd, dg_pre[0], dg_post[0]


def _lane_vec(v, at):
    return jnp.pad(v, (at, BLK - at - v.shape[0])).reshape(1, BLK)


def _ab_fwd(h, g_pre, g_post, w, tabs):
    u, proj = norm_proj(h, _row(g_pre), w["ab_in"], AB_IN_P // 3)
    ret, sall_r = ret_fwd(proj, *tabs)
    alog, dtb = _lane_vec(w["a_log"], N_HEAD_AB), _lane_vec(w["dt_bias"], N_HEAD_AB)
    cq, ck, cv, q, k, v, gates = gdn_prep_fwd(proj, w["conv"], alog, dtb)
    gdn, sall_g = gdn_chunk_fwd(q, k, v, gates, proj, _row(w["out_norm"]))
    mixed = jnp.concatenate([ret, gdn], axis=1)
    y, h_new = proj_norm_res(mixed[None], w["ab_out"][None], h, _row(g_post), 1.0)
    return h_new, (h, u, proj, sall_r, (cq, ck, cv, q, k, v, gates), sall_g, mixed, y, alog, dtb)


def _ab_bwd(dh, saved, g_pre, g_post, w, tabs):
    h, u, proj, sall_r, (cq, ck, cv, q, k, v, gates), sall_g, mixed, y, alog, dtb = saved
    dy, dg_post = post_norm_bwd(y, _row(g_post), dh, 1.0)
    dmix = mm_nt(dy, w["ab_out"])
    dw_out = mm_tn(mixed[None], dy[None], D)[0]
    drq, drk, drv, drg = ret_bwd(proj, *tabs, sall_r, dmix)
    onorm = _row(w["out_norm"])
    dq, dk, dv, dz, dgates, don = gdn_chunk_bwd(q, k, v, gates, proj, onorm, sall_g, dmix)
    dcq, dck, dcv, dgb, dal, ddt = gdn_prep_bwd(cq, ck, cv, proj, alog, dtb, dq, dk, dv, dgates)
    dxq, dxk, dxv, dconv = gdn_conv_bwd(dcq, dck, dcv, proj, w["conv"])
    dproj = jnp.concatenate([t.astype(bf16) for t in (drq, drk, drv, drg, dxq, dxk, dxv, dz, dgb)], axis=1)
    dw_in = mm_tn(u[None], dproj[None], AB_IN_P // 3)[0]
    dh_new, dg_pre = mm_nt_norm_bwd([(dproj[None], w["ab_in"][None])], h, _row(g_pre), dh, ksplit=3)
    small = dict(a_log=dal[0, N_HEAD_AB:2 * N_HEAD_AB], dt_bias=ddt[0, N_HEAD_AB:2 * N_HEAD_AB], out_norm=don[0], conv=dconv[0:GDN_K])
    return dh_new, dw_in, dw_out, dg_pre[0], dg_post[0], small


def _cd_fwd(h, g_pre, g_post, w):
    u, proj = norm_proj(h, _row(g_pre), w["cd_in"], CD_IN // 3)
    swa = swa_fwd(proj, w["sinks"])
    sb = sb_fwd(proj)
    mixed = jnp.concatenate([swa.astype(bf16), sb.astype(bf16)], axis=1)
    y, h_new = proj_norm_res(mixed[None], w["cd_out"][None], h, _row(g_post), 1.0)
    return h_new, (h, u, proj, sb, mixed, y)


def _cd_bwd(dh, saved, g_pre, g_post, w):
    h, u, proj, sb, mixed, y = saved
    dy, dg_post = post_norm_bwd(y, _row(g_post), dh, 1.0)
    dmix = mm_nt(dy, w["cd_out"])
    dw_out = mm_tn(mixed[None], dy[None], D)[0]
    dcq, cur, prev, meta, dsinks = swa_bwd(proj, w["sinks"], dmix)
    dsq, dsk, dsv = sb_bwd(proj, sb, dmix)
    dproj = cd_assemble(dcq, cur, prev, meta, dsq, dsk, dsv)
    dw_in = mm_tn(u[None], dproj[None], CD_IN // 3)[0]
    dh_new, dg_pre = mm_nt_norm_bwd([(dproj[None], w["cd_in"][None])], h, _row(g_pre), dh, ksplit=3)
    return dh_new, dw_in, dw_out, dg_pre[0], dg_post[0], dsinks[:, 0]


def local_step(x, target, w):
    L = x.shape[0]
    T = PAD + N_META + L
    tabs = rot_tables(T)
    h = jnp.concatenate([jnp.zeros((PAD, D), f32), w["meta"], x], axis=0)
    ng = w["norm_gains"]
    saved = []
    for i in range(2):
        g = ng[i]
        h, s1 = _ffn_fwd(h, g[0], g[1], w["wg"][i, 0], w["wu"][i, 0], w["wd"][i, 0])
        if i == 0:
            h, sm = _ab_fwd(h, g[2], g[3], w, tabs)
        else:
            h, sm = _cd_fwd(h, g[2], g[3], w)
        h, s2 = _ffn_fwd(h, g[4], g[5], w["wg"][i, 1], w["wu"][i, 1], w["wd"][i, 1])
        saved.append((s1, sm, s2))
    dh, lpart = loss_grad(h, target)
    grads = {}
    dng = [[None] * 6 for _ in range(2)]
    dwg = [[None, None], [None, None]]
    dwu = [[None, None], [None, None]]
    dwd = [[None, None], [None, None]]
    for i in (1, 0):
        g = ng[i]
        s1, sm, s2 = saved[i]
        dh, dwg[i][1], dwu[i][1], dwd[i][1], dng[i][4], dng[i][5] = _ffn_bwd(dh, s2, g[4], g[5], w["wg"][i, 1], w["wu"][i, 1], w["wd"][i, 1])
        if i == 0:
            dh, grads["ab_in"], grads["ab_out"], dng[i][2], dng[i][3], small = _ab_bwd(dh, sm, g[2], g[3], w, tabs)
            grads.update(small)
        else:
            dh, grads["cd_in"], grads["cd_out"], dng[i][2], dng[i][3], grads["sinks"] = _cd_bwd(dh, sm, g[2], g[3], w)
        dh, dwg[i][0], dwu[i][0], dwd[i][0], dng[i][0], dng[i][1] = _ffn_bwd(dh, s1, g[0], g[1], w["wg"][i, 0], w["wu"][i, 0], w["wd"][i, 0])
    grads["wg"], grads["wu"], grads["wd"] = dwg, dwu, dwd
    grads["norm_gains"] = jnp.stack([jnp.stack(r) for r in dng])
    grads["meta"] = dh[PAD:PAD + N_META]
    return lpart[0, 0], dh[PAD + N_META:], grads


def _r16(n, mult=16):
    return -(-n // mult) * mult


def _big_layout(F):
    halves = (("wg", "wu"), ("wd", "ab_in", "ab_out", "cd_in", "cd_out"))
    rows = dict(wg=4 * F, wu=4 * F, wd=4 * F, ab_in=AB_IN // 4, ab_out=D // 4, cd_in=CD_IN // 4, cd_out=D // 4)
    offs, used = {}, []
    for hf, names in enumerate(halves):
        o = 0
        for n in names:
            offs[n] = (hf, o, rows[n])
            o += _r16(rows[n])
        used.append(o)
    return offs, _r16(max(used), SUM_ROWS), halves


def _cat_rows(parts, total, mult=16):
    out = []
    for p in parts:
        pad = _r16(p.shape[-2], mult) - p.shape[-2]
        out.append(jnp.pad(p, [(0, 0)] * (p.ndim - 2) + [(0, pad), (0, 0)]) if pad else p)
    used = sum(o.shape[-2] for o in out)
    if total > used:
        out.append(jnp.zeros(out[0].shape[:-2] + (total - used, out[0].shape[-1]), out[0].dtype))
    return jnp.concatenate(out, axis=-2)


SMALL_ROWS = 72
REPL_ROWS = 8


def _small_rows(meta, ng, conv):
    lead = meta.shape[:-2]
    return _cat_rows([meta.reshape(lead + (32, BLK)), ng.reshape(lead + (24, BLK)), conv.reshape(lead + (12, BLK))], SMALL_ROWS, 8)


def _small_unrows(buf):
    lead = buf.shape[:-2]
    return buf[..., 0:32, :].reshape(lead + (N_META, D // 4)), buf[..., 32:56, :].reshape(lead + (2, 6, D // 4)), \
        buf[..., 56:68, :].reshape(lead + (GDN_K, 3 * W_AB // 4))


def _shard_cols(a, axis):
    shp = a.shape
    a = a.reshape(shp[:axis] + (4, shp[axis] // 4) + shp[axis + 1:])
    return jnp.moveaxis(a, axis, 0)


def _unshard_cols(a, axis):
    a = jnp.moveaxis(a, 0, axis)
    shp = a.shape
    return a.reshape(shp[:axis] + (4 * shp[axis + 1],) + shp[axis + 2:])


def kernel(x, meta_tokens, norm_gains, ffn_w_gate, ffn_w_up, ffn_w_down, ab_w_in, ab_conv_w, ab_a_log, ab_dt_bias, ab_out_norm, ab_w_out, cd_w_in, cd_sinks, cd_w_out, loss_target, m_meta_tokens, m_norm_gains, m_ffn_w_gate, m_ffn_w_up, m_ffn_w_down, m_ab_w_in, m_ab_conv_w, m_ab_a_log, m_ab_dt_bias, m_ab_out_norm, m_ab_w_out, m_cd_w_in, m_cd_sinks, m_cd_w_out, v_meta_tokens, v_norm_gains, v_ffn_w_gate, v_ffn_w_up, v_ffn_w_down, v_ab_w_in, v_ab_conv_w, v_ab_a_log, v_ab_dt_bias, v_ab_out_norm, v_ab_w_out, v_cd_w_in, v_cd_sinks, v_cd_w_out):
    F = ffn_w_gate.shape[-1]
    offs, H, halves = _big_layout(F)
    shard = dict(wg=ffn_w_gate, wu=ffn_w_up, wd=ffn_w_down, ab_in=ab_w_in, ab_out=ab_w_out, cd_in=cd_w_in, cd_out=cd_w_out)

    wbuf = jnp.stack([_cat_rows([shard[n].reshape(-1, D).astype(bf16) for n in names], H) for names in halves])
    sbuf = _small_rows(meta_tokens, norm_gains, ab_conv_w[0])
    gw, gs = gather_weights(wbuf, sbuf)

    def part(n):
        hf, o, r = offs[n]
        return gw[:, hf, o:o + r]

    meta_s, ng_s, conv_s = _small_unrows(gs)
    w = dict(
        wg=jnp.transpose(part("wg").reshape(4, 2, 2, D, F), (1, 2, 0, 3, 4)),
        wu=jnp.transpose(part("wu").reshape(4, 2, 2, D, F), (1, 2, 0, 3, 4)),
        wd=jnp.transpose(part("wd").reshape(4, 2, 2, F, D), (1, 2, 0, 3, 4)),
        ab_in=jnp.pad(_unshard_cols(part("ab_in").reshape(4, D, AB_IN // 4), 1), ((0, 0), (0, AB_IN_P - AB_IN))),
        ab_out=part("ab_out").reshape(D, D),
        cd_in=_unshard_cols(part("cd_in").reshape(4, D, CD_IN // 4), 1),
        cd_out=part("cd_out").reshape(D, D),
        meta=_unshard_cols(meta_s, 1), norm_gains=_unshard_cols(ng_s, 2), conv=_unshard_cols(conv_s, 1),
        a_log=ab_a_log[0], dt_bias=ab_dt_bias[0], out_norm=ab_out_norm[0], sinks=cd_sinks[0],
    )

    loss_local, dx, g = local_step(x[0], loss_target[0], w)

    def stack22(t):
        return jnp.stack([jnp.stack(r) for r in t])

    gparts = dict(
        wg=jnp.transpose(stack22(g["wg"]), (2, 0, 1, 3, 4)).reshape(4, 4 * F, D),
        wu=jnp.transpose(stack22(g["wu"]), (2, 0, 1, 3, 4)).reshape(4, 4 * F, D),
        wd=jnp.transpose(stack22(g["wd"]), (2, 0, 1, 3, 4)).reshape(4, 4 * F, D),
        ab_in=_shard_cols(g["ab_in"][:, :AB_IN], 1).reshape(4, AB_IN // 4, D),
        ab_out=g["ab_out"].reshape(4, D // 4, D),
        cd_in=_shard_cols(g["cd_in"], 1).reshape(4, CD_IN // 4, D),
        cd_out=g["cd_out"].reshape(4, D // 4, D),
    )
    gbuf = jnp.stack([_cat_rows([gparts[n].astype(bf16) for n in names], H) for names in halves], axis=1)
    core = lax.axis_index("c").astype(jnp.int32)
    chip = (2 * lax.axis_index("x") + lax.axis_index("y")).astype(jnp.int32)
    got = pair_exchange(gbuf)
    hsum = pair_sum(gbuf, got, core.reshape(1))
    parts = chip_exchange(hsum)
    rsum = chip_sum(hsum, parts, chip.reshape(1))
    other = pair_gather(rsum)
    ghalf = (jnp.where(core == 0, rsum, other), jnp.where(core == 0, other, rsum))

    onehot = np.eye(REPL_ROWS, dtype=np.float32)
    repl = sum(onehot[k][:, None] * _lane_vec(g[n], 0) for k, n in enumerate(("a_log", "dt_bias", "out_norm", "sinks")))
    ssrc = jnp.concatenate([_small_rows(_shard_cols(g["meta"], 1), _shard_cols(g["norm_gains"], 2), _shard_cols(g["conv"], 1)),
                            jnp.broadcast_to(repl, (4, REPL_ROWS, BLK))], axis=1)
    sred = small_reduce(ssrc)
    g_meta, g_ng, g_conv = _small_unrows(sred[:SMALL_ROWS])

    def gpart(n, shape):
        hf, o, r = offs[n]
        return ghalf[hf][o:o + r].reshape(shape)

    grad = dict(
        meta_tokens=g_meta, norm_gains=g_ng,
        ffn_w_gate=gpart("wg", ffn_w_gate.shape), ffn_w_up=gpart("wu", ffn_w_up.shape), ffn_w_down=gpart("wd", ffn_w_down.shape),
        ab_w_in=gpart("ab_in", ab_w_in.shape), ab_conv_w=g_conv[None],
        ab_a_log=sred[SMALL_ROWS:SMALL_ROWS + 1, 0:N_HEAD_AB], ab_dt_bias=sred[SMALL_ROWS + 1:SMALL_ROWS + 2, 0:N_HEAD_AB],
        ab_out_norm=sred[SMALL_ROWS + 2:SMALL_ROWS + 3, :], ab_w_out=gpart("ab_out", ab_w_out.shape),
        cd_w_in=gpart("cd_in", cd_w_in.shape), cd_sinks=sred[SMALL_ROWS + 3:SMALL_ROWS + 4, 0:2 * SWA_G], cd_w_out=gpart("cd_out", cd_w_out.shape),
    )

    weights = dict(meta_tokens=meta_tokens, norm_gains=norm_gains, ffn_w_gate=ffn_w_gate, ffn_w_up=ffn_w_up, ffn_w_down=ffn_w_down,
                   ab_w_in=ab_w_in, ab_conv_w=ab_conv_w, ab_a_log=ab_a_log, ab_dt_bias=ab_dt_bias, ab_out_norm=ab_out_norm,
                   ab_w_out=ab_w_out, cd_w_in=cd_w_in, cd_sinks=cd_sinks, cd_w_out=cd_w_out)
    ms = dict(meta_tokens=m_meta_tokens, norm_gains=m_norm_gains, ffn_w_gate=m_ffn_w_gate, ffn_w_up=m_ffn_w_up, ffn_w_down=m_ffn_w_down,
              ab_w_in=m_ab_w_in, ab_conv_w=m_ab_conv_w, ab_a_log=m_ab_a_log, ab_dt_bias=m_ab_dt_bias, ab_out_norm=m_ab_out_norm,
              ab_w_out=m_ab_w_out, cd_w_in=m_cd_w_in, cd_sinks=m_cd_sinks, cd_w_out=m_cd_w_out)
    vs = dict(meta_tokens=v_meta_tokens, norm_gains=v_norm_gains, ffn_w_gate=v_ffn_w_gate, ffn_w_up=v_ffn_w_up, ffn_w_down=v_ffn_w_down,
              ab_w_in=v_ab_w_in, ab_conv_w=v_ab_conv_w, ab_a_log=v_ab_a_log, ab_dt_bias=v_ab_dt_bias, ab_out_norm=v_ab_out_norm,
              ab_w_out=v_ab_w_out, cd_w_in=v_cd_w_in, cd_sinks=v_cd_sinks, cd_w_out=v_cd_w_out)
    order = list(weights)
    delta, new_m, new_v = {}, {}, {}
    for n in order:
        shp = weights[n].shape
        two = (-1, shp[-1])
        d, mn, vn = adamw(weights[n].reshape(two), grad[n].reshape(two), ms[n].reshape(two), vs[n].reshape(two))
        delta[n], new_m[n], new_v[n] = d.reshape(shp), mn.reshape(shp), vn.reshape(shp)

    loss = lax.psum(loss_local, ("x", "y", "c"))
    return (loss, dx[None], *[grad[n].reshape(weights[n].shape) for n in order], *[delta[n] for n in order],
            *[new_m[n] for n in order], *[new_v[n] for n in order])
```

```python
import functools

import numpy as np
import jax
import jax.numpy as jnp
from jax import lax
from jax.experimental import pallas as pl
from jax.experimental.pallas import tpu as pltpu

f32 = jnp.float32
bf16 = jnp.bfloat16

EPS = 1e-6
D = 1024
N_META = 16
PAD = 112
BLK = 128
GDN_C = 64
N_HEAD_AB = 4
DH_AB = 128
AB_IN = 4104
AB_IN_P = 4224
CD_IN = 2304
ADAM_LR, ADAM_B1, ADAM_B2, ADAM_EPS, ADAM_WD, ADAM_STEP = 0.001, 0.9, 0.999, 1e-08, 0.01, 10
VMEM_LIMIT = 56 * 1024 * 1024
NEG = -1e30
SB_EXIT = -104.0

_NT = (((1,), (1,)), ((), ()))
_TN = (((0,), (0,)), ((), ()))


def _cparams(**kw):
    return pltpu.CompilerParams(vmem_limit_bytes=VMEM_LIMIT, **kw)


def _dot(a, b):
    return jnp.dot(a.astype(bf16), b.astype(bf16), preferred_element_type=f32)


def _dot_nt(a, b):
    return lax.dot_general(a.astype(bf16), b.astype(bf16), _NT, preferred_element_type=f32)


def _dot_tn(a, b):
    return lax.dot_general(a.astype(bf16), b.astype(bf16), _TN, preferred_element_type=f32)


def _dot2(a, b01):
    hi = a.astype(bf16)
    lo = (a - hi.astype(f32)).astype(bf16)
    return jnp.dot(hi, b01, preferred_element_type=f32) + jnp.dot(lo, b01, preferred_element_type=f32)


def _row_tile(t):
    for c in (640, 512, 256, 128):
        if t % c == 0:
            return c
    raise ValueError(t)


def _sigmoid(x):
    return 1.0 / (1.0 + jnp.exp(-x))


def _silu(x):
    return x * _sigmoid(x)


def _softplus(x):
    return jnp.maximum(x, 0.0) + jnp.log(1.0 + jnp.exp(-jnp.abs(x)))


def _rms(x, g):
    r = lax.rsqrt(jnp.mean(x * x, axis=-1, keepdims=True) + EPS)
    return x * r * g


def _rms_bwd(x, g, dy):
    r = lax.rsqrt(jnp.mean(x * x, axis=-1, keepdims=True) + EPS)
    xh = x * r
    dg = jnp.sum(dy * xh, axis=0, keepdims=True)
    dxh = dy * g
    dx = r * (dxh - xh * jnp.mean(dxh * xh, axis=-1, keepdims=True))
    return dx, dg


def _zero_pad_rows(v, i, tr):
    rows = i * tr + lax.broadcasted_iota(jnp.int32, (tr, 1), 0)
    return jnp.where(rows >= PAD, v, 0.0)


def _acc8(ref, row, first):
    @pl.when(first)
    def _():
        ref[...] = jnp.zeros_like(ref)
    ref[...] += jnp.broadcast_to(row, ref.shape)


def ffn_up(h, g, wg, wu):
    T, Dm = h.shape
    S, _, F = wg.shape
    tm = _row_tile(T)

    def body(h_ref, g_ref, wg_ref, wu_ref, u_ref, G_ref, U_ref, a_ref):
        @pl.when(pl.program_id(1) == 0)
        def _():
            u_ref[...] = _rms(h_ref[...], g_ref[...]).astype(u_ref.dtype)
        u = u_ref[...]
        G = _dot(u, wg_ref[...])
        U = _dot(u, wu_ref[...])
        G_ref[...] = G.astype(G_ref.dtype)
        U_ref[...] = U.astype(U_ref.dtype)
        a_ref[...] = (_silu(G) * U).astype(a_ref.dtype)

    act = jax.ShapeDtypeStruct((S, T, F), bf16)
    wspec = pl.BlockSpec((None, Dm, F), lambda i, s: (s, 0, 0))
    aspec = pl.BlockSpec((None, tm, F), lambda i, s: (s, i, 0))
    return pl.pallas_call(
        body, name="ffn_up", grid=(T // tm, S),
        in_specs=[pl.BlockSpec((tm, Dm), lambda i, s: (i, 0)), pl.BlockSpec((1, Dm), lambda i, s: (0, 0)), wspec, wspec],
        out_specs=[pl.BlockSpec((tm, Dm), lambda i, s: (i, 0)), aspec, aspec, aspec],
        out_shape=[jax.ShapeDtypeStruct((T, Dm), bf16), act, act, act],
        compiler_params=_cparams(dimension_semantics=("arbitrary", "arbitrary")),
    )(h, g, wg, wu)


def norm_proj(h, g, w, tn):
    T, Dm = h.shape
    N = w.shape[1]
    tm = _row_tile(T)

    def body(h_ref, g_ref, w_ref, u_ref, p_ref):
        @pl.when(pl.program_id(1) == 0)
        def _():
            u_ref[...] = _rms(h_ref[...], g_ref[...]).astype(u_ref.dtype)
        p_ref[...] = _dot(u_ref[...], w_ref[...])

    return pl.pallas_call(
        body, name="norm_proj", grid=(T // tm, N // tn),
        in_specs=[pl.BlockSpec((tm, Dm), lambda i, j: (i, 0)), pl.BlockSpec((1, Dm), lambda i, j: (0, 0)),
                  pl.BlockSpec((Dm, tn), lambda i, j: (0, j))],
        out_specs=[pl.BlockSpec((tm, Dm), lambda i, j: (i, 0)), pl.BlockSpec((tm, tn), lambda i, j: (i, j))],
        out_shape=[jax.ShapeDtypeStruct((T, Dm), bf16), jax.ShapeDtypeStruct((T, N), f32)],
        compiler_params=_cparams(dimension_semantics=("arbitrary", "arbitrary")),
    )(h, g, w)


def proj_norm_res(a, w, h, g, coef):
    S, T, F = a.shape
    Dm = w.shape[2]
    tm = _row_tile(T)

    def body(a_ref, w_ref, h_ref, g_ref, y_ref, o_ref, acc):
        s = pl.program_id(1)

        @pl.when(s == 0)
        def _():
            acc[...] = jnp.zeros_like(acc)
        acc[...] += _dot(a_ref[...], w_ref[...])

        @pl.when(s == S - 1)
        def _():
            y = acc[...]
            y_ref[...] = y
            o_ref[...] = h_ref[...] + coef * _rms(y, g_ref[...])

    row = pl.BlockSpec((tm, Dm), lambda i, s: (i, 0))
    return pl.pallas_call(
        body, name="proj_norm_res", grid=(T // tm, S),
        in_specs=[pl.BlockSpec((None, tm, F), lambda i, s: (s, i, 0)), pl.BlockSpec((None, F, Dm), lambda i, s: (s, 0, 0)),
                  row, pl.BlockSpec((1, Dm), lambda i, s: (0, 0))],
        out_specs=[row, row],
        out_shape=[jax.ShapeDtypeStruct((T, Dm), f32), jax.ShapeDtypeStruct((T, Dm), f32)],
        scratch_shapes=[pltpu.VMEM((tm, Dm), f32)],
        compiler_params=_cparams(dimension_semantics=("arbitrary", "arbitrary")),
    )(a, w, h, g)


def post_norm_bwd(y, g, dz, coef):
    T, Dm = y.shape
    tm = _row_tile(T)

    def body(y_ref, g_ref, dz_ref, dy_ref, dg_ref):
        dy, dg = _rms_bwd(y_ref[...], g_ref[...], coef * dz_ref[...])
        dy_ref[...] = _zero_pad_rows(dy, pl.program_id(0), tm).astype(dy_ref.dtype)
        _acc8(dg_ref, dg, pl.program_id(0) == 0)

    row = pl.BlockSpec((tm, Dm), lambda i: (i, 0))
    return pl.pallas_call(
        body, name="post_norm_bwd", grid=(T // tm,),
        in_specs=[row, pl.BlockSpec((1, Dm), lambda i: (0, 0)), row],
        out_specs=[row, pl.BlockSpec((8, Dm), lambda i: (0, 0))],
        out_shape=[jax.ShapeDtypeStruct((T, Dm), bf16), jax.ShapeDtypeStruct((8, Dm), f32)],
        compiler_params=_cparams(dimension_semantics=("arbitrary",)),
    )(y, g, dz)


def ffn_bwd_act(dy, wd, G, U):
    T, Dm = dy.shape
    S, F, _ = wd.shape
    tm = _row_tile(T)

    def body(dy_ref, w_ref, G_ref, U_ref, dG_ref, dU_ref):
        da = _dot_nt(dy_ref[...], w_ref[...])
        Gv = G_ref[...].astype(f32)
        Uv = U_ref[...].astype(f32)
        sg = _sigmoid(Gv)
        dU_ref[...] = (da * Gv * sg).astype(dU_ref.dtype)
        dG_ref[...] = (da * Uv * sg * (1.0 + Gv * (1.0 - sg))).astype(dG_ref.dtype)

    aspec = pl.BlockSpec((None, tm, F), lambda i, s: (s, i, 0))
    act = jax.ShapeDtypeStruct((S, T, F), bf16)
    return pl.pallas_call(
        body, name="ffn_bwd_act", grid=(T // tm, S),
        in_specs=[pl.BlockSpec((tm, Dm), lambda i, s: (i, 0)), pl.BlockSpec((None, F, Dm), lambda i, s: (s, 0, 0)), aspec, aspec],
        out_specs=[aspec, aspec], out_shape=[act, act],
        compiler_params=_cparams(dimension_semantics=("arbitrary", "arbitrary")),
    )(dy, wd, G, U)


def mm_nt_norm_bwd(pairs, h, g, dres, ksplit=1):
    S, T, K = pairs[0][0].shape
    assert S == 1 or ksplit == 1
    steps = S * ksplit
    tk = K // ksplit
    Dm = h.shape[1]
    tm = _row_tile(T)
    n = len(pairs)

    def body(*refs):
        ab = refs[:2 * n]
        h_ref, g_ref, dres_ref, dh_ref, dg_ref, acc = refs[2 * n:]
        i, s = pl.program_id(0), pl.program_id(1)

        @pl.when(s == 0)
        def _():
            acc[...] = jnp.zeros_like(acc)
        for p in range(n):
            acc[...] += _dot_nt(ab[2 * p][...], ab[2 * p + 1][...])

        @pl.when(s == steps - 1)
        def _():
            dx, dg = _rms_bwd(h_ref[...], g_ref[...], acc[...])
            dh_ref[...] = _zero_pad_rows(dres_ref[...] + dx, i, tm)
            _acc8(dg_ref, dg, i == 0)

    row = pl.BlockSpec((tm, Dm), lambda i, s: (i, 0))
    if S > 1:
        amap, bmap = (lambda i, s: (s, i, 0)), (lambda i, s: (s, 0, 0))
    else:
        amap, bmap = (lambda i, s: (0, i, s)), (lambda i, s: (0, 0, s))
    in_specs, args = [], []
    for a, b in pairs:
        in_specs += [pl.BlockSpec((None, tm, tk), amap), pl.BlockSpec((None, Dm, tk), bmap)]
        args += [a, b]
    return pl.pallas_call(
        body, name="mm_nt_norm_bwd", grid=(T // tm, steps),
        in_specs=in_specs + [row, pl.BlockSpec((1, Dm), lambda i, s: (0, 0)), row],
        out_specs=[row, pl.BlockSpec((8, Dm), lambda i, s: (0, 0))],
        out_shape=[jax.ShapeDtypeStruct((T, Dm), f32), jax.ShapeDtypeStruct((8, Dm), f32)],
        scratch_shapes=[pltpu.VMEM((tm, Dm), f32)],
        compiler_params=_cparams(dimension_semantics=("arbitrary", "arbitrary")),
    )(*args, h, g, dres)


def mm_nt(a, b):
    T, K = a.shape
    N = b.shape[0]
    tm = _row_tile(T)

    def body(a_ref, b_ref, o_ref):
        o_ref[...] = _dot_nt(a_ref[...], b_ref[...])

    return pl.pallas_call(
        body, name="mm_nt", grid=(T // tm,),
        in_specs=[pl.BlockSpec((tm, K), lambda i: (i, 0)), pl.BlockSpec((N, K), lambda i: (0, 0))],
        out_specs=pl.BlockSpec((tm, N), lambda i: (i, 0)),
        out_shape=jax.ShapeDtypeStruct((T, N), f32),
        compiler_params=_cparams(dimension_semantics=("arbitrary",)),
    )(a, b)


def mm_tn(a, b, tn):
    Sa, T, M = a.shape
    Sb, _, N = b.shape
    S = max(Sa, Sb)
    tk = _row_tile(T)
    nk = T // tk

    def body(a_ref, b_ref, o_ref, acc):
        k = pl.program_id(2)

        @pl.when(k == 0)
        def _():
            acc[...] = jnp.zeros_like(acc)
        acc[...] += _dot_tn(a_ref[...], b_ref[...])

        @pl.when(k == nk - 1)
        def _():
            o_ref[...] = acc[...].astype(o_ref.dtype)

    return pl.pallas_call(
        body, name="mm_tn", grid=(S, N // tn, nk),
        in_specs=[pl.BlockSpec((None, tk, M), (lambda s, j, k: (s, k, 0)) if Sa > 1 else (lambda s, j, k: (0, k, 0))),
                  pl.BlockSpec((None, tk, tn), (lambda s, j, k: (s, k, j)) if Sb > 1 else (lambda s, j, k: (0, k, j)))],
        out_specs=pl.BlockSpec((None, M, tn), lambda s, j, k: (s, 0, j)),
        out_shape=jax.ShapeDtypeStruct((S, M, N), bf16),
        scratch_shapes=[pltpu.VMEM((M, tn), f32)],
        compiler_params=_cparams(dimension_semantics=("arbitrary", "arbitrary", "arbitrary")),
    )(a, b)


def adamw(w, g, m, v):
    R, C = w.shape
    tr = 512 if R % 512 == 0 else (256 if R % 256 == 0 else R)
    c1 = np.float32(1.0 - ADAM_B1 ** ADAM_STEP)
    c2 = np.float32(1.0 - ADAM_B2 ** ADAM_STEP)

    def body(w_ref, g_ref, m_ref, v_ref, d_ref, mo_ref, vo_ref):
        gv = g_ref[...]
        mn = ADAM_B1 * m_ref[...] + (1.0 - ADAM_B1) * gv
        vn = ADAM_B2 * v_ref[...] + (1.0 - ADAM_B2) * (gv * gv)
        mo_ref[...] = mn
        vo_ref[...] = vn
        d_ref[...] = -ADAM_LR * ((mn / c1) / (jnp.sqrt(vn / c2) + ADAM_EPS) + ADAM_WD * w_ref[...])

    spec = pl.BlockSpec((tr, C), lambda i: (i, 0))
    sh = jax.ShapeDtypeStruct((R, C), f32)
    return pl.pallas_call(
        body, name="adamw", grid=(R // tr,), in_specs=[spec] * 4, out_specs=[spec] * 3, out_shape=[sh] * 3,
        compiler_params=_cparams(dimension_semantics=("arbitrary",)),
    )(w, g, m, v)


_RET_LOG_GAMMA = [float(v) for v in np.log1p(-np.exp2(-5.0 - np.arange(N_HEAD_AB, dtype=np.float32))).astype(np.float32)]


def rot_tables(T):
    pos = jnp.arange(T, dtype=f32) - float(PAD)
    inv_freq = 1.0 / (10000.0 ** jnp.linspace(0.0, 1.0, DH_AB // 2, dtype=f32))
    ang = pos[:, None] * inv_freq[None, :]
    cos, sin = jnp.cos(ang), jnp.sin(ang)
    return jnp.repeat(cos, 2, axis=1), jnp.stack([-sin, sin], axis=-1).reshape(T, DH_AB)


def _swap_pairs(x):
    lane = lax.broadcasted_iota(jnp.int32, x.shape, 1)
    return jnp.where(lane % 2 == 0, pltpu.roll(x, x.shape[1] - 1, 1), pltpu.roll(x, 1, 1))


def _rot(x, c, s):
    return x * c + _swap_pairs(x) * s


def _rot_bwd(d, c, s):
    return d * c + _swap_pairs(d * s)


def _ret_mats(lg):
    i = lax.broadcasted_iota(jnp.int32, (BLK, BLK), 0).astype(f32)
    j = lax.broadcasted_iota(jnp.int32, (BLK, BLK), 1).astype(f32)
    diff = i - j
    decay = jnp.where(diff >= 0, jnp.exp(jnp.maximum(diff, 0.0) * lg), 0.0)
    xi = jnp.exp((i + 1.0) * lg)
    zeta = jnp.exp((BLK - 1.0 - i) * lg)
    return decay, xi, zeta, float(np.exp(np.float32(BLK * lg)))


def _ret_head(q_ref, k_ref, v_ref, cos, sin, h, Sp):
    sl = slice(DH_AB * h, DH_AB * (h + 1))
    decay, xi, zeta, gc = _ret_mats(_RET_LOG_GAMMA[h])
    q = _rot(q_ref[:, sl], cos, sin)
    k = _rot(k_ref[:, sl], cos, sin) * DH_AB ** -0.5
    v = v_ref[:, sl]
    P = _dot_nt(q, k) * decay
    ret = _dot(P, v) + _dot(q * xi, Sp)
    return sl, q, k, v, P, ret, decay, xi, zeta, gc


def _proj_spec(rows, width, col, rev=None):
    if rev is None:
        return pl.BlockSpec((rows, width), lambda n: (n, col))
    return pl.BlockSpec((rows, width), lambda n: (rev - n, col))


def ret_fwd(proj, cos, sin):
    T = proj.shape[0]
    NC = T // BLK
    W = N_HEAD_AB * DH_AB

    def body(q_ref, k_ref, v_ref, g_ref, cos_ref, sin_ref, o_ref, sall_ref, S):
        @pl.when(pl.program_id(0) == 0)
        def _():
            S[...] = jnp.zeros_like(S)
        cos_v, sin_v = cos_ref[...], sin_ref[...]
        for h in range(N_HEAD_AB):
            Sp = S[h]
            sall_ref[0, h] = Sp
            sl, q, k, v, P, ret, decay, xi, zeta, gc = _ret_head(q_ref, k_ref, v_ref, cos_v, sin_v, h, Sp)
            S[h] = Sp * gc + _dot_tn(k * zeta, v)
            mu = jnp.mean(ret, axis=-1, keepdims=True)
            cen = ret - mu
            y = cen * lax.rsqrt(jnp.mean(cen * cen, axis=-1, keepdims=True) + EPS)
            o_ref[:, sl] = (y * _silu(g_ref[:, sl])).astype(o_ref.dtype)

    tab = pl.BlockSpec((BLK, DH_AB), lambda n: (n, 0))
    return pl.pallas_call(
        body, name="ret_fwd", grid=(NC,),
        in_specs=[_proj_spec(BLK, W, 0), _proj_spec(BLK, W, 1), _proj_spec(BLK, W, 2), _proj_spec(BLK, W, 3), tab, tab],
        out_specs=[pl.BlockSpec((BLK, W), lambda n: (n, 0)), pl.BlockSpec((1, N_HEAD_AB, DH_AB, DH_AB), lambda n: (n, 0, 0, 0))],
        out_shape=[jax.ShapeDtypeStruct((T, W), bf16), jax.ShapeDtypeStruct((NC, N_HEAD_AB, DH_AB, DH_AB), f32)],
        scratch_shapes=[pltpu.VMEM((N_HEAD_AB, DH_AB, DH_AB), f32)],
        compiler_params=_cparams(dimension_semantics=("arbitrary",)),
    )(proj, proj, proj, proj, cos, sin)


def ret_bwd(proj, cos, sin, sall, dmix):
    T = proj.shape[0]
    NC = T // BLK
    W = N_HEAD_AB * DH_AB
    L = NC - 1

    def body(q_ref, k_ref, v_ref, g_ref, cos_ref, sin_ref, sall_ref, do_ref, dq_ref, dk_ref, dv_ref, dg_ref, dS):
        @pl.when(pl.program_id(0) == 0)
        def _():
            dS[...] = jnp.zeros_like(dS)
        cos_v, sin_v = cos_ref[...], sin_ref[...]
        for h in range(N_HEAD_AB):
            Sp = sall_ref[0, h]
            sl, q, k, v, P, ret, decay, xi, zeta, gc = _ret_head(q_ref, k_ref, v_ref, cos_v, sin_v, h, Sp)
            mu = jnp.mean(ret, axis=-1, keepdims=True)
            cen = ret - mu
            r = lax.rsqrt(jnp.mean(cen * cen, axis=-1, keepdims=True) + EPS)
            y = cen * r
            gate = g_ref[:, sl]
            sg = _sigmoid(gate)
            dout = do_ref[:, sl]
            dg_ref[:, sl] = dout * y * (sg * (1.0 + gate * (1.0 - sg)))
            dy = dout * (gate * sg)
            dO = r * (dy - jnp.mean(dy, axis=-1, keepdims=True) - y * jnp.mean(dy * y, axis=-1, keepdims=True))
            dSn = dS[h]
            dv_ref[:, sl] = _dot_tn(P, dO) + _dot(k * zeta, dSn)
            dP = _dot_nt(dO, v) * decay
            dq = _dot(dP, k) + _dot_nt(dO, Sp) * xi
            dk = _dot_tn(dP, q) + _dot_nt(v, dSn) * zeta
            dS[h] = dSn * gc + _dot_tn(q * xi, dO)
            dq_ref[:, sl] = _rot_bwd(dq, cos_v, sin_v)
            dk_ref[:, sl] = _rot_bwd(dk * DH_AB ** -0.5, cos_v, sin_v)

    tab = pl.BlockSpec((BLK, DH_AB), lambda n: (L - n, 0))
    out = pl.BlockSpec((BLK, W), lambda n: (L - n, 0))
    sh = jax.ShapeDtypeStruct((T, W), f32)
    return pl.pallas_call(
        body, name="ret_bwd", grid=(NC,),
        in_specs=[_proj_spec(BLK, W, 0, L), _proj_spec(BLK, W, 1, L), _proj_spec(BLK, W, 2, L), _proj_spec(BLK, W, 3, L), tab, tab,
                  pl.BlockSpec((1, N_HEAD_AB, DH_AB, DH_AB), lambda n: (L - n, 0, 0, 0)), _proj_spec(BLK, W, 0, L)],
        out_specs=[out] * 4, out_shape=[sh] * 4,
        scratch_shapes=[pltpu.VMEM((N_HEAD_AB, DH_AB, DH_AB), f32)],
        compiler_params=_cparams(dimension_semantics=("arbitrary",)),
    )(proj, proj, proj, proj, cos, sin, sall, dmix)


HALO = 8
GDN_K = 4
W_AB = N_HEAD_AB * DH_AB


def _gdn_rowwise(cq, ck, cv, gblk, alog, dtb, rmask):
    def l2n(x):
        return [x[:, DH_AB * h:DH_AB * (h + 1)] for h in range(N_HEAD_AB)]

    def norm(x):
        return x * lax.rsqrt(jnp.sum(x * x, axis=-1, keepdims=True) + EPS)

    qs = [norm(x) for x in l2n(_silu(cq))]
    ks = [norm(x) for x in l2n(_silu(ck))]
    lane = lax.broadcasted_iota(jnp.int32, gblk.shape, 1)
    beta = _sigmoid(gblk)
    g = -jnp.exp(alog) * _softplus(gblk + dtb)
    gates = jnp.where(lane < N_HEAD_AB, beta, jnp.where(lane < 2 * N_HEAD_AB, g, 0.0)) * rmask
    return qs, ks, _silu(cv), gates


def _row_mask(i, tr):
    rows = i * tr + lax.broadcasted_iota(jnp.int32, (tr, 1), 0)
    return (rows >= PAD).astype(f32)


def _conv_specs(tr, cols, nt, nxt=False):
    tiles = [pl.BlockSpec((tr, W_AB), functools.partial(lambda i, c: (i, c), c=c)) for c in cols]
    r = tr // HALO
    if nxt:
        halos = [pl.BlockSpec((HALO, W_AB), functools.partial(lambda i, c: (jnp.minimum((i + 1) * r, nt * r - 1), c), c=c)) for c in cols]
    else:
        halos = [pl.BlockSpec((HALO, W_AB), functools.partial(lambda i, c: (jnp.maximum(i * r - 1, 0), c), c=c)) for c in cols]
    return tiles, halos


def gdn_prep_fwd(proj, conv_w, alog, dtb):
    T = proj.shape[0]
    tr = BLK
    NT = T // tr

    def body(xq, xk, xv, hq, hk, hv, gb_ref, w_ref, al_ref, dt_ref, cq_o, ck_o, cv_o, q_o, k_o, v_o, gates_o, buf):
        i = pl.program_id(0)
        cs = []
        for p, (x_ref, h_ref, c_o) in enumerate(((xq, hq, cq_o), (xk, hk, ck_o), (xv, hv, cv_o))):
            buf[0:HALO, :] = jnp.where(i > 0, h_ref[...], 0.0)
            buf[HALO:, :] = x_ref[...]
            c = jnp.zeros((tr, W_AB), f32)
            for k in range(GDN_K):
                c = c + w_ref[k:k + 1, W_AB * p:W_AB * (p + 1)] * buf[pl.ds(HALO - GDN_K + 1 + k, tr), :]
            c_o[...] = c
            cs.append(c)
        qs, ks, v, gates = _gdn_rowwise(cs[0], cs[1], cs[2], gb_ref[...], al_ref[...], dt_ref[...], _row_mask(i, tr))
        for h in range(N_HEAD_AB):
            q_o[:, DH_AB * h:DH_AB * (h + 1)] = qs[h]
            k_o[:, DH_AB * h:DH_AB * (h + 1)] = ks[h]
        v_o[...] = v
        gates_o[...] = gates

    tiles, halos = _conv_specs(tr, (4, 5, 6), NT)
    vec = pl.BlockSpec((1, BLK), lambda i: (0, 0))
    wide = pl.BlockSpec((tr, W_AB), lambda i: (i, 0))
    sh = jax.ShapeDtypeStruct((T, W_AB), f32)
    return pl.pallas_call(
        body, name="gdn_prep_fwd", grid=(NT,),
        in_specs=tiles + halos + [pl.BlockSpec((tr, BLK), lambda i: (i, AB_IN_P // BLK - 1)),
                                  pl.BlockSpec((GDN_K, 3 * W_AB), lambda i: (0, 0)), vec, vec],
        out_specs=[wide] * 6 + [pl.BlockSpec((tr, BLK), lambda i: (i, 0))],
        out_shape=[sh] * 6 + [jax.ShapeDtypeStruct((T, BLK), f32)],
        scratch_shapes=[pltpu.VMEM((tr + HALO, W_AB), f32)],
        compiler_params=_cparams(dimension_semantics=("arbitrary",)),
    )(proj, proj, proj, proj, proj, proj, proj, conv_w, alog, dtb)


def gdn_prep_bwd(cq, ck, cv, proj, alog, dtb, dq, dk, dv, dgates):
    T = cq.shape[0]
    tr = BLK
    NT = T // tr

    def body(cq_r, ck_r, cv_r, gb_ref, al_ref, dt_ref, dq_r, dk_r, dv_r, dg_r, dcq_o, dck_o, dcv_o, dgb_o, dal_o, ddt_o):
        i = pl.program_id(0)
        mask = _row_mask(i, tr)
        _, vjp = jax.vjp(lambda a, b, c, d, e, f: _gdn_rowwise(a, b, c, d, e, f, mask),
                         cq_r[...], ck_r[...], cv_r[...], gb_ref[...], al_ref[...], dt_ref[...])
        heads = lambda r: [r[:, DH_AB * h:DH_AB * (h + 1)] for h in range(N_HEAD_AB)]
        dcq, dck, dcv, dgb, dal, ddt = vjp((heads(dq_r), heads(dk_r), dv_r[...], dg_r[...]))
        dcq_o[...] = dcq
        dck_o[...] = dck
        dcv_o[...] = dcv
        dgb_o[...] = dgb
        _acc8(dal_o, dal, i == 0)
        _acc8(ddt_o, ddt, i == 0)

    vec = pl.BlockSpec((1, BLK), lambda i: (0, 0))
    wide = pl.BlockSpec((tr, W_AB), lambda i: (i, 0))
    narrow = pl.BlockSpec((tr, BLK), lambda i: (i, 0))
    acc = pl.BlockSpec((8, BLK), lambda i: (0, 0))
    sh = jax.ShapeDtypeStruct((T, W_AB), f32)
    return pl.pallas_call(
        body, name="gdn_prep_bwd", grid=(NT,),
        in_specs=[wide] * 3 + [pl.BlockSpec((tr, BLK), lambda i: (i, AB_IN_P // BLK - 1)), vec, vec] + [wide] * 3 + [narrow],
        out_specs=[wide] * 3 + [narrow, acc, acc],
        out_shape=[sh] * 3 + [jax.ShapeDtypeStruct((T, BLK), f32)] + [jax.ShapeDtypeStruct((8, BLK), f32)] * 2,
        compiler_params=_cparams(dimension_semantics=("arbitrary",)),
    )(cq, ck, cv, proj, alog, dtb, dq, dk, dv, dgates)


def gdn_conv_bwd(dcq, dck, dcv, proj, conv_w):
    T = dcq.shape[0]
    tr = BLK
    NT = T // tr

    def body(dq_r, dk_r, dv_r, nq, nk, nv, xq, xk, xv, hq, hk, hv, w_ref, dxq_o, dxk_o, dxv_o, dw_o, bufd, bufx):
        i = pl.program_id(0)

        @pl.when(i == 0)
        def _():
            dw_o[...] = jnp.zeros_like(dw_o)
        parts = ((dq_r, nq, xq, hq, dxq_o), (dk_r, nk, xk, hk, dxk_o), (dv_r, nv, xv, hv, dxv_o))
        for p, (dc_r, n_r, x_r, h_r, dx_o) in enumerate(parts):
            dc = dc_r[...]
            bufd[0:tr, :] = dc
            bufd[tr:, :] = jnp.where(i < NT - 1, n_r[...], 0.0)
            bufx[0:HALO, :] = jnp.where(i > 0, h_r[...], 0.0)
            bufx[HALO:, :] = x_r[...]
            dx = jnp.zeros((tr, W_AB), f32)
            rows = []
            for k in range(GDN_K):
                dx = dx + w_ref[k:k + 1, W_AB * p:W_AB * (p + 1)] * bufd[pl.ds(GDN_K - 1 - k, tr), :]
                rows.append(jnp.sum(dc * bufx[pl.ds(HALO - GDN_K + 1 + k, tr), :], axis=0, keepdims=True))
            dx_o[...] = dx
            dw_o[:, W_AB * p:W_AB * (p + 1)] += jnp.concatenate(rows + [jnp.zeros((8 - GDN_K, W_AB), f32)], axis=0)

    wide = pl.BlockSpec((tr, W_AB), lambda i: (i, 0))
    r = tr // HALO
    nxt = pl.BlockSpec((HALO, W_AB), lambda i: (jnp.minimum((i + 1) * r, NT * r - 1), 0))
    tiles, halos = _conv_specs(tr, (4, 5, 6), NT)
    sh = jax.ShapeDtypeStruct((T, W_AB), f32)
    return pl.pallas_call(
        body, name="gdn_conv_bwd", grid=(NT,),
        in_specs=[wide] * 3 + [nxt] * 3 + tiles + halos + [pl.BlockSpec((GDN_K, 3 * W_AB), lambda i: (0, 0))],
        out_specs=[wide] * 3 + [pl.BlockSpec((8, 3 * W_AB), lambda i: (0, 0))],
        out_shape=[sh] * 3 + [jax.ShapeDtypeStruct((8, 3 * W_AB), f32)],
        scratch_shapes=[pltpu.VMEM((tr + HALO, W_AB), f32), pltpu.VMEM((tr + HALO, W_AB), f32)],
        compiler_params=_cparams(dimension_semantics=("arbitrary",)),
    )(dcq, dck, dcv, dcq, dck, dcv, proj, proj, proj, proj, proj, proj, conv_w)


def _tri_sum(x, upper):
    n = x.shape[0]
    r = lax.broadcasted_iota(jnp.int32, (n, n), 0)
    c = lax.broadcasted_iota(jnp.int32, (n, n), 1)
    tri = ((r <= c) if upper else (r >= c)).astype(bf16)
    hi = x.astype(bf16)
    lo = (x - hi.astype(f32)).astype(bf16)
    return jnp.dot(tri, hi, preferred_element_type=f32) + jnp.dot(tri, lo, preferred_element_type=f32)


@jax.custom_vjp
def _cumsum_rows(x):
    return _tri_sum(x, False)


_cumsum_rows.defvjp(lambda x: (_tri_sum(x, False), None), lambda _, g: (_tri_sum(g, True),))


@jax.custom_vjp
def _unit_lower_inv(a):
    n = a.shape[0]
    eye = (lax.broadcasted_iota(jnp.int32, (n, n), 0) == lax.broadcasted_iota(jnp.int32, (n, n), 1)).astype(f32)
    b = -a
    x = eye + b
    p = b
    for _ in range(int(np.log2(n)) - 1):
        p = _dot(p, p)
        x = x + _dot(x, p)
    return x


def _unit_lower_inv_fwd(a):
    t = _unit_lower_inv(a)
    return t, t


def _unit_lower_inv_bwd(t, dt):
    return (-_dot_nt(_dot_tn(t, dt), t),)


_unit_lower_inv.defvjp(_unit_lower_inv_fwd, _unit_lower_inv_bwd)


def _gdn_chunk(qs, ks, vs, gates, zs, onorm, Ss):
    C = gates.shape[0]
    ri = lax.broadcasted_iota(jnp.int32, (C, C), 0)
    ci = lax.broadcasted_iota(jnp.int32, (C, C), 1)
    incl, strict = ri >= ci, ri > ci
    gcum = _cumsum_rows(gates)
    gcum_t = gcum.T
    lane = lax.broadcasted_iota(jnp.int32, gates.shape, 1)
    sub = lax.broadcasted_iota(jnp.int32, gcum_t.shape, 0)
    last = lax.broadcasted_iota(jnp.int32, (C, 1), 0) == C - 1
    outs, nxt = [], []
    for h in range(N_HEAD_AB):
        bcol = jnp.sum(jnp.where(lane == h, gates, 0.0), axis=1, keepdims=True)
        gcol = jnp.sum(jnp.where(lane == N_HEAD_AB + h, gcum, 0.0), axis=1, keepdims=True)
        grow = jnp.sum(jnp.where(sub == N_HEAD_AB + h, gcum_t, 0.0), axis=0, keepdims=True)
        gl = jnp.sum(jnp.where(last, gcol, 0.0), axis=0, keepdims=True)
        decay = jnp.where(incl, jnp.exp(jnp.where(incl, gcol - grow, 0.0)), 0.0)
        q = qs[h] * DH_AB ** -0.5
        k, v, S = ks[h], vs[h], Ss[h]
        kb = k * bcol
        a = jnp.where(strict, _dot_nt(kb, k) * decay, 0.0)
        t = _unit_lower_inv(a)
        eg = jnp.exp(gcol)
        u = _dot(t, v * bcol)
        w = _dot(t, kb * eg)
        qk = jnp.where(incl, _dot_nt(q, k) * decay, 0.0)
        v_new = u - _dot(w, S)
        o = _dot(q * eg, S) + _dot(qk, v_new)
        nxt.append(S * jnp.exp(gl) + _dot_tn(k * jnp.exp(gl - gcol), v_new))
        outs.append(_rms(o, onorm) * _silu(zs[h]))
    return outs, nxt


def _heads(ref, r0=None):
    rows = slice(None) if r0 is None else slice(r0, r0 + GDN_C)
    return [ref[rows, DH_AB * h:DH_AB * (h + 1)] for h in range(N_HEAD_AB)]


def _gdn_per_step(n_chunks):
    return next(p for p in (5, 2, 1) if n_chunks % p == 0)


def gdn_chunk_fwd(q, k, v, gates, proj, onorm):
    T = q.shape[0]
    P = _gdn_per_step(q.shape[0] // GDN_C)
    C = GDN_C * P
    NC = T // GDN_C

    def body(q_r, k_r, v_r, g_r, z_r, on_r, o_ref, sall_ref, S):
        @pl.when(pl.program_id(0) == 0)
        def _():
            S[...] = jnp.zeros_like(S)
        Ss = [S[h] for h in range(N_HEAD_AB)]
        for j in range(P):
            r0 = GDN_C * j
            for h in range(N_HEAD_AB):
                sall_ref[j, h] = Ss[h]
            outs, Ss = _gdn_chunk(_heads(q_r, r0), _heads(k_r, r0), _heads(v_r, r0), g_r[r0:r0 + GDN_C, :], _heads(z_r, r0), on_r[...], Ss)
            for h in range(N_HEAD_AB):
                o_ref[r0:r0 + GDN_C, DH_AB * h:DH_AB * (h + 1)] = outs[h].astype(o_ref.dtype)
        for h in range(N_HEAD_AB):
            S[h] = Ss[h]

    wide = pl.BlockSpec((C, W_AB), lambda n: (n, 0))
    return pl.pallas_call(
        body, name="gdn_chunk_fwd", grid=(NC // P,),
        in_specs=[wide] * 3 + [pl.BlockSpec((C, BLK), lambda n: (n, 0)), pl.BlockSpec((C, W_AB), lambda n: (n, 7)),
                               pl.BlockSpec((1, DH_AB), lambda n: (0, 0))],
        out_specs=[wide, pl.BlockSpec((P, N_HEAD_AB, DH_AB, DH_AB), lambda n: (n, 0, 0, 0))],
        out_shape=[jax.ShapeDtypeStruct((T, W_AB), bf16), jax.ShapeDtypeStruct((NC, N_HEAD_AB, DH_AB, DH_AB), f32)],
        scratch_shapes=[pltpu.VMEM((N_HEAD_AB, DH_AB, DH_AB), f32)],
        compiler_params=_cparams(dimension_semantics=("arbitrary",)),
    )(q, k, v, gates, proj, onorm)


def gdn_chunk_bwd(q, k, v, gates, proj, onorm, sall, dmix):
    T = q.shape[0]
    P = _gdn_per_step(q.shape[0] // GDN_C)
    C = GDN_C * P
    NC = T // GDN_C
    L = NC // P - 1

    def body(q_r, k_r, v_r, g_r, z_r, on_r, sall_r, do_r, dq_o, dk_o, dv_o, dz_o, dg_o, don_o, dS):
        @pl.when(pl.program_id(0) == 0)
        def _():
            dS[...] = jnp.zeros_like(dS)
        dSs = [dS[h] for h in range(N_HEAD_AB)]
        don_sum = jnp.zeros((1, DH_AB), f32)
        for j in reversed(range(P)):
            r0 = GDN_C * j
            Ss = [sall_r[j, h] for h in range(N_HEAD_AB)]
            _, vjp = jax.vjp(_gdn_chunk, _heads(q_r, r0), _heads(k_r, r0), _heads(v_r, r0), g_r[r0:r0 + GDN_C, :], _heads(z_r, r0),
                             on_r[...], Ss)
            dqs, dks, dvs, dg, dzs, don, dSs = vjp((_heads(do_r, r0), dSs))
            for h in range(N_HEAD_AB):
                sl = slice(DH_AB * h, DH_AB * (h + 1))
                dq_o[r0:r0 + GDN_C, sl] = dqs[h]
                dk_o[r0:r0 + GDN_C, sl] = dks[h]
                dv_o[r0:r0 + GDN_C, sl] = dvs[h]
                dz_o[r0:r0 + GDN_C, sl] = dzs[h]
            dg_o[r0:r0 + GDN_C, :] = dg
            don_sum = don_sum + don
        for h in range(N_HEAD_AB):
            dS[h] = dSs[h]
        _acc8(don_o, don_sum, pl.program_id(0) == 0)

    wide = pl.BlockSpec((C, W_AB), lambda n: (L - n, 0))
    sh = jax.ShapeDtypeStruct((T, W_AB), f32)
    return pl.pallas_call(
        body, name="gdn_chunk_bwd", grid=(NC // P,),
        in_specs=[wide] * 3 + [pl.BlockSpec((C, BLK), lambda n: (L - n, 0)), pl.BlockSpec((C, W_AB), lambda n: (L - n, 7)),
                               pl.BlockSpec((1, DH_AB), lambda n: (0, 0)),
                               pl.BlockSpec((P, N_HEAD_AB, DH_AB, DH_AB), lambda n: (L - n, 0, 0, 0)),
                               pl.BlockSpec((C, W_AB), lambda n: (L - n, 1))],
        out_specs=[wide] * 4 + [pl.BlockSpec((C, BLK), lambda n: (L - n, 0)), pl.BlockSpec((8, DH_AB), lambda n: (0, 0))],
        out_shape=[sh] * 4 + [jax.ShapeDtypeStruct((T, BLK), f32), jax.ShapeDtypeStruct((8, DH_AB), f32)],
        scratch_shapes=[pltpu.VMEM((N_HEAD_AB, DH_AB, DH_AB), f32)],
        compiler_params=_cparams(dimension_semantics=("arbitrary",)),
    )(q, k, v, gates, proj, onorm, sall, dmix)


DH_CD = 64
SWA_G = 4
SWA_KV = 2
W_CD = 512


def _swa_block(q_ref, km, kp, kc, vm, vp, vc, sinks, g, n):
    scale = DH_CD ** -0.5
    ks = slice(DH_CD * g, DH_CD * (g + 1))
    Q = jnp.concatenate([q_ref[:, DH_CD * (SWA_G * g + j):DH_CD * (SWA_G * g + j + 1)] for j in range(SWA_G)], axis=0) * scale
    K3 = jnp.concatenate([km[:, ks], kp[:, ks], kc[:, ks]], axis=0)
    V3 = jnp.concatenate([vm[:, ks], vp[:, ks], vc[:, ks]], axis=0)
    s = _dot_nt(Q, K3)
    shp = s.shape
    row = lax.broadcasted_iota(jnp.int32, shp, 0)
    col = lax.broadcasted_iota(jnp.int32, shp, 1)
    i, part, j = row % BLK, col // BLK, col % BLK
    meta = (part == 0) & (j >= PAD) & ((j <= i) | (n > 0))
    prev = (part == 1) & (j > i) & (n >= 2)
    cur = (part == 2) & (j <= i) & (n >= 1)
    valid = meta | prev | cur
    grp = lax.broadcasted_iota(jnp.int32, (shp[0], 1), 0) // BLK
    sink = jnp.zeros((shp[0], 1), f32)
    for jj in range(SWA_G):
        sink = jnp.where(grp == jj, sinks[SWA_G * g + jj], sink)
    m = jnp.maximum(jnp.max(jnp.where(valid, s, NEG), axis=1, keepdims=True), sink)
    p = jnp.where(valid, jnp.exp(jnp.where(valid, s - m, 0.0)), 0.0)
    es = jnp.exp(sink - m)
    denom = jnp.sum(p, axis=1, keepdims=True) + es
    return Q, K3, V3, p / denom, es / denom, grp


def _swa_in_specs(rev=None):
    row = (lambda n: n) if rev is None else (lambda n: rev - n)
    kcol, vcol = 512 // BLK, 640 // BLK
    specs = [pl.BlockSpec((BLK, W_CD), lambda n: (row(n), 0))]
    for col in (kcol, vcol):
        specs += [pl.BlockSpec((BLK, BLK), functools.partial(lambda n, c: (0, c), c=col)),
                  pl.BlockSpec((BLK, BLK), functools.partial(lambda n, c: (jnp.maximum(row(n) - 1, 0), c), c=col)),
                  pl.BlockSpec((BLK, BLK), functools.partial(lambda n, c: (row(n), c), c=col))]
    return specs + [pl.BlockSpec(memory_space=pltpu.SMEM)]


def swa_fwd(proj, sinks):
    T = proj.shape[0]
    NB = T // BLK

    def body(q_ref, km, kp, kc, vm, vp, vc, sinks_ref, o_ref):
        n = pl.program_id(0)
        for g in range(SWA_KV):
            Q, K3, V3, pn, ps, grp = _swa_block(q_ref, km, kp, kc, vm, vp, vc, sinks_ref, g, n)
            o = _dot(pn, V3)
            for j in range(SWA_G):
                hd = SWA_G * g + j
                o_ref[:, DH_CD * hd:DH_CD * (hd + 1)] = o[BLK * j:BLK * (j + 1), :]

    return pl.pallas_call(
        body, name="swa_fwd", grid=(NB,), in_specs=_swa_in_specs(),
        out_specs=pl.BlockSpec((BLK, W_CD), lambda n: (n, 0)), out_shape=jax.ShapeDtypeStruct((T, W_CD), f32),
        compiler_params=_cparams(dimension_semantics=("arbitrary",)),
    )(proj, proj, proj, proj, proj, proj, proj, sinks)


def swa_bwd(proj, sinks, dmix):
    T = proj.shape[0]
    NB = T // BLK
    KV = 2 * SWA_KV * DH_CD

    def body(q_ref, km, kp, kc, vm, vp, vc, sinks_ref, do_ref, dq_ref, cur_ref, prev_ref, meta_ref, ds_ref):
        n = pl.program_id(0)

        @pl.when(n == 0)
        def _():
            meta_ref[...] = jnp.zeros_like(meta_ref)
            ds_ref[...] = jnp.zeros_like(ds_ref)
        rows = []
        for g in range(SWA_KV):
            Q, K3, V3, pn, ps, grp = _swa_block(q_ref, km, kp, kc, vm, vp, vc, sinks_ref, g, n)
            dO = jnp.concatenate([do_ref[:, DH_CD * (SWA_G * g + j):DH_CD * (SWA_G * g + j + 1)] for j in range(SWA_G)], axis=0)
            dP = _dot_nt(dO, V3)
            delta = jnp.sum(pn * dP, axis=1, keepdims=True)
            dS = pn * (dP - delta)
            dQ = _dot(dS, K3) * DH_CD ** -0.5
            dK3 = _dot_tn(dS, Q)
            dV3 = _dot_tn(pn, dO)
            dsk = -ps * delta
            for j in range(SWA_G):
                hd = SWA_G * g + j
                dq_ref[:, DH_CD * hd:DH_CD * (hd + 1)] = dQ[BLK * j:BLK * (j + 1), :]
                rows.append(jnp.broadcast_to(jnp.sum(jnp.where(grp == j, dsk, 0.0), axis=0, keepdims=True), (1, BLK)))
            kcols = slice(DH_CD * g, DH_CD * (g + 1))
            vcols = slice(SWA_KV * DH_CD + DH_CD * g, SWA_KV * DH_CD + DH_CD * (g + 1))
            meta_ref[:, kcols] += dK3[0:BLK]
            meta_ref[:, vcols] += dV3[0:BLK]
            prev_ref[:, kcols] = dK3[BLK:2 * BLK]
            prev_ref[:, vcols] = dV3[BLK:2 * BLK]
            cur_ref[:, kcols] = dK3[2 * BLK:]
            cur_ref[:, vcols] = dV3[2 * BLK:]
        ds_ref[...] += jnp.concatenate(rows, axis=0)

    kv = pl.BlockSpec((BLK, KV), lambda n: (n, 0))
    return pl.pallas_call(
        body, name="swa_bwd", grid=(NB,),
        in_specs=_swa_in_specs() + [pl.BlockSpec((BLK, W_CD), lambda n: (n, 0))],
        out_specs=[pl.BlockSpec((BLK, W_CD), lambda n: (n, 0)), kv, kv, pl.BlockSpec((BLK, KV), lambda n: (0, 0)),
                   pl.BlockSpec((8, BLK), lambda n: (0, 0))],
        out_shape=[jax.ShapeDtypeStruct((T, W_CD), f32), jax.ShapeDtypeStruct((T, KV), f32), jax.ShapeDtypeStruct((T, KV), f32),
                   jax.ShapeDtypeStruct((BLK, KV), f32), jax.ShapeDtypeStruct((8, BLK), f32)],
        compiler_params=_cparams(dimension_semantics=("arbitrary",)),
    )(proj, proj, proj, proj, proj, proj, proj, sinks, dmix)


SB_PAIR = 2


def _sb_consts():
    r = lax.broadcasted_iota(jnp.int32, (BLK, BLK), 0)
    c = lax.broadcasted_iota(jnp.int32, (BLK, BLK), 1)
    return r, c, (r > c).astype(bf16), (r >= c).astype(bf16)


def _sb_block(q, kb, n, m, r, c):
    z = _dot_nt(q, kb)
    valid = ((m * BLK + c) < (n * BLK + r)) & ((m * BLK + c) >= PAD)
    sp = _softplus(z)
    return z, valid, jnp.where(valid, -sp, 0.0), sp


def _sb_specs(T):
    qcol, kcol, vcol = 768 // BLK, 1280 // BLK, 1792 // BLK
    return [pl.BlockSpec((BLK, BLK), lambda hp, n: (n, qcol + hp)),
            pl.BlockSpec((T, BLK), lambda hp, n: (0, kcol + hp)),
            pl.BlockSpec((T, BLK), lambda hp, n: (0, vcol + hp))]


def sb_fwd(proj):
    T = proj.shape[0]
    NB = T // BLK

    def body(q_ref, k_ref, v_ref, o_ref):
        n = pl.program_id(1)
        r, c, m_gt, _ = _sb_consts()
        heads = [slice(DH_CD * hh, DH_CD * (hh + 1)) for hh in range(SB_PAIR)]
        qs = [q_ref[:, cols] * DH_CD ** -0.5 for cols in heads]

        def cond(carry):
            m, runs, _ = carry
            return jnp.logical_and(m >= 0, jnp.max(jnp.maximum(runs[0], runs[1])) > SB_EXIT)

        def step(carry):
            m, runs, accs = carry
            off = pl.multiple_of(m * BLK, BLK)
            new_runs, new_accs = [], []
            for hh, cols in enumerate(heads):
                kb = k_ref[pl.ds(off, BLK), cols]
                vb = v_ref[pl.ds(off, BLK), cols]
                z, valid, l, sp = _sb_block(qs[hh], kb, n, m, r, c)
                e = (z - sp) + _dot2(l, m_gt) + runs[hh]
                a = jnp.where(valid, jnp.exp(jnp.where(valid, e, 0.0)), 0.0)
                new_runs.append(runs[hh] + jnp.sum(l, axis=1, keepdims=True))
                new_accs.append(accs[hh] + _dot(a, vb))
            return m - 1, tuple(new_runs), tuple(new_accs)

        zero = jnp.zeros((BLK, 1), f32)
        acc0 = jnp.zeros((BLK, DH_CD), f32)
        _, _, accs = lax.while_loop(cond, step, (n, (zero, zero), (acc0, acc0)))
        for hh, cols in enumerate(heads):
            o_ref[:, cols] = accs[hh]

    return pl.pallas_call(
        body, name="sb_fwd", grid=(W_CD // BLK, NB), in_specs=_sb_specs(T),
        out_specs=pl.BlockSpec((BLK, BLK), lambda hp, n: (n, hp)), out_shape=jax.ShapeDtypeStruct((T, W_CD), f32),
        compiler_params=_cparams(dimension_semantics=("arbitrary", "arbitrary")),
    )(proj, proj, proj)


def sb_bwd(proj, o, dmix):
    T = proj.shape[0]
    NB = T // BLK

    def body(q_ref, k_ref, v_ref, o_ref, do_ref, dq_ref, dk_ref, dv_ref):
        n = pl.program_id(1)

        @pl.when(n == 0)
        def _():
            dk_ref[...] = jnp.zeros_like(dk_ref)
            dv_ref[...] = jnp.zeros_like(dv_ref)
        r, c, m_gt, m_ge = _sb_consts()
        heads = [slice(DH_CD * hh, DH_CD * (hh + 1)) for hh in range(SB_PAIR)]
        qs = [q_ref[:, cols] * DH_CD ** -0.5 for cols in heads]
        dOs = [do_ref[:, cols].astype(bf16) for cols in heads]
        deltas = [jnp.sum(dOs[hh].astype(f32) * o_ref[:, cols], axis=1, keepdims=True) for hh, cols in enumerate(heads)]

        def cond(carry):
            m, runs = carry[0], carry[1]
            return jnp.logical_and(m >= 0, jnp.max(jnp.maximum(runs[0], runs[1])) > SB_EXIT)

        def step(carry):
            m, runs, runs_e, dqs = carry
            off = pl.multiple_of(m * BLK, BLK)
            new_runs, new_runs_e, new_dqs, dks, dvs = [], [], [], [], []
            for hh, cols in enumerate(heads):
                kb = k_ref[pl.ds(off, BLK), cols]
                vb = v_ref[pl.ds(off, BLK), cols]
                z, valid, l, sp = _sb_block(qs[hh], kb, n, m, r, c)
                e = (z - sp) + _dot2(l, m_gt) + runs[hh]
                a = jnp.where(valid, jnp.exp(jnp.where(valid, e, 0.0)), 0.0).astype(bf16)
                E = a.astype(f32) * _dot_nt(dOs[hh], vb)
                F = deltas[hh] - runs_e[hh] - _dot2(E, m_ge)
                sig = jnp.exp(z - sp)
                live = jnp.logical_and(valid, jnp.max(runs[hh]) > SB_EXIT)
                dz = jnp.where(live, E * (1.0 - sig) - F * sig, 0.0)
                dks.append(_dot_tn(dz, qs[hh]))
                dvs.append(_dot_tn(a, dOs[hh]))
                new_runs.append(runs[hh] + jnp.sum(l, axis=1, keepdims=True))
                new_runs_e.append(runs_e[hh] + jnp.sum(E, axis=1, keepdims=True))
                new_dqs.append(dqs[hh] + _dot(dz, kb))
            dk_ref[pl.ds(off, BLK), :] += jnp.concatenate(dks, axis=1)
            dv_ref[pl.ds(off, BLK), :] += jnp.concatenate(dvs, axis=1)
            return m - 1, tuple(new_runs), tuple(new_runs_e), tuple(new_dqs)

        zero = jnp.zeros((BLK, 1), f32)
        dq0 = jnp.zeros((BLK, DH_CD), f32)
        res = lax.while_loop(cond, step, (n, (zero, zero), (zero, zero), (dq0, dq0)))
        for hh, cols in enumerate(heads):
            dq_ref[:, cols] = res[3][hh] * DH_CD ** -0.5

    blk = pl.BlockSpec((BLK, BLK), lambda hp, n: (n, hp))
    full = pl.BlockSpec((T, BLK), lambda hp, n: (0, hp))
    sh = jax.ShapeDtypeStruct((T, W_CD), f32)
    return pl.pallas_call(
        body, name="sb_bwd", grid=(W_CD // BLK, NB),
        in_specs=_sb_specs(T) + [blk, pl.BlockSpec((BLK, BLK), lambda hp, n: (n, W_CD // BLK + hp))],
        out_specs=[blk, full, full], out_shape=[sh] * 3,
        compiler_params=_cparams(dimension_semantics=("arbitrary", "arbitrary")),
    )(proj, proj, proj, o, dmix)


def cd_assemble(dcq, cur, prev, meta, dsq, dsk, dsv):
    T = dcq.shape[0]
    NB = T // BLK
    KV = cur.shape[1]

    def body(dcq_r, cur_r, nxt_r, meta_r, dsq_r, dsk_r, dsv_r, o_ref):
        n = pl.program_id(0)
        kv = cur_r[...] + jnp.where(n < NB - 1, nxt_r[...], 0.0) + jnp.where(n == 0, meta_r[...], 0.0)
        o_ref[:, 0:W_CD] = dcq_r[...].astype(o_ref.dtype)
        o_ref[:, W_CD:W_CD + KV] = kv.astype(o_ref.dtype)
        for j, ref in enumerate((dsq_r, dsk_r, dsv_r)):
            o_ref[:, W_CD + KV + W_CD * j:W_CD + KV + W_CD * (j + 1)] = ref[...].astype(o_ref.dtype)

    wide = pl.BlockSpec((BLK, W_CD), lambda n: (n, 0))
    return pl.pallas_call(
        body, name="cd_assemble", grid=(NB,),
        in_specs=[wide, pl.BlockSpec((BLK, KV), lambda n: (n, 0)), pl.BlockSpec((BLK, KV), lambda n: (jnp.minimum(n + 1, NB - 1), 0)),
                  pl.BlockSpec((BLK, KV), lambda n: (0, 0)), wide, wide, wide],
        out_specs=pl.BlockSpec((BLK, CD_IN), lambda n: (n, 0)), out_shape=jax.ShapeDtypeStruct((T, CD_IN), bf16),
        compiler_params=_cparams(dimension_semantics=("arbitrary",)),
    )(dcq, cur, prev, meta, dsq, dsk, dsv)


def loss_grad(h, target):
    T, Dm = h.shape
    NB = T // BLK

    def body(h_ref, t_ref, dh_ref, l_ref):
        n = pl.program_id(0)

        @pl.when(n == 0)
        def _():
            dh_ref[...] = jnp.zeros_like(dh_ref)
            l_ref[...] = jnp.zeros_like(l_ref)

        @pl.when(n > 0)
        def _():
            err = h_ref[...] - t_ref[...]
            dh_ref[...] = err * (1.0 / Dm)
            part = 0.5 * jnp.sum(jnp.mean(err * err, axis=-1, keepdims=True), axis=0, keepdims=True)
            l_ref[...] += jnp.broadcast_to(part, l_ref.shape)

    row = pl.BlockSpec((BLK, Dm), lambda n: (n, 0))
    return pl.pallas_call(
        body, name="loss_grad", grid=(NB,),
        in_specs=[row, pl.BlockSpec((BLK, Dm), lambda n: (jnp.maximum(n - 1, 0), 0))],
        out_specs=[row, pl.BlockSpec((8, BLK), lambda n: (0, 0))],
        out_shape=[jax.ShapeDtypeStruct((T, Dm), f32), jax.ShapeDtypeStruct((8, BLK), f32)],
        compiler_params=_cparams(dimension_semantics=("arbitrary",)),
    )(h, target)


SUM_ROWS = 256
_MESH = pl.DeviceIdType.MESH
_ANY = pl.BlockSpec(memory_space=pl.ANY)


def _place():
    return lax.axis_index("x"), lax.axis_index("y"), lax.axis_index("c")


def _other_chips(x, y):
    return [(1 - x, y, 2 * (1 - x) + y), (x, 1 - y, 2 * x + 1 - y), (1 - x, 1 - y, 2 * (1 - x) + 1 - y)]


def gather_weights(wbuf, sbuf):
    def body(w_ref, s_ref, out_ref, outs_ref, send_sems, recv_sems):
        x, y, c = _place()
        p = 2 * x + y
        chips = _other_chips(x, y)
        sibling = (x, y, 1 - c)

        def copy(k, src, dst, to):
            return pltpu.make_async_remote_copy(src_ref=src, dst_ref=dst, send_sem=send_sems.at[k], recv_sem=recv_sems.at[k],
                                                device_id=to, device_id_type=_MESH)

        sends = [copy(9, w_ref, out_ref.at[p], sibling), copy(10, s_ref, outs_ref.at[p], sibling)]
        for j, (qx, qy, q) in enumerate(chips):
            sends.append(copy(j, w_ref.at[c], out_ref.at[p, c], (qx, qy, c)))
            sends.append(copy(3 + j, s_ref, outs_ref.at[p], (qx, qy, c)))
        for cp in sends:
            cp.start()
        for j, (qx, qy, q) in enumerate(chips):
            copy(j, w_ref.at[c], out_ref.at[q, c], (qx, qy, c)).wait_recv()
            fwd = copy(6 + j, out_ref.at[q, c], out_ref.at[q, c], sibling)
            fwd.start()
            sends.append(fwd)
        for j, (qx, qy, q) in enumerate(chips):
            copy(3 + j, s_ref, outs_ref.at[q], (qx, qy, c)).wait_recv()
            copy(6 + j, out_ref.at[q, 1 - c], out_ref.at[q, 1 - c], sibling).wait_recv()
        copy(9, w_ref, out_ref.at[p], sibling).wait_recv()
        copy(10, s_ref, outs_ref.at[p], sibling).wait_recv()
        for cp in sends:
            cp.wait_send()

    return pl.pallas_call(
        body, name="gather_weights", in_specs=[_ANY, _ANY], out_specs=[_ANY, _ANY],
        out_shape=[jax.ShapeDtypeStruct((4,) + wbuf.shape, wbuf.dtype), jax.ShapeDtypeStruct((4,) + sbuf.shape, sbuf.dtype)],
        scratch_shapes=[pltpu.SemaphoreType.DMA((11,)), pltpu.SemaphoreType.DMA((11,))],
    )(wbuf, sbuf)


def pair_exchange(g):
    S, _, H, Cw = g.shape

    def body(g_ref, out_ref, send_sem, recv_sem):
        x, y, c = _place()
        cp = pltpu.make_async_remote_copy(src_ref=g_ref.at[:, 1 - c], dst_ref=out_ref, send_sem=send_sem, recv_sem=recv_sem,
                                          device_id=(x, y, 1 - c), device_id_type=_MESH)
        cp.start()
        cp.wait()

    return pl.pallas_call(
        body, name="pair_exchange", in_specs=[_ANY], out_specs=_ANY, out_shape=jax.ShapeDtypeStruct((S, H, Cw), g.dtype),
        scratch_shapes=[pltpu.SemaphoreType.DMA, pltpu.SemaphoreType.DMA],
    )(g)


def pair_sum(g, got, c):
    S, _, H, Cw = g.shape
    tb = 3 * SUM_ROWS if H % (3 * SUM_ROWS) == 0 else SUM_ROWS

    def body(c_ref, a_ref, b_ref, o_ref):
        o_ref[...] = (a_ref[...].astype(f32) + b_ref[...].astype(f32)).astype(o_ref.dtype)

    spec = pl.BlockSpec((None, tb, Cw), lambda s, i, c_ref: (s, i, 0))
    return pl.pallas_call(
        body, name="pair_sum",
        grid_spec=pltpu.PrefetchScalarGridSpec(
            num_scalar_prefetch=1, grid=(S, H // tb),
            in_specs=[pl.BlockSpec((None, None, tb, Cw), lambda s, i, c_ref: (s, c_ref[0], i, 0)), spec], out_specs=spec),
        out_shape=jax.ShapeDtypeStruct((S, H, Cw), g.dtype),
        compiler_params=_cparams(dimension_semantics=("arbitrary", "arbitrary")),
    )(c, g, got)


def chip_exchange(hsum):
    S, H, Cw = hsum.shape

    def body(h_ref, out_ref, send_sems, recv_sems):
        x, y, c = _place()
        sends = []
        for j, (qx, qy, q) in enumerate(_other_chips(x, y)):
            cp = pltpu.make_async_remote_copy(src_ref=h_ref.at[q], dst_ref=out_ref.at[j], send_sem=send_sems.at[j],
                                              recv_sem=recv_sems.at[j], device_id=(qx, qy, c), device_id_type=_MESH)
            cp.start()
            sends.append(cp)
        for cp in sends:
            cp.wait()

    return pl.pallas_call(
        body, name="chip_exchange", in_specs=[_ANY], out_specs=_ANY, out_shape=jax.ShapeDtypeStruct((3, H, Cw), hsum.dtype),
        scratch_shapes=[pltpu.SemaphoreType.DMA((3,)), pltpu.SemaphoreType.DMA((3,))],
    )(hsum)


def chip_sum(hsum, parts, p):
    S, H, Cw = parts.shape
    tb = 3 * SUM_ROWS if H % (3 * SUM_ROWS) == 0 else SUM_ROWS

    def body(p_ref, own_ref, parts_ref, o_ref):
        acc = own_ref[...].astype(f32)
        for s in range(S):
            acc = acc + parts_ref[s].astype(f32)
        o_ref[...] = acc

    return pl.pallas_call(
        body, name="chip_sum",
        grid_spec=pltpu.PrefetchScalarGridSpec(
            num_scalar_prefetch=1, grid=(H // tb,),
            in_specs=[pl.BlockSpec((None, tb, Cw), lambda i, p_ref: (p_ref[0], i, 0)), pl.BlockSpec((S, tb, Cw), lambda i, p_ref: (0, i, 0))],
            out_specs=pl.BlockSpec((tb, Cw), lambda i, p_ref: (i, 0))),
        out_shape=jax.ShapeDtypeStruct((H, Cw), f32),
        compiler_params=_cparams(dimension_semantics=("arbitrary",)),
    )(p, hsum, parts)


def pair_gather(rsum):
    def body(r_ref, out_ref, send_sem, recv_sem):
        x, y, c = _place()
        cp = pltpu.make_async_remote_copy(src_ref=r_ref, dst_ref=out_ref, send_sem=send_sem, recv_sem=recv_sem,
                                          device_id=(x, y, 1 - c), device_id_type=_MESH)
        cp.start()
        cp.wait()

    return pl.pallas_call(
        body, name="pair_gather", in_specs=[_ANY], out_specs=_ANY, out_shape=jax.ShapeDtypeStruct(rsum.shape, rsum.dtype),
        scratch_shapes=[pltpu.SemaphoreType.DMA, pltpu.SemaphoreType.DMA],
    )(rsum)


def small_reduce(src):
    S, RS, Cw = src.shape

    def body(src_ref, out_ref, recv, send_sems, recv_sems):
        x, y, c = _place()
        me = 4 * x + 2 * y + c
        p = 2 * x + y
        recv[me] = src_ref[p]
        flips = [(fx, fy, fc) for fx in (0, 1) for fy in (0, 1) for fc in (0, 1)][1:]
        sends = []
        for k, (fx, fy, fc) in enumerate(flips):
            tx, ty, tc = (1 - x if fx else x), (1 - y if fy else y), (1 - c if fc else c)
            cp = pltpu.make_async_remote_copy(src_ref=src_ref.at[2 * tx + ty], dst_ref=recv.at[me], send_sem=send_sems.at[k],
                                              recv_sem=recv_sems.at[me], device_id=(tx, ty, tc), device_id_type=_MESH)
            cp.start()
            sends.append(cp)
        for k, (fx, fy, fc) in enumerate(flips):
            tx, ty, tc = (1 - x if fx else x), (1 - y if fy else y), (1 - c if fc else c)
            frm = 4 * tx + 2 * ty + tc
            pltpu.make_async_remote_copy(src_ref=src_ref.at[p], dst_ref=recv.at[frm], send_sem=send_sems.at[k],
                                         recv_sem=recv_sems.at[frm], device_id=(tx, ty, tc), device_id_type=_MESH).wait_recv()
        for cp in sends:
            cp.wait_send()
        acc = recv[0]
        for d in range(1, 8):
            acc = acc + recv[d]
        out_ref[...] = acc

    vm = pl.BlockSpec(memory_space=pltpu.VMEM)
    return pl.pallas_call(
        body, name="small_reduce", in_specs=[vm], out_specs=vm, out_shape=jax.ShapeDtypeStruct((RS, Cw), f32),
        scratch_shapes=[pltpu.VMEM((8, RS, Cw), f32), pltpu.SemaphoreType.DMA((7,)), pltpu.SemaphoreType.DMA((8,))],
    )(src)


def _row(v):
    return v.reshape(1, -1)


def _ffn_fwd(h, g_pre, g_post, wg, wu, wd):
    u, G, U, a = ffn_up(h, _row(g_pre), wg, wu)
    y, h_new = proj_norm_res(a, wd, h, _row(g_post), 0.5)
    return h_new, (h, u, G, U, a, y)


def _ffn_bwd(dh, saved, g_pre, g_post, wg, wu, wd):
    h, u, G, U, a, y = saved
    F = wg.shape[2]
    dy, dg_post = post_norm_bwd(y, _row(g_post), dh, 0.5)
    dG, dU = ffn_bwd_act(dy, wd, G, U)
    dwd = mm_tn(a, dy[None], D)
    dwg = mm_tn(u[None], dG, F)
    dwu = mm_tn(u[None], dU, F)
    dh_new, dg_pre = mm_nt_norm_bwd([(dG, wg), (dU, wu)], h, _row(g_pre), dh)
    return dh_new, dwg, dwu, dwd, dg_pre[0], dg_post[0]


def _lane_vec(v, at):
    return jnp.pad(v, (at, BLK - at - v.shape[0])).reshape(1, BLK)


def _ab_fwd(h, g_pre, g_post, w, tabs):
    u, proj = norm_proj(h, _row(g_pre), w["ab_in"], AB_IN_P // 3)
    ret, sall_r = ret_fwd(proj, *tabs)
    alog, dtb = _lane_vec(w["a_log"], N_HEAD_AB), _lane_vec(w["dt_bias"], N_HEAD_AB)
    cq, ck, cv, q, k, v, gates = gdn_prep_fwd(proj, w["conv"], alog, dtb)
    gdn, sall_g = gdn_chunk_fwd(q, k, v, gates, proj, _row(w["out_norm"]))
    mixed = jnp.concatenate([ret, gdn], axis=1)
    y, h_new = proj_norm_res(mixed[None], w["ab_out"][None], h, _row(g_post), 1.0)
    return h_new, (h, u, proj, sall_r, (cq, ck, cv, q, k, v, gates), sall_g, mixed, y, alog, dtb)


def _ab_bwd(dh, saved, g_pre, g_post, w, tabs):
    h, u, proj, sall_r, (cq, ck, cv, q, k, v, gates), sall_g, mixed, y, alog, dtb = saved
    dy, dg_post = post_norm_bwd(y, _row(g_post), dh, 1.0)
    dmix = mm_nt(dy, w["ab_out"])
    dw_out = mm_tn(mixed[None], dy[None], D)[0]
    drq, drk, drv, drg = ret_bwd(proj, *tabs, sall_r, dmix)
    onorm = _row(w["out_norm"])
    dq, dk, dv, dz, dgates, don = gdn_chunk_bwd(q, k, v, gates, proj, onorm, sall_g, dmix)
    dcq, dck, dcv, dgb, dal, ddt = gdn_prep_bwd(cq, ck, cv, proj, alog, dtb, dq, dk, dv, dgates)
    dxq, dxk, dxv, dconv = gdn_conv_bwd(dcq, dck, dcv, proj, w["conv"])
    dproj = jnp.concatenate([t.astype(bf16) for t in (drq, drk, drv, drg, dxq, dxk, dxv, dz, dgb)], axis=1)
    dw_in = mm_tn(u[None], dproj[None], AB_IN_P // 3)[0]
    dh_new, dg_pre = mm_nt_norm_bwd([(dproj[None], w["ab_in"][None])], h, _row(g_pre), dh, ksplit=3)
    small = dict(a_log=dal[0, N_HEAD_AB:2 * N_HEAD_AB], dt_bias=ddt[0, N_HEAD_AB:2 * N_HEAD_AB], out_norm=don[0], conv=dconv[0:GDN_K])
    return dh_new, dw_in, dw_out, dg_pre[0], dg_post[0], small


def _cd_fwd(h, g_pre, g_post, w):
    u, proj = norm_proj(h, _row(g_pre), w["cd_in"], CD_IN // 3)
    swa = swa_fwd(proj, w["sinks"])
    sb = sb_fwd(proj)
    mixed = jnp.concatenate([swa.astype(bf16), sb.astype(bf16)], axis=1)
    y, h_new = proj_norm_res(mixed[None], w["cd_out"][None], h, _row(g_post), 1.0)
    return h_new, (h, u, proj, sb, mixed, y)


def _cd_bwd(dh, saved, g_pre, g_post, w):
    h, u, proj, sb, mixed, y = saved
    dy, dg_post = post_norm_bwd(y, _row(g_post), dh, 1.0)
    dmix = mm_nt(dy, w["cd_out"])
    dw_out = mm_tn(mixed[None], dy[None], D)[0]
    dcq, cur, prev, meta, dsinks = swa_bwd(proj, w["sinks"], dmix)
    dsq, dsk, dsv = sb_bwd(proj, sb, dmix)
    dproj = cd_assemble(dcq, cur, prev, meta, dsq, dsk, dsv)
    dw_in = mm_tn(u[None], dproj[None], CD_IN // 3)[0]
    dh_new, dg_pre = mm_nt_norm_bwd([(dproj[None], w["cd_in"][None])], h, _row(g_pre), dh, ksplit=3)
    return dh_new, dw_in, dw_out, dg_pre[0], dg_post[0], dsinks[:, 0]


def local_step(x, target, w):
    L = x.shape[0]
    T = PAD + N_META + L
    tabs = rot_tables(T)
    h = jnp.concatenate([jnp.zeros((PAD, D), f32), w["meta"], x], axis=0)
    ng = w["norm_gains"]
    saved = []
    for i in range(2):
        g = ng[i]
        h, s1 = _ffn_fwd(h, g[0], g[1], w["wg"][i, 0], w["wu"][i, 0], w["wd"][i, 0])
        if i == 0:
            h, sm = _ab_fwd(h, g[2], g[3], w, tabs)
        else:
            h, sm = _cd_fwd(h, g[2], g[3], w)
        h, s2 = _ffn_fwd(h, g[4], g[5], w["wg"][i, 1], w["wu"][i, 1], w["wd"][i, 1])
        saved.append((s1, sm, s2))
    dh, lpart = loss_grad(h, target)
    grads = {}
    dng = [[None] * 6 for _ in range(2)]
    dwg = [[None, None], [None, None]]
    dwu = [[None, None], [None, None]]
    dwd = [[None, None], [None, None]]
    for i in (1, 0):
        g = ng[i]
        s1, sm, s2 = saved[i]
        dh, dwg[i][1], dwu[i][1], dwd[i][1], dng[i][4], dng[i][5] = _ffn_bwd(dh, s2, g[4], g[5], w["wg"][i, 1], w["wu"][i, 1], w["wd"][i, 1])
        if i == 0:
            dh, grads["ab_in"], grads["ab_out"], dng[i][2], dng[i][3], small = _ab_bwd(dh, sm, g[2], g[3], w, tabs)
            grads.update(small)
        else:
            dh, grads["cd_in"], grads["cd_out"], dng[i][2], dng[i][3], grads["sinks"] = _cd_bwd(dh, sm, g[2], g[3], w)
        dh, dwg[i][0], dwu[i][0], dwd[i][0], dng[i][0], dng[i][1] = _ffn_bwd(dh, s1, g[0], g[1], w["wg"][i, 0], w["wu"][i, 0], w["wd"][i, 0])
    grads["wg"], grads["wu"], grads["wd"] = dwg, dwu, dwd
    grads["norm_gains"] = jnp.stack([jnp.stack(r) for r in dng])
    grads["meta"] = dh[PAD:PAD + N_META]
    return lpart[0, 0], dh[PAD + N_META:], grads


def _r16(n, mult=16):
    return -(-n // mult) * mult


def _big_layout(F):
    halves = (("wg", "wu"), ("wd", "ab_in", "ab_out", "cd_in", "cd_out"))
    rows = dict(wg=4 * F, wu=4 * F, wd=4 * F, ab_in=AB_IN // 4, ab_out=D // 4, cd_in=CD_IN // 4, cd_out=D // 4)
    offs, used = {}, []
    for hf, names in enumerate(halves):
        o = 0
        for n in names:
            offs[n] = (hf, o, rows[n])
            o += _r16(rows[n])
        used.append(o)
    return offs, _r16(max(used), SUM_ROWS), halves


def _cat_rows(parts, total, mult=16):
    out = []
    for p in parts:
        pad = _r16(p.shape[-2], mult) - p.shape[-2]
        out.append(jnp.pad(p, [(0, 0)] * (p.ndim - 2) + [(0, pad), (0, 0)]) if pad else p)
    used = sum(o.shape[-2] for o in out)
    if total > used:
        out.append(jnp.zeros(out[0].shape[:-2] + (total - used, out[0].shape[-1]), out[0].dtype))
    return jnp.concatenate(out, axis=-2)


SMALL_ROWS = 72
REPL_ROWS = 8


def _small_rows(meta, ng, conv):
    lead = meta.shape[:-2]
    return _cat_rows([meta.reshape(lead + (32, BLK)), ng.reshape(lead + (24, BLK)), conv.reshape(lead + (12, BLK))], SMALL_ROWS, 8)


def _small_unrows(buf):
    lead = buf.shape[:-2]
    return buf[..., 0:32, :].reshape(lead + (N_META, D // 4)), buf[..., 32:56, :].reshape(lead + (2, 6, D // 4)), \
        buf[..., 56:68, :].reshape(lead + (GDN_K, 3 * W_AB // 4))


def _shard_cols(a, axis):
    shp = a.shape
    a = a.reshape(shp[:axis] + (4, shp[axis] // 4) + shp[axis + 1:])
    return jnp.moveaxis(a, axis, 0)


def _unshard_cols(a, axis):
    a = jnp.moveaxis(a, 0, axis)
    shp = a.shape
    return a.reshape(shp[:axis] + (4 * shp[axis + 1],) + shp[axis + 2:])


def kernel(x, meta_tokens, norm_gains, ffn_w_gate, ffn_w_up, ffn_w_down, ab_w_in, ab_conv_w, ab_a_log, ab_dt_bias, ab_out_norm, ab_w_out, cd_w_in, cd_sinks, cd_w_out, loss_target, m_meta_tokens, m_norm_gains, m_ffn_w_gate, m_ffn_w_up, m_ffn_w_down, m_ab_w_in, m_ab_conv_w, m_ab_a_log, m_ab_dt_bias, m_ab_out_norm, m_ab_w_out, m_cd_w_in, m_cd_sinks, m_cd_w_out, v_meta_tokens, v_norm_gains, v_ffn_w_gate, v_ffn_w_up, v_ffn_w_down, v_ab_w_in, v_ab_conv_w, v_ab_a_log, v_ab_dt_bias, v_ab_out_norm, v_ab_w_out, v_cd_w_in, v_cd_sinks, v_cd_w_out):
    F = ffn_w_gate.shape[-1]
    offs, H, halves = _big_layout(F)
    shard = dict(wg=ffn_w_gate, wu=ffn_w_up, wd=ffn_w_down, ab_in=ab_w_in, ab_out=ab_w_out, cd_in=cd_w_in, cd_out=cd_w_out)

    wbuf = jnp.stack([_cat_rows([shard[n].reshape(-1, D).astype(bf16) for n in names], H) for names in halves])
    sbuf = _small_rows(meta_tokens, norm_gains, ab_conv_w[0])
    gw, gs = gather_weights(wbuf, sbuf)

    def part(n):
        hf, o, r = offs[n]
        return gw[:, hf, o:o + r]

    meta_s, ng_s, conv_s = _small_unrows(gs)
    w = dict(
        wg=jnp.transpose(part("wg").reshape(4, 2, 2, D, F), (1, 2, 0, 3, 4)),
        wu=jnp.transpose(part("wu").reshape(4, 2, 2, D, F), (1, 2, 0, 3, 4)),
        wd=jnp.transpose(part("wd").reshape(4, 2, 2, F, D), (1, 2, 0, 3, 4)),
        ab_in=jnp.pad(_unshard_cols(part("ab_in").reshape(4, D, AB_IN // 4), 1), ((0, 0), (0, AB_IN_P - AB_IN))),
        ab_out=part("ab_out").reshape(D, D),
        cd_in=_unshard_cols(part("cd_in").reshape(4, D, CD_IN // 4), 1),
        cd_out=part("cd_out").reshape(D, D),
        meta=_unshard_cols(meta_s, 1), norm_gains=_unshard_cols(ng_s, 2), conv=_unshard_cols(conv_s, 1),
        a_log=ab_a_log[0], dt_bias=ab_dt_bias[0], out_norm=ab_out_norm[0], sinks=cd_sinks[0],
    )

    loss_local, dx, g = local_step(x[0], loss_target[0], w)

    def stack22(t):
        return jnp.stack([jnp.stack(r) for r in t])

    gparts = dict(
        wg=jnp.transpose(stack22(g["wg"]), (2, 0, 1, 3, 4)).reshape(4, 4 * F, D),
        wu=jnp.transpose(stack22(g["wu"]), (2, 0, 1, 3, 4)).reshape(4, 4 * F, D),
        wd=jnp.transpose(stack22(g["wd"]), (2, 0, 1, 3, 4)).reshape(4, 4 * F, D),
        ab_in=_shard_cols(g["ab_in"][:, :AB_IN], 1).reshape(4, AB_IN // 4, D),
        ab_out=g["ab_out"].reshape(4, D // 4, D),
        cd_in=_shard_cols(g["cd_in"], 1).reshape(4, CD_IN // 4, D),
        cd_out=g["cd_out"].reshape(4, D // 4, D),
    )
    gbuf = jnp.stack([_cat_rows([gparts[n].astype(bf16) for n in names], H) for names in halves], axis=1)
    core = lax.axis_index("c").astype(jnp.int32)
    chip = (2 * lax.axis_index("x") + lax.axis_index("y")).astype(jnp.int32)
    got = pair_exchange(gbuf)
    hsum = pair_sum(gbuf, got, core.reshape(1))
    parts = chip_exchange(hsum)
    rsum = chip_sum(hsum, parts, chip.reshape(1))
    other = pair_gather(rsum)
    ghalf = (jnp.where(core == 0, rsum, other), jnp.where(core == 0, other, rsum))

    onehot = np.eye(REPL_ROWS, dtype=np.float32)
    repl = sum(onehot[k][:, None] * _lane_vec(g[n], 0) for k, n in enumerate(("a_log", "dt_bias", "out_norm", "sinks")))
    ssrc = jnp.concatenate([_small_rows(_shard_cols(g["meta"], 1), _shard_cols(g["norm_gains"], 2), _shard_cols(g["conv"], 1)),
                            jnp.broadcast_to(repl, (4, REPL_ROWS, BLK))], axis=1)
    sred = small_reduce(ssrc)
    g_meta, g_ng, g_conv = _small_unrows(sred[:SMALL_ROWS])

    def gpart(n, shape):
        hf, o, r = offs[n]
        return ghalf[hf][o:o + r].reshape(shape)

    grad = dict(
        meta_tokens=g_meta, norm_gains=g_ng,
        ffn_w_gate=gpart("wg", ffn_w_gate.shape), ffn_w_up=gpart("wu", ffn_w_up.shape), ffn_w_down=gpart("wd", ffn_w_down.shape),
        ab_w_in=gpart("ab_in", ab_w_in.shape), ab_conv_w=g_conv[None],
        ab_a_log=sred[SMALL_ROWS:SMALL_ROWS + 1, 0:N_HEAD_AB], ab_dt_bias=sred[SMALL_ROWS + 1:SMALL_ROWS + 2, 0:N_HEAD_AB],
        ab_out_norm=sred[SMALL_ROWS + 2:SMALL_ROWS + 3, :], ab_w_out=gpart("ab_out", ab_w_out.shape),
        cd_w_in=gpart("cd_in", cd_w_in.shape), cd_sinks=sred[SMALL_ROWS + 3:SMALL_ROWS + 4, 0:2 * SWA_G], cd_w_out=gpart("cd_out", cd_w_out.shape),
    )

    weights = dict(meta_tokens=meta_tokens, norm_gains=norm_gains, ffn_w_gate=ffn_w_gate, ffn_w_up=ffn_w_up, ffn_w_down=ffn_w_down,
                   ab_w_in=ab_w_in, ab_conv_w=ab_conv_w, ab_a_log=ab_a_log, ab_dt_bias=ab_dt_bias, ab_out_norm=ab_out_norm,
                   ab_w_out=ab_w_out, cd_w_in=cd_w_in, cd_sinks=cd_sinks, cd_w_out=cd_w_out)
    ms = dict(meta_tokens=m_meta_tokens, norm_gains=m_norm_gains, ffn_w_gate=m_ffn_w_gate, ffn_w_up=m_ffn_w_up, ffn_w_down=m_ffn_w_down,
              ab_w_in=m_ab_w_in, ab_conv_w=m_ab_conv_w, ab_a_log=m_ab_a_log, ab_dt_bias=m_ab_dt_bias, ab_out_norm=m_ab_out_norm,
              ab_w_out=m_ab_w_out, cd_w_in=m_cd_w_in, cd_sinks=m_cd_sinks, cd_w_out=m_cd_w_out)
    vs = dict(meta_tokens=v_meta_tokens, norm_gains=v_norm_gains, ffn_w_gate=v_ffn_w_gate, ffn_w_up=v_ffn_w_up, ffn_w_down=v_ffn_w_down,
              ab_w_in=v_ab_w_in, ab_conv_w=v_ab_conv_w, ab_a_log=v_ab_a_log, ab_dt_bias=v_ab_dt_bias, ab_out_norm=v_ab_out_norm,
              ab_w_out=v_ab_w_out, cd_w_in=v_cd_w_in, cd_sinks=v_cd_sinks, cd_w_out=v_cd_w_out)
    order = list(weights)
    delta, new_m, new_v = {}, {}, {}
    for n in order:
        shp = weights[n].shape
        two = (-1, shp[-1])
        d, mn, vn = adamw(weights[n].reshape(two), grad[n].reshape(two), ms[n].reshape(two), vs[n].reshape(two))
        delta[n], new_m[n], new_v[n] = d.reshape(shp), mn.reshape(shp), vn.reshape(shp)

    loss = lax.psum(loss_local, ("x", "y", "c"))
    return (loss, dx[None], *[grad[n].reshape(weights[n].shape) for n in order], *[delta[n] for n in order],
            *[new_m[n] for n in order], *[new_v[n] for n in order])
```

```python
import functools

import numpy as np
import jax
import jax.numpy as jnp
from jax import lax
from jax.experimental import pallas as pl
from jax.experimental.pallas import tpu as pltpu

f32 = jnp.float32
bf16 = jnp.bfloat16

EPS = 1e-6
D = 1024
N_META = 16
PAD = 112
BLK = 128
GDN_C = 64
N_HEAD_AB = 4
DH_AB = 128
AB_IN = 4104
AB_IN_P = 4224
CD_IN = 2304
ADAM_LR, ADAM_B1, ADAM_B2, ADAM_EPS, ADAM_WD, ADAM_STEP = 0.001, 0.9, 0.999, 1e-08, 0.01, 10
VMEM_LIMIT = 56 * 1024 * 1024
NEG = -1e30
SB_EXIT = -104.0

_NT = (((1,), (1,)), ((), ()))
_TN = (((0,), (0,)), ((), ()))


def _cparams(**kw):
    return pltpu.CompilerParams(vmem_limit_bytes=VMEM_LIMIT, **kw)


def _dot(a, b):
    return jnp.dot(a.astype(bf16), b.astype(bf16), preferred_element_type=f32)


def _dot_nt(a, b):
    return lax.dot_general(a.astype(bf16), b.astype(bf16), _NT, preferred_element_type=f32)


def _dot_tn(a, b):
    return lax.dot_general(a.astype(bf16), b.astype(bf16), _TN, preferred_element_type=f32)


def _dot2(a, b01):
    hi = a.astype(bf16)
    lo = (a - hi.astype(f32)).astype(bf16)
    return jnp.dot(hi, b01, preferred_element_type=f32) + jnp.dot(lo, b01, preferred_element_type=f32)


def _row_tile(t):
    for c in (640, 512, 256, 128):
        if t % c == 0:
            return c
    raise ValueError(t)


def _sigmoid(x):
    return 1.0 / (1.0 + jnp.exp(-x))


def _sigmoid_fast(x):
    return pl.reciprocal(1.0 + jnp.exp(-x), approx=True)


def _silu(x):
    return x * _sigmoid(x)


def _softplus(x):
    return jnp.maximum(x, 0.0) + jnp.log(1.0 + jnp.exp(-jnp.abs(x)))


def _rms(x, g):
    r = lax.rsqrt(jnp.mean(x * x, axis=-1, keepdims=True) + EPS)
    return x * r * g


def _rms_bwd(x, g, dy):
    r = lax.rsqrt(jnp.mean(x * x, axis=-1, keepdims=True) + EPS)
    xh = x * r
    dg = jnp.sum(dy * xh, axis=0, keepdims=True)
    dxh = dy * g
    dx = r * (dxh - xh * jnp.mean(dxh * xh, axis=-1, keepdims=True))
    return dx, dg


def _zero_pad_rows(v, i, tr):
    rows = i * tr + lax.broadcasted_iota(jnp.int32, (tr, 1), 0)
    return jnp.where(rows >= PAD, v, 0.0)


def _acc8(ref, row, first):
    @pl.when(first)
    def _():
        ref[...] = jnp.zeros_like(ref)
    ref[...] += jnp.broadcast_to(row, ref.shape)


def ffn_up(h, g, wg, wu):
    T, Dm = h.shape
    S, _, F = wg.shape
    tm = _row_tile(T)

    def body(h_ref, g_ref, wg_ref, wu_ref, u_ref, G_ref, U_ref, a_ref):
        @pl.when(pl.program_id(1) == 0)
        def _():
            u_ref[...] = _rms(h_ref[...], g_ref[...]).astype(u_ref.dtype)
        u = u_ref[...]
        G = _dot(u, wg_ref[...])
        U = _dot(u, wu_ref[...])
        G_ref[...] = G.astype(G_ref.dtype)
        U_ref[...] = U.astype(U_ref.dtype)
        a_ref[...] = (G * _sigmoid_fast(G) * U).astype(a_ref.dtype)

    act = jax.ShapeDtypeStruct((S, T, F), bf16)
    wspec = pl.BlockSpec((None, Dm, F), lambda i, s: (s, 0, 0))
    aspec = pl.BlockSpec((None, tm, F), lambda i, s: (s, i, 0))
    return pl.pallas_call(
        body, name="ffn_up", grid=(T // tm, S),
        in_specs=[pl.BlockSpec((tm, Dm), lambda i, s: (i, 0)), pl.BlockSpec((1, Dm), lambda i, s: (0, 0)), wspec, wspec],
        out_specs=[pl.BlockSpec((tm, Dm), lambda i, s: (i, 0)), aspec, aspec, aspec],
        out_shape=[jax.ShapeDtypeStruct((T, Dm), bf16), act, act, act],
        compiler_params=_cparams(dimension_semantics=("arbitrary", "arbitrary")),
    )(h, g, wg, wu)


def norm_proj(h, g, w, tn):
    T, Dm = h.shape
    N = w.shape[1]
    tm = _row_tile(T)

    def body(h_ref, g_ref, w_ref, u_ref, p_ref):
        @pl.when(pl.program_id(1) == 0)
        def _():
            u_ref[...] = _rms(h_ref[...], g_ref[...]).astype(u_ref.dtype)
        p_ref[...] = _dot(u_ref[...], w_ref[...])

    return pl.pallas_call(
        body, name="norm_proj", grid=(T // tm, N // tn),
        in_specs=[pl.BlockSpec((tm, Dm), lambda i, j: (i, 0)), pl.BlockSpec((1, Dm), lambda i, j: (0, 0)),
                  pl.BlockSpec((Dm, tn), lambda i, j: (0, j))],
        out_specs=[pl.BlockSpec((tm, Dm), lambda i, j: (i, 0)), pl.BlockSpec((tm, tn), lambda i, j: (i, j))],
        out_shape=[jax.ShapeDtypeStruct((T, Dm), bf16), jax.ShapeDtypeStruct((T, N), f32)],
        compiler_params=_cparams(dimension_semantics=("arbitrary", "arbitrary")),
    )(h, g, w)


def proj_norm_res(a, w, h, g, coef):
    S, T, F = a.shape
    Dm = w.shape[2]
    tm = _row_tile(T)

    def body(a_ref, w_ref, h_ref, g_ref, y_ref, o_ref, acc):
        s = pl.program_id(1)

        @pl.when(s == 0)
        def _():
            acc[...] = jnp.zeros_like(acc)
        acc[...] += _dot(a_ref[...], w_ref[...])

        @pl.when(s == S - 1)
        def _():
            y = acc[...]
            y_ref[...] = y
            o_ref[...] = h_ref[...] + coef * _rms(y, g_ref[...])

    row = pl.BlockSpec((tm, Dm), lambda i, s: (i, 0))
    return pl.pallas_call(
        body, name="proj_norm_res", grid=(T // tm, S),
        in_specs=[pl.BlockSpec((None, tm, F), lambda i, s: (s, i, 0)), pl.BlockSpec((None, F, Dm), lambda i, s: (s, 0, 0)),
                  row, pl.BlockSpec((1, Dm), lambda i, s: (0, 0))],
        out_specs=[row, row],
        out_shape=[jax.ShapeDtypeStruct((T, Dm), f32), jax.ShapeDtypeStruct((T, Dm), f32)],
        scratch_shapes=[pltpu.VMEM((tm, Dm), f32)],
        compiler_params=_cparams(dimension_semantics=("arbitrary", "arbitrary")),
    )(a, w, h, g)


def post_norm_bwd(y, g, dz, coef):
    T, Dm = y.shape
    tm = _row_tile(T)

    def body(y_ref, g_ref, dz_ref, dy_ref, dg_ref):
        dy, dg = _rms_bwd(y_ref[...], g_ref[...], coef * dz_ref[...])
        dy_ref[...] = _zero_pad_rows(dy, pl.program_id(0), tm).astype(dy_ref.dtype)
        _acc8(dg_ref, dg, pl.program_id(0) == 0)

    row = pl.BlockSpec((tm, Dm), lambda i: (i, 0))
    return pl.pallas_call(
        body, name="post_norm_bwd", grid=(T // tm,),
        in_specs=[row, pl.BlockSpec((1, Dm), lambda i: (0, 0)), row],
        out_specs=[row, pl.BlockSpec((8, Dm), lambda i: (0, 0))],
        out_shape=[jax.ShapeDtypeStruct((T, Dm), bf16), jax.ShapeDtypeStruct((8, Dm), f32)],
        compiler_params=_cparams(dimension_semantics=("arbitrary",)),
    )(y, g, dz)


def ffn_bwd_act(dy, wd, G, U):
    T, Dm = dy.shape
    S, F, _ = wd.shape
    tm = _row_tile(T)

    def body(dy_ref, w_ref, G_ref, U_ref, dG_ref, dU_ref):
        da = _dot_nt(dy_ref[...], w_ref[...])
        Gv = G_ref[...].astype(f32)
        Uv = U_ref[...].astype(f32)
        sg = _sigmoid_fast(Gv)
        dU_ref[...] = (da * Gv * sg).astype(dU_ref.dtype)
        dG_ref[...] = (da * Uv * sg * (1.0 + Gv * (1.0 - sg))).astype(dG_ref.dtype)

    aspec = pl.BlockSpec((None, tm, F), lambda i, s: (s, i, 0))
    act = jax.ShapeDtypeStruct((S, T, F), bf16)
    return pl.pallas_call(
        body, name="ffn_bwd_act", grid=(T // tm, S),
        in_specs=[pl.BlockSpec((tm, Dm), lambda i, s: (i, 0)), pl.BlockSpec((None, F, Dm), lambda i, s: (s, 0, 0)), aspec, aspec],
        out_specs=[aspec, aspec], out_shape=[act, act],
        compiler_params=_cparams(dimension_semantics=("arbitrary", "arbitrary")),
    )(dy, wd, G, U)


def mm_nt_norm_bwd(pairs, h, g, dres, ksplit=1):
    S, T, K = pairs[0][0].shape
    assert S == 1 or ksplit == 1
    steps = S * ksplit
    tk = K // ksplit
    Dm = h.shape[1]
    tm = _row_tile(T)
    n = len(pairs)

    def body(*refs):
        ab = refs[:2 * n]
        h_ref, g_ref, dres_ref, dh_ref, dg_ref, acc = refs[2 * n:]
        i, s = pl.program_id(0), pl.program_id(1)

        @pl.when(s == 0)
        def _():
            acc[...] = jnp.zeros_like(acc)
        for p in range(n):
            acc[...] += _dot_nt(ab[2 * p][...], ab[2 * p + 1][...])

        @pl.when(s == steps - 1)
        def _():
            dx, dg = _rms_bwd(h_ref[...], g_ref[...], acc[...])
            dh_ref[...] = _zero_pad_rows(dres_ref[...] + dx, i, tm)
            _acc8(dg_ref, dg, i == 0)

    row = pl.BlockSpec((tm, Dm), lambda i, s: (i, 0))
    if S > 1:
        amap, bmap = (lambda i, s: (s, i, 0)), (lambda i, s: (s, 0, 0))
    else:
        amap, bmap = (lambda i, s: (0, i, s)), (lambda i, s: (0, 0, s))
    in_specs, args = [], []
    for a, b in pairs:
        in_specs += [pl.BlockSpec((None, tm, tk), amap), pl.BlockSpec((None, Dm, tk), bmap)]
        args += [a, b]
    return pl.pallas_call(
        body, name="mm_nt_norm_bwd", grid=(T // tm, steps),
        in_specs=in_specs + [row, pl.BlockSpec((1, Dm), lambda i, s: (0, 0)), row],
        out_specs=[row, pl.BlockSpec((8, Dm), lambda i, s: (0, 0))],
        out_shape=[jax.ShapeDtypeStruct((T, Dm), f32), jax.ShapeDtypeStruct((8, Dm), f32)],
        scratch_shapes=[pltpu.VMEM((tm, Dm), f32)],
        compiler_params=_cparams(dimension_semantics=("arbitrary", "arbitrary")),
    )(*args, h, g, dres)


def mm_nt(a, b):
    T, K = a.shape
    N = b.shape[0]
    tm = _row_tile(T)

    def body(a_ref, b_ref, o_ref):
        o_ref[...] = _dot_nt(a_ref[...], b_ref[...])

    return pl.pallas_call(
        body, name="mm_nt", grid=(T // tm,),
        in_specs=[pl.BlockSpec((tm, K), lambda i: (i, 0)), pl.BlockSpec((N, K), lambda i: (0, 0))],
        out_specs=pl.BlockSpec((tm, N), lambda i: (i, 0)),
        out_shape=jax.ShapeDtypeStruct((T, N), f32),
        compiler_params=_cparams(dimension_semantics=("arbitrary",)),
    )(a, b)


def mm_tn(a, b, tn):
    Sa, T, M = a.shape
    Sb, _, N = b.shape
    S = max(Sa, Sb)
    tk = 13 * BLK if T % (13 * BLK) == 0 else _row_tile(T)
    nk = T // tk

    def body(a_ref, b_ref, o_ref, acc):
        k = pl.program_id(2)

        @pl.when(k == 0)
        def _():
            acc[...] = jnp.zeros_like(acc)
        acc[...] += _dot_tn(a_ref[...], b_ref[...])

        @pl.when(k == nk - 1)
        def _():
            o_ref[...] = acc[...].astype(o_ref.dtype)

    return pl.pallas_call(
        body, name="mm_tn", grid=(S, N // tn, nk),
        in_specs=[pl.BlockSpec((None, tk, M), (lambda s, j, k: (s, k, 0)) if Sa > 1 else (lambda s, j, k: (0, k, 0))),
                  pl.BlockSpec((None, tk, tn), (lambda s, j, k: (s, k, j)) if Sb > 1 else (lambda s, j, k: (0, k, j)))],
        out_specs=pl.BlockSpec((None, M, tn), lambda s, j, k: (s, 0, j)),
        out_shape=jax.ShapeDtypeStruct((S, M, N), bf16),
        scratch_shapes=[pltpu.VMEM((M, tn), f32)],
        compiler_params=_cparams(dimension_semantics=("arbitrary", "arbitrary", "arbitrary")),
    )(a, b)


def adamw(w, g, m, v):
    R, C = w.shape
    tr = 512 if R % 512 == 0 else (256 if R % 256 == 0 else R)
    c1 = np.float32(1.0 - ADAM_B1 ** ADAM_STEP)
    c2 = np.float32(1.0 - ADAM_B2 ** ADAM_STEP)

    def body(w_ref, g_ref, m_ref, v_ref, d_ref, mo_ref, vo_ref):
        gv = g_ref[...]
        mn = ADAM_B1 * m_ref[...] + (1.0 - ADAM_B1) * gv
        vn = ADAM_B2 * v_ref[...] + (1.0 - ADAM_B2) * (gv * gv)
        mo_ref[...] = mn
        vo_ref[...] = vn
        d_ref[...] = -ADAM_LR * ((mn / c1) / (jnp.sqrt(vn / c2) + ADAM_EPS) + ADAM_WD * w_ref[...])

    spec = pl.BlockSpec((tr, C), lambda i: (i, 0))
    sh = jax.ShapeDtypeStruct((R, C), f32)
    return pl.pallas_call(
        body, name="adamw", grid=(R // tr,), in_specs=[spec] * 4, out_specs=[spec] * 3, out_shape=[sh] * 3,
        compiler_params=_cparams(dimension_semantics=("arbitrary",)),
    )(w, g, m, v)


_RET_LOG_GAMMA = [float(v) for v in np.log1p(-np.exp2(-5.0 - np.arange(N_HEAD_AB, dtype=np.float32))).astype(np.float32)]


def rot_tables(T):
    pos = jnp.arange(T, dtype=f32) - float(PAD)
    inv_freq = 1.0 / (10000.0 ** jnp.linspace(0.0, 1.0, DH_AB // 2, dtype=f32))
    ang = pos[:, None] * inv_freq[None, :]
    cos, sin = jnp.cos(ang), jnp.sin(ang)
    return jnp.repeat(cos, 2, axis=1), jnp.stack([-sin, sin], axis=-1).reshape(T, DH_AB)


def _swap_pairs(x):
    lane = lax.broadcasted_iota(jnp.int32, x.shape, 1)
    return jnp.where(lane % 2 == 0, pltpu.roll(x, x.shape[1] - 1, 1), pltpu.roll(x, 1, 1))


def _rot(x, c, s):
    return x * c + _swap_pairs(x) * s


def _rot_bwd(d, c, s):
    return d * c + _swap_pairs(d * s)


def _ret_mats(lg):
    i = lax.broadcasted_iota(jnp.int32, (BLK, BLK), 0).astype(f32)
    j = lax.broadcasted_iota(jnp.int32, (BLK, BLK), 1).astype(f32)
    diff = i - j
    decay = jnp.where(diff >= 0, jnp.exp(jnp.maximum(diff, 0.0) * lg), 0.0)
    xi = jnp.exp((i + 1.0) * lg)
    zeta = jnp.exp((BLK - 1.0 - i) * lg)
    return decay, xi, zeta, float(np.exp(np.float32(BLK * lg)))


def _ret_head(q_ref, k_ref, v_ref, cos, sin, h, Sp):
    sl = slice(DH_AB * h, DH_AB * (h + 1))
    decay, xi, zeta, gc = _ret_mats(_RET_LOG_GAMMA[h])
    q = _rot(q_ref[:, sl], cos, sin)
    k = _rot(k_ref[:, sl], cos, sin) * DH_AB ** -0.5
    v = v_ref[:, sl]
    P = _dot_nt(q, k) * decay
    ret = _dot(P, v) + _dot(q * xi, Sp)
    return sl, q, k, v, P, ret, decay, xi, zeta, gc


def _proj_spec(rows, width, col, rev=None):
    if rev is None:
        return pl.BlockSpec((rows, width), lambda n: (n, col))
    return pl.BlockSpec((rows, width), lambda n: (rev - n, col))


def ret_fwd(proj, cos, sin):
    T = proj.shape[0]
    NC = T // BLK
    W = N_HEAD_AB * DH_AB

    def body(q_ref, k_ref, v_ref, g_ref, cos_ref, sin_ref, o_ref, sall_ref, S):
        @pl.when(pl.program_id(0) == 0)
        def _():
            S[...] = jnp.zeros_like(S)
        cos_v, sin_v = cos_ref[...], sin_ref[...]
        for h in range(N_HEAD_AB):
            Sp = S[h]
            sall_ref[0, h] = Sp
            sl, q, k, v, P, ret, decay, xi, zeta, gc = _ret_head(q_ref, k_ref, v_ref, cos_v, sin_v, h, Sp)
            S[h] = Sp * gc + _dot_tn(k * zeta, v)
            mu = jnp.mean(ret, axis=-1, keepdims=True)
            cen = ret - mu
            y = cen * lax.rsqrt(jnp.mean(cen * cen, axis=-1, keepdims=True) + EPS)
            o_ref[:, sl] = (y * _silu(g_ref[:, sl])).astype(o_ref.dtype)

    tab = pl.BlockSpec((BLK, DH_AB), lambda n: (n, 0))
    return pl.pallas_call(
        body, name="ret_fwd", grid=(NC,),
        in_specs=[_proj_spec(BLK, W, 0), _proj_spec(BLK, W, 1), _proj_spec(BLK, W, 2), _proj_spec(BLK, W, 3), tab, tab],
        out_specs=[pl.BlockSpec((BLK, W), lambda n: (n, 0)), pl.BlockSpec((1, N_HEAD_AB, DH_AB, DH_AB), lambda n: (n, 0, 0, 0))],
        out_shape=[jax.ShapeDtypeStruct((T, W), bf16), jax.ShapeDtypeStruct((NC, N_HEAD_AB, DH_AB, DH_AB), f32)],
        scratch_shapes=[pltpu.VMEM((N_HEAD_AB, DH_AB, DH_AB), f32)],
        compiler_params=_cparams(dimension_semantics=("arbitrary",)),
    )(proj, proj, proj, proj, cos, sin)


def ret_bwd(proj, cos, sin, sall, dmix):
    T = proj.shape[0]
    NC = T // BLK
    W = N_HEAD_AB * DH_AB
    L = NC - 1

    def body(q_ref, k_ref, v_ref, g_ref, cos_ref, sin_ref, sall_ref, do_ref, dq_ref, dk_ref, dv_ref, dg_ref, dS):
        @pl.when(pl.program_id(0) == 0)
        def _():
            dS[...] = jnp.zeros_like(dS)
        cos_v, sin_v = cos_ref[...], sin_ref[...]
        for h in range(N_HEAD_AB):
            Sp = sall_ref[0, h]
            sl, q, k, v, P, ret, decay, xi, zeta, gc = _ret_head(q_ref, k_ref, v_ref, cos_v, sin_v, h, Sp)
            mu = jnp.mean(ret, axis=-1, keepdims=True)
            cen = ret - mu
            r = lax.rsqrt(jnp.mean(cen * cen, axis=-1, keepdims=True) + EPS)
            y = cen * r
            gate = g_ref[:, sl]
            sg = _sigmoid(gate)
            dout = do_ref[:, sl]
            dg_ref[:, sl] = dout * y * (sg * (1.0 + gate * (1.0 - sg)))
            dy = dout * (gate * sg)
            dO = r * (dy - jnp.mean(dy, axis=-1, keepdims=True) - y * jnp.mean(dy * y, axis=-1, keepdims=True))
            dSn = dS[h]
            dv_ref[:, sl] = _dot_tn(P, dO) + _dot(k * zeta, dSn)
            dP = _dot_nt(dO, v) * decay
            dq = _dot(dP, k) + _dot_nt(dO, Sp) * xi
            dk = _dot_tn(dP, q) + _dot_nt(v, dSn) * zeta
            dS[h] = dSn * gc + _dot_tn(q * xi, dO)
            dq_ref[:, sl] = _rot_bwd(dq, cos_v, sin_v)
            dk_ref[:, sl] = _rot_bwd(dk * DH_AB ** -0.5, cos_v, sin_v)

    tab = pl.BlockSpec((BLK, DH_AB), lambda n: (L - n, 0))
    out = pl.BlockSpec((BLK, W), lambda n: (L - n, 0))
    sh = jax.ShapeDtypeStruct((T, W), f32)
    return pl.pallas_call(
        body, name="ret_bwd", grid=(NC,),
        in_specs=[_proj_spec(BLK, W, 0, L), _proj_spec(BLK, W, 1, L), _proj_spec(BLK, W, 2, L), _proj_spec(BLK, W, 3, L), tab, tab,
                  pl.BlockSpec((1, N_HEAD_AB, DH_AB, DH_AB), lambda n: (L - n, 0, 0, 0)), _proj_spec(BLK, W, 0, L)],
        out_specs=[out] * 4, out_shape=[sh] * 4,
        scratch_shapes=[pltpu.VMEM((N_HEAD_AB, DH_AB, DH_AB), f32)],
        compiler_params=_cparams(dimension_semantics=("arbitrary",)),
    )(proj, proj, proj, proj, cos, sin, sall, dmix)


HALO = 8
GDN_K = 4
W_AB = N_HEAD_AB * DH_AB


def _gdn_rowwise(cq, ck, cv, gblk, alog, dtb, rmask):
    def l2n(x):
        return [x[:, DH_AB * h:DH_AB * (h + 1)] for h in range(N_HEAD_AB)]

    def norm(x):
        return x * lax.rsqrt(jnp.sum(x * x, axis=-1, keepdims=True) + EPS)

    qs = [norm(x) for x in l2n(_silu(cq))]
    ks = [norm(x) for x in l2n(_silu(ck))]
    lane = lax.broadcasted_iota(jnp.int32, gblk.shape, 1)
    beta = _sigmoid(gblk)
    g = -jnp.exp(alog) * _softplus(gblk + dtb)
    gates = jnp.where(lane < N_HEAD_AB, beta, jnp.where(lane < 2 * N_HEAD_AB, g, 0.0)) * rmask
    return qs, ks, _silu(cv), gates


def _row_mask(i, tr):
    rows = i * tr + lax.broadcasted_iota(jnp.int32, (tr, 1), 0)
    return (rows >= PAD).astype(f32)


def _conv_specs(tr, cols, nt, nxt=False):
    tiles = [pl.BlockSpec((tr, W_AB), functools.partial(lambda i, c: (i, c), c=c)) for c in cols]
    r = tr // HALO
    if nxt:
        halos = [pl.BlockSpec((HALO, W_AB), functools.partial(lambda i, c: (jnp.minimum((i + 1) * r, nt * r - 1), c), c=c)) for c in cols]
    else:
        halos = [pl.BlockSpec((HALO, W_AB), functools.partial(lambda i, c: (jnp.maximum(i * r - 1, 0), c), c=c)) for c in cols]
    return tiles, halos


def gdn_prep_fwd(proj, conv_w, alog, dtb):
    T = proj.shape[0]
    tr = BLK
    NT = T // tr

    def body(xq, xk, xv, hq, hk, hv, gb_ref, w_ref, al_ref, dt_ref, cq_o, ck_o, cv_o, q_o, k_o, v_o, gates_o, buf):
        i = pl.program_id(0)
        cs = []
        for p, (x_ref, h_ref, c_o) in enumerate(((xq, hq, cq_o), (xk, hk, ck_o), (xv, hv, cv_o))):
            buf[0:HALO, :] = jnp.where(i > 0, h_ref[...], 0.0)
            buf[HALO:, :] = x_ref[...]
            c = jnp.zeros((tr, W_AB), f32)
            for k in range(GDN_K):
                c = c + w_ref[k:k + 1, W_AB * p:W_AB * (p + 1)] * buf[pl.ds(HALO - GDN_K + 1 + k, tr), :]
            c_o[...] = c
            cs.append(c)
        qs, ks, v, gates = _gdn_rowwise(cs[0], cs[1], cs[2], gb_ref[...], al_ref[...], dt_ref[...], _row_mask(i, tr))
        for h in range(N_HEAD_AB):
            q_o[:, DH_AB * h:DH_AB * (h + 1)] = qs[h]
            k_o[:, DH_AB * h:DH_AB * (h + 1)] = ks[h]
        v_o[...] = v
        gates_o[...] = gates

    tiles, halos = _conv_specs(tr, (4, 5, 6), NT)
    vec = pl.BlockSpec((1, BLK), lambda i: (0, 0))
    wide = pl.BlockSpec((tr, W_AB), lambda i: (i, 0))
    sh = jax.ShapeDtypeStruct((T, W_AB), f32)
    return pl.pallas_call(
        body, name="gdn_prep_fwd", grid=(NT,),
        in_specs=tiles + halos + [pl.BlockSpec((tr, BLK), lambda i: (i, AB_IN_P // BLK - 1)),
                                  pl.BlockSpec((GDN_K, 3 * W_AB), lambda i: (0, 0)), vec, vec],
        out_specs=[wide] * 6 + [pl.BlockSpec((tr, BLK), lambda i: (i, 0))],
        out_shape=[sh] * 6 + [jax.ShapeDtypeStruct((T, BLK), f32)],
        scratch_shapes=[pltpu.VMEM((tr + HALO, W_AB), f32)],
        compiler_params=_cparams(dimension_semantics=("arbitrary",)),
    )(proj, proj, proj, proj, proj, proj, proj, conv_w, alog, dtb)


def gdn_prep_bwd(cq, ck, cv, proj, alog, dtb, dq, dk, dv, dgates):
    T = cq.shape[0]
    tr = BLK
    NT = T // tr

    def body(cq_r, ck_r, cv_r, gb_ref, al_ref, dt_ref, dq_r, dk_r, dv_r, dg_r, dcq_o, dck_o, dcv_o, dgb_o, dal_o, ddt_o):
        i = pl.program_id(0)
        mask = _row_mask(i, tr)
        _, vjp = jax.vjp(lambda a, b, c, d, e, f: _gdn_rowwise(a, b, c, d, e, f, mask),
                         cq_r[...], ck_r[...], cv_r[...], gb_ref[...], al_ref[...], dt_ref[...])
        heads = lambda r: [r[:, DH_AB * h:DH_AB * (h + 1)] for h in range(N_HEAD_AB)]
        dcq, dck, dcv, dgb, dal, ddt = vjp((heads(dq_r), heads(dk_r), dv_r[...], dg_r[...]))
        dcq_o[...] = dcq
        dck_o[...] = dck
        dcv_o[...] = dcv
        dgb_o[...] = dgb
        _acc8(dal_o, dal, i == 0)
        _acc8(ddt_o, ddt, i == 0)

    vec = pl.BlockSpec((1, BLK), lambda i: (0, 0))
    wide = pl.BlockSpec((tr, W_AB), lambda i: (i, 0))
    narrow = pl.BlockSpec((tr, BLK), lambda i: (i, 0))
    acc = pl.BlockSpec((8, BLK), lambda i: (0, 0))
    sh = jax.ShapeDtypeStruct((T, W_AB), f32)
    return pl.pallas_call(
        body, name="gdn_prep_bwd", grid=(NT,),
        in_specs=[wide] * 3 + [pl.BlockSpec((tr, BLK), lambda i: (i, AB_IN_P // BLK - 1)), vec, vec] + [wide] * 3 + [narrow],
        out_specs=[wide] * 3 + [narrow, acc, acc],
        out_shape=[sh] * 3 + [jax.ShapeDtypeStruct((T, BLK), f32)] + [jax.ShapeDtypeStruct((8, BLK), f32)] * 2,
        compiler_params=_cparams(dimension_semantics=("arbitrary",)),
    )(cq, ck, cv, proj, alog, dtb, dq, dk, dv, dgates)


def gdn_conv_bwd(dcq, dck, dcv, proj, conv_w):
    T = dcq.shape[0]
    tr = BLK
    NT = T // tr

    def body(dq_r, dk_r, dv_r, nq, nk, nv, xq, xk, xv, hq, hk, hv, w_ref, dxq_o, dxk_o, dxv_o, dw_o, bufd, bufx):
        i = pl.program_id(0)

        @pl.when(i == 0)
        def _():
            dw_o[...] = jnp.zeros_like(dw_o)
        parts = ((dq_r, nq, xq, hq, dxq_o), (dk_r, nk, xk, hk, dxk_o), (dv_r, nv, xv, hv, dxv_o))
        for p, (dc_r, n_r, x_r, h_r, dx_o) in enumerate(parts):
            dc = dc_r[...]
            bufd[0:tr, :] = dc
            bufd[tr:, :] = jnp.where(i < NT - 1, n_r[...], 0.0)
            bufx[0:HALO, :] = jnp.where(i > 0, h_r[...], 0.0)
            bufx[HALO:, :] = x_r[...]
            dx = jnp.zeros((tr, W_AB), f32)
            rows = []
            for k in range(GDN_K):
                dx = dx + w_ref[k:k + 1, W_AB * p:W_AB * (p + 1)] * bufd[pl.ds(GDN_K - 1 - k, tr), :]
                rows.append(jnp.sum(dc * bufx[pl.ds(HALO - GDN_K + 1 + k, tr), :], axis=0, keepdims=True))
            dx_o[...] = dx
            dw_o[:, W_AB * p:W_AB * (p + 1)] += jnp.concatenate(rows + [jnp.zeros((8 - GDN_K, W_AB), f32)], axis=0)

    wide = pl.BlockSpec((tr, W_AB), lambda i: (i, 0))
    r = tr // HALO
    nxt = pl.BlockSpec((HALO, W_AB), lambda i: (jnp.minimum((i + 1) * r, NT * r - 1), 0))
    tiles, halos = _conv_specs(tr, (4, 5, 6), NT)
    sh = jax.ShapeDtypeStruct((T, W_AB), f32)
    return pl.pallas_call(
        body, name="gdn_conv_bwd", grid=(NT,),
        in_specs=[wide] * 3 + [nxt] * 3 + tiles + halos + [pl.BlockSpec((GDN_K, 3 * W_AB), lambda i: (0, 0))],
        out_specs=[wide] * 3 + [pl.BlockSpec((8, 3 * W_AB), lambda i: (0, 0))],
        out_shape=[sh] * 3 + [jax.ShapeDtypeStruct((8, 3 * W_AB), f32)],
        scratch_shapes=[pltpu.VMEM((tr + HALO, W_AB), f32), pltpu.VMEM((tr + HALO, W_AB), f32)],
        compiler_params=_cparams(dimension_semantics=("arbitrary",)),
    )(dcq, dck, dcv, dcq, dck, dcv, proj, proj, proj, proj, proj, proj, conv_w)


def _tri_sum(x, upper):
    n = x.shape[0]
    r = lax.broadcasted_iota(jnp.int32, (n, n), 0)
    c = lax.broadcasted_iota(jnp.int32, (n, n), 1)
    tri = ((r <= c) if upper else (r >= c)).astype(bf16)
    hi = x.astype(bf16)
    lo = (x - hi.astype(f32)).astype(bf16)
    return jnp.dot(tri, hi, preferred_element_type=f32) + jnp.dot(tri, lo, preferred_element_type=f32)


@jax.custom_vjp
def _cumsum_rows(x):
    return _tri_sum(x, False)


_cumsum_rows.defvjp(lambda x: (_tri_sum(x, False), None), lambda _, g: (_tri_sum(g, True),))


@jax.custom_vjp
def _unit_lower_inv(a):
    n = a.shape[0]
    eye = (lax.broadcasted_iota(jnp.int32, (n, n), 0) == lax.broadcasted_iota(jnp.int32, (n, n), 1)).astype(f32)
    b = -a
    x = eye + b
    p = b
    for _ in range(int(np.log2(n)) - 1):
        p = _dot(p, p)
        x = x + _dot(x, p)
    return x


def _unit_lower_inv_fwd(a):
    t = _unit_lower_inv(a)
    return t, t


def _unit_lower_inv_bwd(t, dt):
    return (-_dot_nt(_dot_tn(t, dt), t),)


_unit_lower_inv.defvjp(_unit_lower_inv_fwd, _unit_lower_inv_bwd)


def _gdn_chunk(qs, ks, vs, gates, zs, onorm, Ss):
    C = gates.shape[0]
    ri = lax.broadcasted_iota(jnp.int32, (C, C), 0)
    ci = lax.broadcasted_iota(jnp.int32, (C, C), 1)
    incl, strict = ri >= ci, ri > ci
    gcum = _cumsum_rows(gates)
    gcum_t = gcum.T
    lane = lax.broadcasted_iota(jnp.int32, gates.shape, 1)
    sub = lax.broadcasted_iota(jnp.int32, gcum_t.shape, 0)
    last = lax.broadcasted_iota(jnp.int32, (C, 1), 0) == C - 1
    outs, nxt = [], []
    for h in range(N_HEAD_AB):
        bcol = jnp.sum(jnp.where(lane == h, gates, 0.0), axis=1, keepdims=True)
        gcol = jnp.sum(jnp.where(lane == N_HEAD_AB + h, gcum, 0.0), axis=1, keepdims=True)
        grow = jnp.sum(jnp.where(sub == N_HEAD_AB + h, gcum_t, 0.0), axis=0, keepdims=True)
        gl = jnp.sum(jnp.where(last, gcol, 0.0), axis=0, keepdims=True)
        decay = jnp.where(incl, jnp.exp(jnp.where(incl, gcol - grow, 0.0)), 0.0)
        q = qs[h] * DH_AB ** -0.5
        k, v, S = ks[h], vs[h], Ss[h]
        kb = k * bcol
        a = jnp.where(strict, _dot_nt(kb, k) * decay, 0.0)
        t = _unit_lower_inv(a)
        eg = jnp.exp(gcol)
        u = _dot(t, v * bcol)
        w = _dot(t, kb * eg)
        qk = jnp.where(incl, _dot_nt(q, k) * decay, 0.0)
        v_new = u - _dot(w, S)
        o = _dot(q * eg, S) + _dot(qk, v_new)
        nxt.append(S * jnp.exp(gl) + _dot_tn(k * jnp.exp(gl - gcol), v_new))
        outs.append(_rms(o, onorm) * _silu(zs[h]))
    return outs, nxt


def _heads(ref, r0=None):
    rows = slice(None) if r0 is None else slice(r0, r0 + GDN_C)
    return [ref[rows, DH_AB * h:DH_AB * (h + 1)] for h in range(N_HEAD_AB)]


def _gdn_per_step(n_chunks):
    return next(p for p in (5, 2, 1) if n_chunks % p == 0)


def gdn_chunk_fwd(q, k, v, gates, proj, onorm):
    T = q.shape[0]
    P = _gdn_per_step(q.shape[0] // GDN_C)
    C = GDN_C * P
    NC = T // GDN_C

    def body(q_r, k_r, v_r, g_r, z_r, on_r, o_ref, sall_ref, S):
        @pl.when(pl.program_id(0) == 0)
        def _():
            S[...] = jnp.zeros_like(S)
        Ss = [S[h] for h in range(N_HEAD_AB)]
        for j in range(P):
            r0 = GDN_C * j
            for h in range(N_HEAD_AB):
                sall_ref[j, h] = Ss[h]
            outs, Ss = _gdn_chunk(_heads(q_r, r0), _heads(k_r, r0), _heads(v_r, r0), g_r[r0:r0 + GDN_C, :], _heads(z_r, r0), on_r[...], Ss)
            for h in range(N_HEAD_AB):
                o_ref[r0:r0 + GDN_C, DH_AB * h:DH_AB * (h + 1)] = outs[h].astype(o_ref.dtype)
        for h in range(N_HEAD_AB):
            S[h] = Ss[h]

    wide = pl.BlockSpec((C, W_AB), lambda n: (n, 0))
    return pl.pallas_call(
        body, name="gdn_chunk_fwd", grid=(NC // P,),
        in_specs=[wide] * 3 + [pl.BlockSpec((C, BLK), lambda n: (n, 0)), pl.BlockSpec((C, W_AB), lambda n: (n, 7)),
                               pl.BlockSpec((1, DH_AB), lambda n: (0, 0))],
        out_specs=[wide, pl.BlockSpec((P, N_HEAD_AB, DH_AB, DH_AB), lambda n: (n, 0, 0, 0))],
        out_shape=[jax.ShapeDtypeStruct((T, W_AB), bf16), jax.ShapeDtypeStruct((NC, N_HEAD_AB, DH_AB, DH_AB), f32)],
        scratch_shapes=[pltpu.VMEM((N_HEAD_AB, DH_AB, DH_AB), f32)],
        compiler_params=_cparams(dimension_semantics=("arbitrary",)),
    )(q, k, v, gates, proj, onorm)


def gdn_chunk_bwd(q, k, v, gates, proj, onorm, sall, dmix):
    T = q.shape[0]
    P = _gdn_per_step(q.shape[0] // GDN_C)
    C = GDN_C * P
    NC = T // GDN_C
    L = NC // P - 1

    def body(q_r, k_r, v_r, g_r, z_r, on_r, sall_r, do_r, dq_o, dk_o, dv_o, dz_o, dg_o, don_o, dS):
        @pl.when(pl.program_id(0) == 0)
        def _():
            dS[...] = jnp.zeros_like(dS)
        dSs = [dS[h] for h in range(N_HEAD_AB)]
        don_sum = jnp.zeros((1, DH_AB), f32)
        for j in reversed(range(P)):
            r0 = GDN_C * j
            Ss = [sall_r[j, h] for h in range(N_HEAD_AB)]
            _, vjp = jax.vjp(_gdn_chunk, _heads(q_r, r0), _heads(k_r, r0), _heads(v_r, r0), g_r[r0:r0 + GDN_C, :], _heads(z_r, r0),
                             on_r[...], Ss)
            dqs, dks, dvs, dg, dzs, don, dSs = vjp((_heads(do_r, r0), dSs))
            for h in range(N_HEAD_AB):
                sl = slice(DH_AB * h, DH_AB * (h + 1))
                dq_o[r0:r0 + GDN_C, sl] = dqs[h]
                dk_o[r0:r0 + GDN_C, sl] = dks[h]
                dv_o[r0:r0 + GDN_C, sl] = dvs[h]
                dz_o[r0:r0 + GDN_C, sl] = dzs[h]
            dg_o[r0:r0 + GDN_C, :] = dg
            don_sum = don_sum + don
        for h in range(N_HEAD_AB):
            dS[h] = dSs[h]
        _acc8(don_o, don_sum, pl.program_id(0) == 0)

    wide = pl.BlockSpec((C, W_AB), lambda n: (L - n, 0))
    sh = jax.ShapeDtypeStruct((T, W_AB), f32)
    return pl.pallas_call(
        body, name="gdn_chunk_bwd", grid=(NC // P,),
        in_specs=[wide] * 3 + [pl.BlockSpec((C, BLK), lambda n: (L - n, 0)), pl.BlockSpec((C, W_AB), lambda n: (L - n, 7)),
                               pl.BlockSpec((1, DH_AB), lambda n: (0, 0)),
                               pl.BlockSpec((P, N_HEAD_AB, DH_AB, DH_AB), lambda n: (L - n, 0, 0, 0)),
                               pl.BlockSpec((C, W_AB), lambda n: (L - n, 1))],
        out_specs=[wide] * 4 + [pl.BlockSpec((C, BLK), lambda n: (L - n, 0)), pl.BlockSpec((8, DH_AB), lambda n: (0, 0))],
        out_shape=[sh] * 4 + [jax.ShapeDtypeStruct((T, BLK), f32), jax.ShapeDtypeStruct((8, DH_AB), f32)],
        scratch_shapes=[pltpu.VMEM((N_HEAD_AB, DH_AB, DH_AB), f32)],
        compiler_params=_cparams(dimension_semantics=("arbitrary",)),
    )(q, k, v, gates, proj, onorm, sall, dmix)


DH_CD = 64
SWA_G = 4
SWA_KV = 2
W_CD = 512


def _swa_block(q_ref, km, kp, kc, vm, vp, vc, sinks, g, n):
    scale = DH_CD ** -0.5
    ks = slice(DH_CD * g, DH_CD * (g + 1))
    Q = jnp.concatenate([q_ref[:, DH_CD * (SWA_G * g + j):DH_CD * (SWA_G * g + j + 1)] for j in range(SWA_G)], axis=0) * scale
    K3 = jnp.concatenate([km[:, ks], kp[:, ks], kc[:, ks]], axis=0)
    V3 = jnp.concatenate([vm[:, ks], vp[:, ks], vc[:, ks]], axis=0)
    s = _dot_nt(Q, K3)
    shp = s.shape
    row = lax.broadcasted_iota(jnp.int32, shp, 0)
    col = lax.broadcasted_iota(jnp.int32, shp, 1)
    i, part, j = row % BLK, col // BLK, col % BLK
    meta = (part == 0) & (j >= PAD) & ((j <= i) | (n > 0))
    prev = (part == 1) & (j > i) & (n >= 2)
    cur = (part == 2) & (j <= i) & (n >= 1)
    valid = meta | prev | cur
    grp = lax.broadcasted_iota(jnp.int32, (shp[0], 1), 0) // BLK
    sink = jnp.zeros((shp[0], 1), f32)
    for jj in range(SWA_G):
        sink = jnp.where(grp == jj, sinks[SWA_G * g + jj], sink)
    m = jnp.maximum(jnp.max(jnp.where(valid, s, NEG), axis=1, keepdims=True), sink)
    p = jnp.where(valid, jnp.exp(jnp.where(valid, s - m, 0.0)), 0.0)
    es = jnp.exp(sink - m)
    denom = jnp.sum(p, axis=1, keepdims=True) + es
    return Q, K3, V3, p / denom, es / denom, grp


def _swa_in_specs(rev=None):
    row = (lambda n: n) if rev is None else (lambda n: rev - n)
    kcol, vcol = 512 // BLK, 640 // BLK
    specs = [pl.BlockSpec((BLK, W_CD), lambda n: (row(n), 0))]
    for col in (kcol, vcol):
        specs += [pl.BlockSpec((BLK, BLK), functools.partial(lambda n, c: (0, c), c=col)),
                  pl.BlockSpec((BLK, BLK), functools.partial(lambda n, c: (jnp.maximum(row(n) - 1, 0), c), c=col)),
                  pl.BlockSpec((BLK, BLK), functools.partial(lambda n, c: (row(n), c), c=col))]
    return specs + [pl.BlockSpec(memory_space=pltpu.SMEM)]


def swa_fwd(proj, sinks):
    T = proj.shape[0]
    NB = T // BLK

    def body(q_ref, km, kp, kc, vm, vp, vc, sinks_ref, o_ref):
        n = pl.program_id(0)
        for g in range(SWA_KV):
            Q, K3, V3, pn, ps, grp = _swa_block(q_ref, km, kp, kc, vm, vp, vc, sinks_ref, g, n)
            o = _dot(pn, V3)
            for j in range(SWA_G):
                hd = SWA_G * g + j
                o_ref[:, DH_CD * hd:DH_CD * (hd + 1)] = o[BLK * j:BLK * (j + 1), :]

    return pl.pallas_call(
        body, name="swa_fwd", grid=(NB,), in_specs=_swa_in_specs(),
        out_specs=pl.BlockSpec((BLK, W_CD), lambda n: (n, 0)), out_shape=jax.ShapeDtypeStruct((T, W_CD), f32),
        compiler_params=_cparams(dimension_semantics=("arbitrary",)),
    )(proj, proj, proj, proj, proj, proj, proj, sinks)


def swa_bwd(proj, sinks, dmix):
    T = proj.shape[0]
    NB = T // BLK
    KV = 2 * SWA_KV * DH_CD

    def body(q_ref, km, kp, kc, vm, vp, vc, sinks_ref, do_ref, dq_ref, cur_ref, prev_ref, meta_ref, ds_ref):
        n = pl.program_id(0)

        @pl.when(n == 0)
        def _():
            meta_ref[...] = jnp.zeros_like(meta_ref)
            ds_ref[...] = jnp.zeros_like(ds_ref)
        rows = []
        for g in range(SWA_KV):
            Q, K3, V3, pn, ps, grp = _swa_block(q_ref, km, kp, kc, vm, vp, vc, sinks_ref, g, n)
            dO = jnp.concatenate([do_ref[:, DH_CD * (SWA_G * g + j):DH_CD * (SWA_G * g + j + 1)] for j in range(SWA_G)], axis=0)
            dP = _dot_nt(dO, V3)
            delta = jnp.sum(pn * dP, axis=1, keepdims=True)
            dS = pn * (dP - delta)
            dQ = _dot(dS, K3) * DH_CD ** -0.5
            dK3 = _dot_tn(dS, Q)
            dV3 = _dot_tn(pn, dO)
            dsk = -ps * delta
            for j in range(SWA_G):
                hd = SWA_G * g + j
                dq_ref[:, DH_CD * hd:DH_CD * (hd + 1)] = dQ[BLK * j:BLK * (j + 1), :]
                rows.append(jnp.broadcast_to(jnp.sum(jnp.where(grp == j, dsk, 0.0), axis=0, keepdims=True), (1, BLK)))
            kcols = slice(DH_CD * g, DH_CD * (g + 1))
            vcols = slice(SWA_KV * DH_CD + DH_CD * g, SWA_KV * DH_CD + DH_CD * (g + 1))
            meta_ref[:, kcols] += dK3[0:BLK]
            meta_ref[:, vcols] += dV3[0:BLK]
            prev_ref[:, kcols] = dK3[BLK:2 * BLK]
            prev_ref[:, vcols] = dV3[BLK:2 * BLK]
            cur_ref[:, kcols] = dK3[2 * BLK:]
            cur_ref[:, vcols] = dV3[2 * BLK:]
        ds_ref[...] += jnp.concatenate(rows, axis=0)

    kv = pl.BlockSpec((BLK, KV), lambda n: (n, 0))
    return pl.pallas_call(
        body, name="swa_bwd", grid=(NB,),
        in_specs=_swa_in_specs() + [pl.BlockSpec((BLK, W_CD), lambda n: (n, 0))],
        out_specs=[pl.BlockSpec((BLK, W_CD), lambda n: (n, 0)), kv, kv, pl.BlockSpec((BLK, KV), lambda n: (0, 0)),
                   pl.BlockSpec((8, BLK), lambda n: (0, 0))],
        out_shape=[jax.ShapeDtypeStruct((T, W_CD), f32), jax.ShapeDtypeStruct((T, KV), f32), jax.ShapeDtypeStruct((T, KV), f32),
                   jax.ShapeDtypeStruct((BLK, KV), f32), jax.ShapeDtypeStruct((8, BLK), f32)],
        compiler_params=_cparams(dimension_semantics=("arbitrary",)),
    )(proj, proj, proj, proj, proj, proj, proj, sinks, dmix)


SB_PAIR = 2


def _sb_consts():
    r = lax.broadcasted_iota(jnp.int32, (BLK, BLK), 0)
    c = lax.broadcasted_iota(jnp.int32, (BLK, BLK), 1)
    return r, c, (r > c).astype(bf16), (r >= c).astype(bf16)


def _sb_block(q, kb, n, m, r, c):
    z = _dot_nt(q, kb)
    valid = ((m * BLK + c) < (n * BLK + r)) & ((m * BLK + c) >= PAD)
    sp = _softplus(z)
    return z, valid, jnp.where(valid, -sp, 0.0), sp


def _sb_specs(T):
    qcol, kcol, vcol = 768 // BLK, 1280 // BLK, 1792 // BLK
    return [pl.BlockSpec((BLK, BLK), lambda hp, n: (n, qcol + hp)),
            pl.BlockSpec((T, BLK), lambda hp, n: (0, kcol + hp)),
            pl.BlockSpec((T, BLK), lambda hp, n: (0, vcol + hp))]


def sb_fwd(proj):
    T = proj.shape[0]
    NB = T // BLK

    def body(q_ref, k_ref, v_ref, o_ref):
        n = pl.program_id(1)
        r, c, m_gt, _ = _sb_consts()
        heads = [slice(DH_CD * hh, DH_CD * (hh + 1)) for hh in range(SB_PAIR)]
        qs = [q_ref[:, cols] * DH_CD ** -0.5 for cols in heads]

        def cond(carry):
            m, runs, _ = carry
            return jnp.logical_and(m >= 0, jnp.max(jnp.maximum(runs[0], runs[1])) > SB_EXIT)

        def step(carry):
            m, runs, accs = carry
            off = pl.multiple_of(m * BLK, BLK)
            new_runs, new_accs = [], []
            for hh, cols in enumerate(heads):
                kb = k_ref[pl.ds(off, BLK), cols]
                vb = v_ref[pl.ds(off, BLK), cols]
                z, valid, l, sp = _sb_block(qs[hh], kb, n, m, r, c)
                e = (z - sp) + _dot2(l, m_gt) + runs[hh]
                a = jnp.where(valid, jnp.exp(jnp.where(valid, e, 0.0)), 0.0)
                new_runs.append(runs[hh] + jnp.sum(l, axis=1, keepdims=True))
                new_accs.append(accs[hh] + _dot(a, vb))
            return m - 1, tuple(new_runs), tuple(new_accs)

        zero = jnp.zeros((BLK, 1), f32)
        acc0 = jnp.zeros((BLK, DH_CD), f32)
        _, _, accs = lax.while_loop(cond, step, (n, (zero, zero), (acc0, acc0)))
        for hh, cols in enumerate(heads):
            o_ref[:, cols] = accs[hh]

    return pl.pallas_call(
        body, name="sb_fwd", grid=(W_CD // BLK, NB), in_specs=_sb_specs(T),
        out_specs=pl.BlockSpec((BLK, BLK), lambda hp, n: (n, hp)), out_shape=jax.ShapeDtypeStruct((T, W_CD), f32),
        compiler_params=_cparams(dimension_semantics=("arbitrary", "arbitrary")),
    )(proj, proj, proj)


def sb_bwd(proj, o, dmix):
    T = proj.shape[0]
    NB = T // BLK

    def body(q_ref, k_ref, v_ref, o_ref, do_ref, dq_ref, dk_ref, dv_ref):
        n = pl.program_id(1)

        @pl.when(n == 0)
        def _():
            dk_ref[...] = jnp.zeros_like(dk_ref)
            dv_ref[...] = jnp.zeros_like(dv_ref)
        r, c, m_gt, m_ge = _sb_consts()
        heads = [slice(DH_CD * hh, DH_CD * (hh + 1)) for hh in range(SB_PAIR)]
        qs = [q_ref[:, cols] * DH_CD ** -0.5 for cols in heads]
        dOs = [do_ref[:, cols].astype(bf16) for cols in heads]
        deltas = [jnp.sum(dOs[hh].astype(f32) * o_ref[:, cols], axis=1, keepdims=True) for hh, cols in enumerate(heads)]

        def cond(carry):
            m, runs = carry[0], carry[1]
            return jnp.logical_and(m >= 0, jnp.max(jnp.maximum(runs[0], runs[1])) > SB_EXIT)

        def step(carry):
            m, runs, runs_e, dqs = carry
            off = pl.multiple_of(m * BLK, BLK)
            new_runs, new_runs_e, new_dqs, dks, dvs = [], [], [], [], []
            for hh, cols in enumerate(heads):
                kb = k_ref[pl.ds(off, BLK), cols]
                vb = v_ref[pl.ds(off, BLK), cols]
                z, valid, l, sp = _sb_block(qs[hh], kb, n, m, r, c)
                e = (z - sp) + _dot2(l, m_gt) + runs[hh]
                a = jnp.where(valid, jnp.exp(jnp.where(valid, e, 0.0)), 0.0).astype(bf16)
                E = a.astype(f32) * _dot_nt(dOs[hh], vb)
                F = deltas[hh] - runs_e[hh] - _dot2(E, m_ge)
                sig = jnp.exp(z - sp)
                live = jnp.logical_and(valid, jnp.max(runs[hh]) > SB_EXIT)
                dz = jnp.where(live, E * (1.0 - sig) - F * sig, 0.0)
                dks.append(_dot_tn(dz, qs[hh]))
                dvs.append(_dot_tn(a, dOs[hh]))
                new_runs.append(runs[hh] + jnp.sum(l, axis=1, keepdims=True))
                new_runs_e.append(runs_e[hh] + jnp.sum(E, axis=1, keepdims=True))
                new_dqs.append(dqs[hh] + _dot(dz, kb))
            dk_ref[pl.ds(off, BLK), :] += jnp.concatenate(dks, axis=1)
            dv_ref[pl.ds(off, BLK), :] += jnp.concatenate(dvs, axis=1)
            return m - 1, tuple(new_runs), tuple(new_runs_e), tuple(new_dqs)

        zero = jnp.zeros((BLK, 1), f32)
        dq0 = jnp.zeros((BLK, DH_CD), f32)
        res = lax.while_loop(cond, step, (n, (zero, zero), (zero, zero), (dq0, dq0)))
        for hh, cols in enumerate(heads):
            dq_ref[:, cols] = res[3][hh] * DH_CD ** -0.5

    blk = pl.BlockSpec((BLK, BLK), lambda hp, n: (n, hp))
    full = pl.BlockSpec((T, BLK), lambda hp, n: (0, hp))
    sh = jax.ShapeDtypeStruct((T, W_CD), f32)
    return pl.pallas_call(
        body, name="sb_bwd", grid=(W_CD // BLK, NB),
        in_specs=_sb_specs(T) + [blk, pl.BlockSpec((BLK, BLK), lambda hp, n: (n, W_CD // BLK + hp))],
        out_specs=[blk, full, full], out_shape=[sh] * 3,
        compiler_params=_cparams(dimension_semantics=("arbitrary", "arbitrary")),
    )(proj, proj, proj, o, dmix)


def cd_assemble(dcq, cur, prev, meta, dsq, dsk, dsv):
    T = dcq.shape[0]
    NB = T // BLK
    KV = cur.shape[1]

    def body(dcq_r, cur_r, nxt_r, meta_r, dsq_r, dsk_r, dsv_r, o_ref):
        n = pl.program_id(0)
        kv = cur_r[...] + jnp.where(n < NB - 1, nxt_r[...], 0.0) + jnp.where(n == 0, meta_r[...], 0.0)
        o_ref[:, 0:W_CD] = dcq_r[...].astype(o_ref.dtype)
        o_ref[:, W_CD:W_CD + KV] = kv.astype(o_ref.dtype)
        for j, ref in enumerate((dsq_r, dsk_r, dsv_r)):
            o_ref[:, W_CD + KV + W_CD * j:W_CD + KV + W_CD * (j + 1)] = ref[...].astype(o_ref.dtype)

    wide = pl.BlockSpec((BLK, W_CD), lambda n: (n, 0))
    return pl.pallas_call(
        body, name="cd_assemble", grid=(NB,),
        in_specs=[wide, pl.BlockSpec((BLK, KV), lambda n: (n, 0)), pl.BlockSpec((BLK, KV), lambda n: (jnp.minimum(n + 1, NB - 1), 0)),
                  pl.BlockSpec((BLK, KV), lambda n: (0, 0)), wide, wide, wide],
        out_specs=pl.BlockSpec((BLK, CD_IN), lambda n: (n, 0)), out_shape=jax.ShapeDtypeStruct((T, CD_IN), bf16),
        compiler_params=_cparams(dimension_semantics=("arbitrary",)),
    )(dcq, cur, prev, meta, dsq, dsk, dsv)


def loss_grad(h, target):
    T, Dm = h.shape
    NB = T // BLK

    def body(h_ref, t_ref, dh_ref, l_ref):
        n = pl.program_id(0)

        @pl.when(n == 0)
        def _():
            dh_ref[...] = jnp.zeros_like(dh_ref)
            l_ref[...] = jnp.zeros_like(l_ref)

        @pl.when(n > 0)
        def _():
            err = h_ref[...] - t_ref[...]
            dh_ref[...] = err * (1.0 / Dm)
            part = 0.5 * jnp.sum(jnp.mean(err * err, axis=-1, keepdims=True), axis=0, keepdims=True)
            l_ref[...] += jnp.broadcast_to(part, l_ref.shape)

    row = pl.BlockSpec((BLK, Dm), lambda n: (n, 0))
    return pl.pallas_call(
        body, name="loss_grad", grid=(NB,),
        in_specs=[row, pl.BlockSpec((BLK, Dm), lambda n: (jnp.maximum(n - 1, 0), 0))],
        out_specs=[row, pl.BlockSpec((8, BLK), lambda n: (0, 0))],
        out_shape=[jax.ShapeDtypeStruct((T, Dm), f32), jax.ShapeDtypeStruct((8, BLK), f32)],
        compiler_params=_cparams(dimension_semantics=("arbitrary",)),
    )(h, target)


SUM_ROWS = 256
_MESH = pl.DeviceIdType.MESH
_ANY = pl.BlockSpec(memory_space=pl.ANY)


def _place():
    return lax.axis_index("x"), lax.axis_index("y"), lax.axis_index("c")


def _other_chips(x, y):
    return [(1 - x, y, 2 * (1 - x) + y), (x, 1 - y, 2 * x + 1 - y), (1 - x, 1 - y, 2 * (1 - x) + 1 - y)]


def gather_weights(wbuf, sbuf):
    def body(w_ref, s_ref, out_ref, outs_ref, send_sems, recv_sems):
        x, y, c = _place()
        p = 2 * x + y
        chips = _other_chips(x, y)
        sibling = (x, y, 1 - c)

        def copy(k, src, dst, to):
            return pltpu.make_async_remote_copy(src_ref=src, dst_ref=dst, send_sem=send_sems.at[k], recv_sem=recv_sems.at[k],
                                                device_id=to, device_id_type=_MESH)

        sends = [copy(9, w_ref, out_ref.at[p], sibling), copy(10, s_ref, outs_ref.at[p], sibling)]
        for j, (qx, qy, q) in enumerate(chips):
            sends.append(copy(j, w_ref.at[c], out_ref.at[p, c], (qx, qy, c)))
            sends.append(copy(3 + j, s_ref, outs_ref.at[p], (qx, qy, c)))
        for cp in sends:
            cp.start()
        for j, (qx, qy, q) in enumerate(chips):
            copy(j, w_ref.at[c], out_ref.at[q, c], (qx, qy, c)).wait_recv()
            fwd = copy(6 + j, out_ref.at[q, c], out_ref.at[q, c], sibling)
            fwd.start()
            sends.append(fwd)
        for j, (qx, qy, q) in enumerate(chips):
            copy(3 + j, s_ref, outs_ref.at[q], (qx, qy, c)).wait_recv()
            copy(6 + j, out_ref.at[q, 1 - c], out_ref.at[q, 1 - c], sibling).wait_recv()
        copy(9, w_ref, out_ref.at[p], sibling).wait_recv()
        copy(10, s_ref, outs_ref.at[p], sibling).wait_recv()
        for cp in sends:
            cp.wait_send()

    return pl.pallas_call(
        body, name="gather_weights", in_specs=[_ANY, _ANY], out_specs=[_ANY, _ANY],
        out_shape=[jax.ShapeDtypeStruct((4,) + wbuf.shape, wbuf.dtype), jax.ShapeDtypeStruct((4,) + sbuf.shape, sbuf.dtype)],
        scratch_shapes=[pltpu.SemaphoreType.DMA((11,)), pltpu.SemaphoreType.DMA((11,))],
    )(wbuf, sbuf)


def pair_exchange(g):
    S, _, H, Cw = g.shape

    def body(g_ref, out_ref, send_sem, recv_sem):
        x, y, c = _place()
        cp = pltpu.make_async_remote_copy(src_ref=g_ref.at[:, 1 - c], dst_ref=out_ref, send_sem=send_sem, recv_sem=recv_sem,
                                          device_id=(x, y, 1 - c), device_id_type=_MESH)
        cp.start()
        cp.wait()

    return pl.pallas_call(
        body, name="pair_exchange", in_specs=[_ANY], out_specs=_ANY, out_shape=jax.ShapeDtypeStruct((S, H, Cw), g.dtype),
        scratch_shapes=[pltpu.SemaphoreType.DMA, pltpu.SemaphoreType.DMA],
    )(g)


def pair_sum(g, got, c):
    S, _, H, Cw = g.shape
    tb = 3 * SUM_ROWS if H % (3 * SUM_ROWS) == 0 else SUM_ROWS

    def body(c_ref, a_ref, b_ref, o_ref):
        o_ref[...] = (a_ref[...].astype(f32) + b_ref[...].astype(f32)).astype(o_ref.dtype)

    spec = pl.BlockSpec((None, tb, Cw), lambda s, i, c_ref: (s, i, 0))
    return pl.pallas_call(
        body, name="pair_sum",
        grid_spec=pltpu.PrefetchScalarGridSpec(
            num_scalar_prefetch=1, grid=(S, H // tb),
            in_specs=[pl.BlockSpec((None, None, tb, Cw), lambda s, i, c_ref: (s, c_ref[0], i, 0)), spec], out_specs=spec),
        out_shape=jax.ShapeDtypeStruct((S, H, Cw), g.dtype),
        compiler_params=_cparams(dimension_semantics=("arbitrary", "arbitrary")),
    )(c, g, got)


def chip_exchange(hsum):
    S, H, Cw = hsum.shape

    def body(h_ref, out_ref, send_sems, recv_sems):
        x, y, c = _place()
        sends = []
        for j, (qx, qy, q) in enumerate(_other_chips(x, y)):
            cp = pltpu.make_async_remote_copy(src_ref=h_ref.at[q], dst_ref=out_ref.at[j], send_sem=send_sems.at[j],
                                              recv_sem=recv_sems.at[j], device_id=(qx, qy, c), device_id_type=_MESH)
            cp.start()
            sends.append(cp)
        for cp in sends:
            cp.wait()

    return pl.pallas_call(
        body, name="chip_exchange", in_specs=[_ANY], out_specs=_ANY, out_shape=jax.ShapeDtypeStruct((3, H, Cw), hsum.dtype),
        scratch_shapes=[pltpu.SemaphoreType.DMA((3,)), pltpu.SemaphoreType.DMA((3,))],
    )(hsum)


def chip_sum(hsum, parts, p):
    S, H, Cw = parts.shape
    tb = 3 * SUM_ROWS if H % (3 * SUM_ROWS) == 0 else SUM_ROWS

    def body(p_ref, own_ref, parts_ref, o_ref):
        acc = own_ref[...].astype(f32)
        for s in range(S):
            acc = acc + parts_ref[s].astype(f32)
        o_ref[...] = acc

    return pl.pallas_call(
        body, name="chip_sum",
        grid_spec=pltpu.PrefetchScalarGridSpec(
            num_scalar_prefetch=1, grid=(H // tb,),
            in_specs=[pl.BlockSpec((None, tb, Cw), lambda i, p_ref: (p_ref[0], i, 0)), pl.BlockSpec((S, tb, Cw), lambda i, p_ref: (0, i, 0))],
            out_specs=pl.BlockSpec((tb, Cw), lambda i, p_ref: (i, 0))),
        out_shape=jax.ShapeDtypeStruct((H, Cw), f32),
        compiler_params=_cparams(dimension_semantics=("arbitrary",)),
    )(p, hsum, parts)


def pair_gather(rsum):
    def body(r_ref, out_ref, send_sem, recv_sem):
        x, y, c = _place()
        cp = pltpu.make_async_remote_copy(src_ref=r_ref, dst_ref=out_ref, send_sem=send_sem, recv_sem=recv_sem,
                                          device_id=(x, y, 1 - c), device_id_type=_MESH)
        cp.start()
        cp.wait()

    return pl.pallas_call(
        body, name="pair_gather", in_specs=[_ANY], out_specs=_ANY, out_shape=jax.ShapeDtypeStruct(rsum.shape, rsum.dtype),
        scratch_shapes=[pltpu.SemaphoreType.DMA, pltpu.SemaphoreType.DMA],
    )(rsum)


def small_reduce(src):
    S, RS, Cw = src.shape

    def body(src_ref, out_ref, recv, send_sems, recv_sems):
        x, y, c = _place()
        me = 4 * x + 2 * y + c
        p = 2 * x + y
        recv[me] = src_ref[p]
        flips = [(fx, fy, fc) for fx in (0, 1) for fy in (0, 1) for fc in (0, 1)][1:]
        sends = []
        for k, (fx, fy, fc) in enumerate(flips):
            tx, ty, tc = (1 - x if fx else x), (1 - y if fy else y), (1 - c if fc else c)
            cp = pltpu.make_async_remote_copy(src_ref=src_ref.at[2 * tx + ty], dst_ref=recv.at[me], send_sem=send_sems.at[k],
                                              recv_sem=recv_sems.at[me], device_id=(tx, ty, tc), device_id_type=_MESH)
            cp.start()
            sends.append(cp)
        for k, (fx, fy, fc) in enumerate(flips):
            tx, ty, tc = (1 - x if fx else x), (1 - y if fy else y), (1 - c if fc else c)
            frm = 4 * tx + 2 * ty + tc
            pltpu.make_async_remote_copy(src_ref=src_ref.at[p], dst_ref=recv.at[frm], send_sem=send_sems.at[k],
                                         recv_sem=recv_sems.at[frm], device_id=(tx, ty, tc), device_id_type=_MESH).wait_recv()
        for cp in sends:
            cp.wait_send()
        acc = recv[0]
        for d in range(1, 8):
            acc = acc + recv[d]
        out_ref[...] = acc

    vm = pl.BlockSpec(memory_space=pltpu.VMEM)
    return pl.pallas_call(
        body, name="small_reduce", in_specs=[vm], out_specs=vm, out_shape=jax.ShapeDtypeStruct((RS, Cw), f32),
        scratch_shapes=[pltpu.VMEM((8, RS, Cw), f32), pltpu.SemaphoreType.DMA((7,)), pltpu.SemaphoreType.DMA((8,))],
    )(src)


def _row(v):
    return v.reshape(1, -1)


def _ffn_fwd(h, g_pre, g_post, wg, wu, wd):
    u, G, U, a = ffn_up(h, _row(g_pre), wg, wu)
    y, h_new = proj_norm_res(a, wd, h, _row(g_post), 0.5)
    return h_new, (h, u, G, U, a, y)


def _ffn_bwd(dh, saved, g_pre, g_post, wg, wu, wd):
    h, u, G, U, a, y = saved
    F = wg.shape[2]
    dy, dg_post = post_norm_bwd(y, _row(g_post), dh, 0.5)
    dG, dU = ffn_bwd_act(dy, wd, G, U)
    dwd = mm_tn(a, dy[None], D)
    dwg = mm_tn(u[None], dG, F)
    dwu = mm_tn(u[None], dU, F)
    dh_new, dg_pre = mm_nt_norm_bwd([(dG, wg), (dU, wu)], h, _row(g_pre), dh)
    return dh_new, dwg, dwu, dwd, dg_pre[0], dg_post[0]


def _lane_vec(v, at):
    return jnp.pad(v, (at, BLK - at - v.shape[0])).reshape(1, BLK)


def _ab_fwd(h, g_pre, g_post, w, tabs):
    u, proj = norm_proj(h, _row(g_pre), w["ab_in"], AB_IN_P // 3)
    ret, sall_r = ret_fwd(proj, *tabs)
    alog, dtb = _lane_vec(w["a_log"], N_HEAD_AB), _lane_vec(w["dt_bias"], N_HEAD_AB)
    cq, ck, cv, q, k, v, gates = gdn_prep_fwd(proj, w["conv"], alog, dtb)
    gdn, sall_g = gdn_chunk_fwd(q, k, v, gates, proj, _row(w["out_norm"]))
    mixed = jnp.concatenate([ret, gdn], axis=1)
    y, h_new = proj_norm_res(mixed[None], w["ab_out"][None], h, _row(g_post), 1.0)
    return h_new, (h, u, proj, sall_r, (cq, ck, cv, q, k, v, gates), sall_g, mixed, y, alog, dtb)


def _ab_bwd(dh, saved, g_pre, g_post, w, tabs):
    h, u, proj, sall_r, (cq, ck, cv, q, k, v, gates), sall_g, mixed, y, alog, dtb = saved
    dy, dg_post = post_norm_bwd(y, _row(g_post), dh, 1.0)
    dmix = mm_nt(dy, w["ab_out"])
    dw_out = mm_tn(mixed[None], dy[None], D)[0]
    drq, drk, drv, drg = ret_bwd(proj, *tabs, sall_r, dmix)
    onorm = _row(w["out_norm"])
    dq, dk, dv, dz, dgates, don = gdn_chunk_bwd(q, k, v, gates, proj, onorm, sall_g, dmix)
    dcq, dck, dcv, dgb, dal, ddt = gdn_prep_bwd(cq, ck, cv, proj, alog, dtb, dq, dk, dv, dgates)
    dxq, dxk, dxv, dconv = gdn_conv_bwd(dcq, dck, dcv, proj, w["conv"])
    dproj = jnp.concatenate([t.astype(bf16) for t in (drq, drk, drv, drg, dxq, dxk, dxv, dz, dgb)], axis=1)
    dw_in = mm_tn(u[None], dproj[None], AB_IN_P // 3)[0]
    dh_new, dg_pre = mm_nt_norm_bwd([(dproj[None], w["ab_in"][None])], h, _row(g_pre), dh, ksplit=3)
    small = dict(a_log=dal[0, N_HEAD_AB:2 * N_HEAD_AB], dt_bias=ddt[0, N_HEAD_AB:2 * N_HEAD_AB], out_norm=don[0], conv=dconv[0:GDN_K])
    return dh_new, dw_in, dw_out, dg_pre[0], dg_post[0], small


def _cd_fwd(h, g_pre, g_post, w):
    u, proj = norm_proj(h, _row(g_pre), w["cd_in"], CD_IN // 3)
    swa = swa_fwd(proj, w["sinks"])
    sb = sb_fwd(proj)
    mixed = jnp.concatenate([swa.astype(bf16), sb.astype(bf16)], axis=1)
    y, h_new = proj_norm_res(mixed[None], w["cd_out"][None], h, _row(g_post), 1.0)
    return h_new, (h, u, proj, sb, mixed, y)


def _cd_bwd(dh, saved, g_pre, g_post, w):
    h, u, proj, sb, mixed, y = saved
    dy, dg_post = post_norm_bwd(y, _row(g_post), dh, 1.0)
    dmix = mm_nt(dy, w["cd_out"])
    dw_out = mm_tn(mixed[None], dy[None], D)[0]
    dcq, cur, prev, meta, dsinks = swa_bwd(proj, w["sinks"], dmix)
    dsq, dsk, dsv = sb_bwd(proj, sb, dmix)
    dproj = cd_assemble(dcq, cur, prev, meta, dsq, dsk, dsv)
    dw_in = mm_tn(u[None], dproj[None], CD_IN // 3)[0]
    dh_new, dg_pre = mm_nt_norm_bwd([(dproj[None], w["cd_in"][None])], h, _row(g_pre), dh, ksplit=3)
    return dh_new, dw_in, dw_out, dg_pre[0], dg_post[0], dsinks[:, 0]


def local_step(x, target, w):
    L = x.shape[0]
    T = PAD + N_META + L
    tabs = rot_tables(T)
    h = jnp.concatenate([jnp.zeros((PAD, D), f32), w["meta"], x], axis=0)
    ng = w["norm_gains"]
    saved = []
    for i in range(2):
        g = ng[i]
        h, s1 = _ffn_fwd(h, g[0], g[1], w["wg"][i, 0], w["wu"][i, 0], w["wd"][i, 0])
        if i == 0:
            h, sm = _ab_fwd(h, g[2], g[3], w, tabs)
        else:
            h, sm = _cd_fwd(h, g[2], g[3], w)
        h, s2 = _ffn_fwd(h, g[4], g[5], w["wg"][i, 1], w["wu"][i, 1], w["wd"][i, 1])
        saved.append((s1, sm, s2))
    dh, lpart = loss_grad(h, target)
    grads = {}
    dng = [[None] * 6 for _ in range(2)]
    dwg = [[None, None], [None, None]]
    dwu = [[None, None], [None, None]]
    dwd = [[None, None], [None, None]]
    for i in (1, 0):
        g = ng[i]
        s1, sm, s2 = saved[i]
        dh, dwg[i][1], dwu[i][1], dwd[i][1], dng[i][4], dng[i][5] = _ffn_bwd(dh, s2, g[4], g[5], w["wg"][i, 1], w["wu"][i, 1], w["wd"][i, 1])
        if i == 0:
            dh, grads["ab_in"], grads["ab_out"], dng[i][2], dng[i][3], small = _ab_bwd(dh, sm, g[2], g[3], w, tabs)
            grads.update(small)
        else:
            dh, grads["cd_in"], grads["cd_out"], dng[i][2], dng[i][3], grads["sinks"] = _cd_bwd(dh, sm, g[2], g[3], w)
        dh, dwg[i][0], dwu[i][0], dwd[i][0], dng[i][0], dng[i][1] = _ffn_bwd(dh, s1, g[0], g[1], w["wg"][i, 0], w["wu"][i, 0], w["wd"][i, 0])
    grads["wg"], grads["wu"], grads["wd"] = dwg, dwu, dwd
    grads["norm_gains"] = jnp.stack([jnp.stack(r) for r in dng])
    grads["meta"] = dh[PAD:PAD + N_META]
    return lpart[0, 0], dh[PAD + N_META:], grads


def _r16(n, mult=16):
    return -(-n // mult) * mult


def _big_layout(F):
    halves = (("wg", "wu"), ("wd", "ab_in", "ab_out", "cd_in", "cd_out"))
    rows = dict(wg=4 * F, wu=4 * F, wd=4 * F, ab_in=AB_IN // 4, ab_out=D // 4, cd_in=CD_IN // 4, cd_out=D // 4)
    offs, used = {}, []
    for hf, names in enumerate(halves):
        o = 0
        for n in names:
            offs[n] = (hf, o, rows[n])
            o += _r16(rows[n])
        used.append(o)
    return offs, _r16(max(used), SUM_ROWS), halves


def _cat_rows(parts, total, mult=16):
    out = []
    for p in parts:
        pad = _r16(p.shape[-2], mult) - p.shape[-2]
        out.append(jnp.pad(p, [(0, 0)] * (p.ndim - 2) + [(0, pad), (0, 0)]) if pad else p)
    used = sum(o.shape[-2] for o in out)
    if total > used:
        out.append(jnp.zeros(out[0].shape[:-2] + (total - used, out[0].shape[-1]), out[0].dtype))
    return jnp.concatenate(out, axis=-2)


SMALL_ROWS = 72
REPL_ROWS = 8


def _small_rows(meta, ng, conv):
    lead = meta.shape[:-2]
    return _cat_rows([meta.reshape(lead + (32, BLK)), ng.reshape(lead + (24, BLK)), conv.reshape(lead + (12, BLK))], SMALL_ROWS, 8)


def _small_unrows(buf):
    lead = buf.shape[:-2]
    return buf[..., 0:32, :].reshape(lead + (N_META, D // 4)), buf[..., 32:56, :].reshape(lead + (2, 6, D // 4)), \
        buf[..., 56:68, :].reshape(lead + (GDN_K, 3 * W_AB // 4))


def _shard_cols(a, axis):
    shp = a.shape
    a = a.reshape(shp[:axis] + (4, shp[axis] // 4) + shp[axis + 1:])
    return jnp.moveaxis(a, axis, 0)


def _unshard_cols(a, axis):
    a = jnp.moveaxis(a, 0, axis)
    shp = a.shape
    return a.reshape(shp[:axis] + (4 * shp[axis + 1],) + shp[axis + 2:])


def kernel(x, meta_tokens, norm_gains, ffn_w_gate, ffn_w_up, ffn_w_down, ab_w_in, ab_conv_w, ab_a_log, ab_dt_bias, ab_out_norm, ab_w_out, cd_w_in, cd_sinks, cd_w_out, loss_target, m_meta_tokens, m_norm_gains, m_ffn_w_gate, m_ffn_w_up, m_ffn_w_down, m_ab_w_in, m_ab_conv_w, m_ab_a_log, m_ab_dt_bias, m_ab_out_norm, m_ab_w_out, m_cd_w_in, m_cd_sinks, m_cd_w_out, v_meta_tokens, v_norm_gains, v_ffn_w_gate, v_ffn_w_up, v_ffn_w_down, v_ab_w_in, v_ab_conv_w, v_ab_a_log, v_ab_dt_bias, v_ab_out_norm, v_ab_w_out, v_cd_w_in, v_cd_sinks, v_cd_w_out):
    F = ffn_w_gate.shape[-1]
    offs, H, halves = _big_layout(F)
    shard = dict(wg=ffn_w_gate, wu=ffn_w_up, wd=ffn_w_down, ab_in=ab_w_in, ab_out=ab_w_out, cd_in=cd_w_in, cd_out=cd_w_out)

    wbuf = jnp.stack([_cat_rows([shard[n].reshape(-1, D).astype(bf16) for n in names], H) for names in halves])
    sbuf = _small_rows(meta_tokens, norm_gains, ab_conv_w[0])
    gw, gs = gather_weights(wbuf, sbuf)

    def part(n):
        hf, o, r = offs[n]
        return gw[:, hf, o:o + r]

    meta_s, ng_s, conv_s = _small_unrows(gs)
    w = dict(
        wg=jnp.transpose(part("wg").reshape(4, 2, 2, D, F), (1, 2, 0, 3, 4)),
        wu=jnp.transpose(part("wu").reshape(4, 2, 2, D, F), (1, 2, 0, 3, 4)),
        wd=jnp.transpose(part("wd").reshape(4, 2, 2, F, D), (1, 2, 0, 3, 4)),
        ab_in=jnp.pad(_unshard_cols(part("ab_in").reshape(4, D, AB_IN // 4), 1), ((0, 0), (0, AB_IN_P - AB_IN))),
        ab_out=part("ab_out").reshape(D, D),
        cd_in=_unshard_cols(part("cd_in").reshape(4, D, CD_IN // 4), 1),
        cd_out=part("cd_out").reshape(D, D),
        meta=_unshard_cols(meta_s, 1), norm_gains=_unshard_cols(ng_s, 2), conv=_unshard_cols(conv_s, 1),
        a_log=ab_a_log[0], dt_bias=ab_dt_bias[0], out_norm=ab_out_norm[0], sinks=cd_sinks[0],
    )

    loss_local, dx, g = local_step(x[0], loss_target[0], w)

    def stack22(t):
        return jnp.stack([jnp.stack(r) for r in t])

    gparts = dict(
        wg=jnp.transpose(stack22(g["wg"]), (2, 0, 1, 3, 4)).reshape(4, 4 * F, D),
        wu=jnp.transpose(stack22(g["wu"]), (2, 0, 1, 3, 4)).reshape(4, 4 * F, D),
        wd=jnp.transpose(stack22(g["wd"]), (2, 0, 1, 3, 4)).reshape(4, 4 * F, D),
        ab_in=_shard_cols(g["ab_in"][:, :AB_IN], 1).reshape(4, AB_IN // 4, D),
        ab_out=g["ab_out"].reshape(4, D // 4, D),
        cd_in=_shard_cols(g["cd_in"], 1).reshape(4, CD_IN // 4, D),
        cd_out=g["cd_out"].reshape(4, D // 4, D),
    )
    gbuf = jnp.stack([_cat_rows([gparts[n].astype(bf16) for n in names], H) for names in halves], axis=1)
    core = lax.axis_index("c").astype(jnp.int32)
    chip = (2 * lax.axis_index("x") + lax.axis_index("y")).astype(jnp.int32)
    got = pair_exchange(gbuf)
    hsum = pair_sum(gbuf, got, core.reshape(1))
    parts = chip_exchange(hsum)
    rsum = chip_sum(hsum, parts, chip.reshape(1))
    other = pair_gather(rsum)
    ghalf = (jnp.where(core == 0, rsum, other), jnp.where(core == 0, other, rsum))

    onehot = np.eye(REPL_ROWS, dtype=np.float32)
    repl = sum(onehot[k][:, None] * _lane_vec(g[n], 0) for k, n in enumerate(("a_log", "dt_bias", "out_norm", "sinks")))
    ssrc = jnp.concatenate([_small_rows(_shard_cols(g["meta"], 1), _shard_cols(g["norm_gains"], 2), _shard_cols(g["conv"], 1)),
                            jnp.broadcast_to(repl, (4, REPL_ROWS, BLK))], axis=1)
    sred = small_reduce(ssrc)
    g_meta, g_ng, g_conv = _small_unrows(sred[:SMALL_ROWS])

    def gpart(n, shape):
        hf, o, r = offs[n]
        return ghalf[hf][o:o + r].reshape(shape)

    grad = dict(
        meta_tokens=g_meta, norm_gains=g_ng,
        ffn_w_gate=gpart("wg", ffn_w_gate.shape), ffn_w_up=gpart("wu", ffn_w_up.shape), ffn_w_down=gpart("wd", ffn_w_down.shape),
        ab_w_in=gpart("ab_in", ab_w_in.shape), ab_conv_w=g_conv[None],
        ab_a_log=sred[SMALL_ROWS:SMALL_ROWS + 1, 0:N_HEAD_AB], ab_dt_bias=sred[SMALL_ROWS + 1:SMALL_ROWS + 2, 0:N_HEAD_AB],
        ab_out_norm=sred[SMALL_ROWS + 2:SMALL_ROWS + 3, :], ab_w_out=gpart("ab_out", ab_w_out.shape),
        cd_w_in=gpart("cd_in", cd_w_in.shape), cd_sinks=sred[SMALL_ROWS + 3:SMALL_ROWS + 4, 0:2 * SWA_G], cd_w_out=gpart("cd_out", cd_w_out.shape),
    )

    weights = dict(meta_tokens=meta_tokens, norm_gains=norm_gains, ffn_w_gate=ffn_w_gate, ffn_w_up=ffn_w_up, ffn_w_down=ffn_w_down,
                   ab_w_in=ab_w_in, ab_conv_w=ab_conv_w, ab_a_log=ab_a_log, ab_dt_bias=ab_dt_bias, ab_out_norm=ab_out_norm,
                   ab_w_out=ab_w_out, cd_w_in=cd_w_in, cd_sinks=cd_sinks, cd_w_out=cd_w_out)
    ms = dict(meta_tokens=m_meta_tokens, norm_gains=m_norm_gains, ffn_w_gate=m_ffn_w_gate, ffn_w_up=m_ffn_w_up, ffn_w_down=m_ffn_w_down,
              ab_w_in=m_ab_w_in, ab_conv_w=m_ab_conv_w, ab_a_log=m_ab_a_log, ab_dt_bias=m_ab_dt_bias, ab_out_norm=m_ab_out_norm,
              ab_w_out=m_ab_w_out, cd_w_in=m_cd_w_in, cd_sinks=m_cd_sinks, cd_w_out=m_cd_w_out)
    vs = dict(meta_tokens=v_meta_tokens, norm_gains=v_norm_gains, ffn_w_gate=v_ffn_w_gate, ffn_w_up=v_ffn_w_up, ffn_w_down=v_ffn_w_down,
              ab_w_in=v_ab_w_in, ab_conv_w=v_ab_conv_w, ab_a_log=v_ab_a_log, ab_dt_bias=v_ab_dt_bias, ab_out_norm=v_ab_out_norm,
              ab_w_out=v_ab_w_out, cd_w_in=v_cd_w_in, cd_sinks=v_cd_sinks, cd_w_out=v_cd_w_out)
    order = list(weights)
    delta, new_m, new_v = {}, {}, {}
    for n in order:
        shp = weights[n].shape
        two = (-1, shp[-1])
        d, mn, vn = adamw(weights[n].reshape(two), grad[n].reshape(two), ms[n].reshape(two), vs[n].reshape(two))
        delta[n], new_m[n], new_v[n] = d.reshape(shp), mn.reshape(shp), vn.reshape(shp)

    loss = lax.psum(loss_local, ("x", "y", "c"))
    return (loss, dx[None], *[grad[n].reshape(weights[n].shape) for n in order], *[delta[n] for n in order],
            *[new_m[n] for n in order], *[new_v[n] for n in order])
```

```python
import functools

import numpy as np
import jax
import jax.numpy as jnp
from jax import lax
from jax.experimental import pallas as pl
from jax.experimental.pallas import tpu as pltpu

f32 = jnp.float32
bf16 = jnp.bfloat16

EPS = 1e-6
D = 1024
N_META = 16
PAD = 112
BLK = 128
GDN_C = 64
N_HEAD_AB = 4
DH_AB = 128
AB_IN = 4104
AB_IN_P = 4224
CD_IN = 2304
ADAM_LR, ADAM_B1, ADAM_B2, ADAM_EPS, ADAM_WD, ADAM_STEP = 0.001, 0.9, 0.999, 1e-08, 0.01, 10
VMEM_LIMIT = 56 * 1024 * 1024
NEG = -1e30
SB_EXIT = -104.0

_NT = (((1,), (1,)), ((), ()))
_TN = (((0,), (0,)), ((), ()))


def _cparams(**kw):
    return pltpu.CompilerParams(vmem_limit_bytes=VMEM_LIMIT, **kw)


def _dot(a, b):
    return jnp.dot(a.astype(bf16), b.astype(bf16), preferred_element_type=f32)


def _dot_nt(a, b):
    return lax.dot_general(a.astype(bf16), b.astype(bf16), _NT, preferred_element_type=f32)


def _dot_tn(a, b):
    return lax.dot_general(a.astype(bf16), b.astype(bf16), _TN, preferred_element_type=f32)


def _dot2(a, b01):
    hi = a.astype(bf16)
    lo = (a - hi.astype(f32)).astype(bf16)
    return jnp.dot(hi, b01, preferred_element_type=f32) + jnp.dot(lo, b01, preferred_element_type=f32)


def _row_tile(t):
    for c in (640, 512, 256, 128):
        if t % c == 0:
            return c
    raise ValueError(t)


def _sigmoid(x):
    return 1.0 / (1.0 + jnp.exp(-x))


def _sigmoid_fast(x):
    return pl.reciprocal(1.0 + jnp.exp(-x), approx=True)


def _silu(x):
    return x * _sigmoid(x)


def _softplus(x):
    return jnp.maximum(x, 0.0) + jnp.log(1.0 + jnp.exp(-jnp.abs(x)))


def _rms(x, g):
    r = lax.rsqrt(jnp.mean(x * x, axis=-1, keepdims=True) + EPS)
    return x * r * g


def _rms_bwd(x, g, dy):
    r = lax.rsqrt(jnp.mean(x * x, axis=-1, keepdims=True) + EPS)
    xh = x * r
    dg = jnp.sum(dy * xh, axis=0, keepdims=True)
    dxh = dy * g
    dx = r * (dxh - xh * jnp.mean(dxh * xh, axis=-1, keepdims=True))
    return dx, dg


def _zero_pad_rows(v, i, tr):
    rows = i * tr + lax.broadcasted_iota(jnp.int32, (tr, 1), 0)
    return jnp.where(rows >= PAD, v, 0.0)


def _acc8(ref, row, first):
    @pl.when(first)
    def _():
        ref[...] = jnp.zeros_like(ref)
    ref[...] += jnp.broadcast_to(row, ref.shape)


def ffn_up(h, g, wg, wu):
    T, Dm = h.shape
    S, _, F = wg.shape
    tm = _row_tile(T)

    def body(h_ref, g_ref, wg_ref, wu_ref, u_ref, G_ref, U_ref, a_ref):
        @pl.when(pl.program_id(1) == 0)
        def _():
            u_ref[...] = _rms(h_ref[...], g_ref[...]).astype(u_ref.dtype)
        u = u_ref[...]
        G = _dot(u, wg_ref[...])
        U = _dot(u, wu_ref[...])
        G_ref[...] = G.astype(G_ref.dtype)
        U_ref[...] = U.astype(U_ref.dtype)
        a_ref[...] = (G * _sigmoid_fast(G) * U).astype(a_ref.dtype)

    act = jax.ShapeDtypeStruct((S, T, F), bf16)
    wspec = pl.BlockSpec((None, Dm, F), lambda i, s: (s, 0, 0))
    aspec = pl.BlockSpec((None, tm, F), lambda i, s: (s, i, 0))
    return pl.pallas_call(
        body, name="ffn_up", grid=(T // tm, S),
        in_specs=[pl.BlockSpec((tm, Dm), lambda i, s: (i, 0)), pl.BlockSpec((1, Dm), lambda i, s: (0, 0)), wspec, wspec],
        out_specs=[pl.BlockSpec((tm, Dm), lambda i, s: (i, 0)), aspec, aspec, aspec],
        out_shape=[jax.ShapeDtypeStruct((T, Dm), bf16), act, act, act],
        compiler_params=_cparams(dimension_semantics=("arbitrary", "arbitrary")),
    )(h, g, wg, wu)


def norm_proj(h, g, w, tn):
    T, Dm = h.shape
    N = w.shape[1]
    tm = _row_tile(T)

    def body(h_ref, g_ref, w_ref, u_ref, p_ref):
        @pl.when(pl.program_id(1) == 0)
        def _():
            u_ref[...] = _rms(h_ref[...], g_ref[...]).astype(u_ref.dtype)
        p_ref[...] = _dot(u_ref[...], w_ref[...])

    return pl.pallas_call(
        body, name="norm_proj", grid=(T // tm, N // tn),
        in_specs=[pl.BlockSpec((tm, Dm), lambda i, j: (i, 0)), pl.BlockSpec((1, Dm), lambda i, j: (0, 0)),
                  pl.BlockSpec((Dm, tn), lambda i, j: (0, j))],
        out_specs=[pl.BlockSpec((tm, Dm), lambda i, j: (i, 0)), pl.BlockSpec((tm, tn), lambda i, j: (i, j))],
        out_shape=[jax.ShapeDtypeStruct((T, Dm), bf16), jax.ShapeDtypeStruct((T, N), f32)],
        compiler_params=_cparams(dimension_semantics=("arbitrary", "arbitrary")),
    )(h, g, w)


def proj_norm_res(a, w, h, g, coef):
    S, T, F = a.shape
    Dm = w.shape[2]
    tm = _row_tile(T)

    def body(a_ref, w_ref, h_ref, g_ref, y_ref, o_ref, acc):
        s = pl.program_id(1)

        @pl.when(s == 0)
        def _():
            acc[...] = jnp.zeros_like(acc)
        acc[...] += _dot(a_ref[...], w_ref[...])

        @pl.when(s == S - 1)
        def _():
            y = acc[...]
            y_ref[...] = y
            o_ref[...] = h_ref[...] + coef * _rms(y, g_ref[...])

    row = pl.BlockSpec((tm, Dm), lambda i, s: (i, 0))
    return pl.pallas_call(
        body, name="proj_norm_res", grid=(T // tm, S),
        in_specs=[pl.BlockSpec((None, tm, F), lambda i, s: (s, i, 0)), pl.BlockSpec((None, F, Dm), lambda i, s: (s, 0, 0)),
                  row, pl.BlockSpec((1, Dm), lambda i, s: (0, 0))],
        out_specs=[row, row],
        out_shape=[jax.ShapeDtypeStruct((T, Dm), f32), jax.ShapeDtypeStruct((T, Dm), f32)],
        scratch_shapes=[pltpu.VMEM((tm, Dm), f32)],
        compiler_params=_cparams(dimension_semantics=("arbitrary", "arbitrary")),
    )(a, w, h, g)


def post_norm_bwd(y, g, dz, coef):
    T, Dm = y.shape
    tm = _row_tile(T)

    def body(y_ref, g_ref, dz_ref, dy_ref, dg_ref):
        dy, dg = _rms_bwd(y_ref[...], g_ref[...], coef * dz_ref[...])
        dy_ref[...] = _zero_pad_rows(dy, pl.program_id(0), tm).astype(dy_ref.dtype)
        _acc8(dg_ref, dg, pl.program_id(0) == 0)

    row = pl.BlockSpec((tm, Dm), lambda i: (i, 0))
    return pl.pallas_call(
        body, name="post_norm_bwd", grid=(T // tm,),
        in_specs=[row, pl.BlockSpec((1, Dm), lambda i: (0, 0)), row],
        out_specs=[row, pl.BlockSpec((8, Dm), lambda i: (0, 0))],
        out_shape=[jax.ShapeDtypeStruct((T, Dm), bf16), jax.ShapeDtypeStruct((8, Dm), f32)],
        compiler_params=_cparams(dimension_semantics=("arbitrary",)),
    )(y, g, dz)


def ffn_bwd_act(y, g, dz, coef, wd, G, U):
    T, Dm = y.shape
    S, F, _ = wd.shape
    tm = _row_tile(T)

    def body(y_ref, g_ref, dz_ref, w_ref, G_ref, U_ref, dy_ref, dg_ref, dG_ref, dU_ref):
        i = pl.program_id(0)

        @pl.when(pl.program_id(1) == 0)
        def _():
            dy, dg = _rms_bwd(y_ref[...], g_ref[...], coef * dz_ref[...])
            dy_ref[...] = _zero_pad_rows(dy, i, tm).astype(dy_ref.dtype)
            _acc8(dg_ref, dg, i == 0)
        da = _dot_nt(dy_ref[...], w_ref[...])
        Gv = G_ref[...].astype(f32)
        Uv = U_ref[...].astype(f32)
        sg = _sigmoid_fast(Gv)
        dU_ref[...] = (da * Gv * sg).astype(dU_ref.dtype)
        dG_ref[...] = (da * Uv * sg * (1.0 + Gv * (1.0 - sg))).astype(dG_ref.dtype)

    aspec = pl.BlockSpec((None, tm, F), lambda i, s: (s, i, 0))
    row = pl.BlockSpec((tm, Dm), lambda i, s: (i, 0))
    act = jax.ShapeDtypeStruct((S, T, F), bf16)
    return pl.pallas_call(
        body, name="ffn_bwd_act", grid=(T // tm, S),
        in_specs=[row, pl.BlockSpec((1, Dm), lambda i, s: (0, 0)), row, pl.BlockSpec((None, F, Dm), lambda i, s: (s, 0, 0)), aspec, aspec],
        out_specs=[row, pl.BlockSpec((8, Dm), lambda i, s: (0, 0)), aspec, aspec],
        out_shape=[jax.ShapeDtypeStruct((T, Dm), bf16), jax.ShapeDtypeStruct((8, Dm), f32), act, act],
        compiler_params=_cparams(dimension_semantics=("arbitrary", "arbitrary")),
    )(y, g, dz, wd, G, U)


def mm_nt_norm_bwd(pairs, h, g, dres, ksplit=1):
    S, T, K = pairs[0][0].shape
    assert S == 1 or ksplit == 1
    steps = S * ksplit
    tk = K // ksplit
    Dm = h.shape[1]
    tm = _row_tile(T)
    n = len(pairs)

    def body(*refs):
        ab = refs[:2 * n]
        h_ref, g_ref, dres_ref, dh_ref, dg_ref, acc = refs[2 * n:]
        i, s = pl.program_id(0), pl.program_id(1)

        @pl.when(s == 0)
        def _():
            acc[...] = jnp.zeros_like(acc)
        for p in range(n):
            acc[...] += _dot_nt(ab[2 * p][...], ab[2 * p + 1][...])

        @pl.when(s == steps - 1)
        def _():
            dx, dg = _rms_bwd(h_ref[...], g_ref[...], acc[...])
            dh_ref[...] = _zero_pad_rows(dres_ref[...] + dx, i, tm)
            _acc8(dg_ref, dg, i == 0)

    row = pl.BlockSpec((tm, Dm), lambda i, s: (i, 0))
    if S > 1:
        amap, bmap = (lambda i, s: (s, i, 0)), (lambda i, s: (s, 0, 0))
    else:
        amap, bmap = (lambda i, s: (0, i, s)), (lambda i, s: (0, 0, s))
    in_specs, args = [], []
    for a, b in pairs:
        in_specs += [pl.BlockSpec((None, tm, tk), amap), pl.BlockSpec((None, Dm, tk), bmap)]
        args += [a, b]
    return pl.pallas_call(
        body, name="mm_nt_norm_bwd", grid=(T // tm, steps),
        in_specs=in_specs + [row, pl.BlockSpec((1, Dm), lambda i, s: (0, 0)), row],
        out_specs=[row, pl.BlockSpec((8, Dm), lambda i, s: (0, 0))],
        out_shape=[jax.ShapeDtypeStruct((T, Dm), f32), jax.ShapeDtypeStruct((8, Dm), f32)],
        scratch_shapes=[pltpu.VMEM((tm, Dm), f32)],
        compiler_params=_cparams(dimension_semantics=("arbitrary", "arbitrary")),
    )(*args, h, g, dres)


def mm_nt(a, b):
    T, K = a.shape
    N = b.shape[0]
    tm = _row_tile(T)

    def body(a_ref, b_ref, o_ref):
        o_ref[...] = _dot_nt(a_ref[...], b_ref[...])

    return pl.pallas_call(
        body, name="mm_nt", grid=(T // tm,),
        in_specs=[pl.BlockSpec((tm, K), lambda i: (i, 0)), pl.BlockSpec((N, K), lambda i: (0, 0))],
        out_specs=pl.BlockSpec((tm, N), lambda i: (i, 0)),
        out_shape=jax.ShapeDtypeStruct((T, N), f32),
        compiler_params=_cparams(dimension_semantics=("arbitrary",)),
    )(a, b)


def mm_tn(a, b, tn):
    Sa, T, M = a.shape
    Sb, _, N = b.shape
    S = max(Sa, Sb)
    tk = 13 * BLK if T % (13 * BLK) == 0 else _row_tile(T)
    nk = T // tk

    def body(a_ref, b_ref, o_ref, acc):
        k = pl.program_id(2)

        @pl.when(k == 0)
        def _():
            acc[...] = jnp.zeros_like(acc)
        acc[...] += _dot_tn(a_ref[...], b_ref[...])

        @pl.when(k == nk - 1)
        def _():
            o_ref[...] = acc[...].astype(o_ref.dtype)

    return pl.pallas_call(
        body, name="mm_tn", grid=(S, N // tn, nk),
        in_specs=[pl.BlockSpec((None, tk, M), (lambda s, j, k: (s, k, 0)) if Sa > 1 else (lambda s, j, k: (0, k, 0))),
                  pl.BlockSpec((None, tk, tn), (lambda s, j, k: (s, k, j)) if Sb > 1 else (lambda s, j, k: (0, k, j)))],
        out_specs=pl.BlockSpec((None, M, tn), lambda s, j, k: (s, 0, j)),
        out_shape=jax.ShapeDtypeStruct((S, M, N), bf16),
        scratch_shapes=[pltpu.VMEM((M, tn), f32)],
        compiler_params=_cparams(dimension_semantics=("arbitrary", "arbitrary", "arbitrary")),
    )(a, b)


def adamw(w, g, m, v):
    R, C = w.shape
    tr = 512 if R % 512 == 0 else (256 if R % 256 == 0 else R)
    c1 = np.float32(1.0 - ADAM_B1 ** ADAM_STEP)
    c2 = np.float32(1.0 - ADAM_B2 ** ADAM_STEP)

    def body(w_ref, g_ref, m_ref, v_ref, d_ref, mo_ref, vo_ref):
        gv = g_ref[...]
        mn = ADAM_B1 * m_ref[...] + (1.0 - ADAM_B1) * gv
        vn = ADAM_B2 * v_ref[...] + (1.0 - ADAM_B2) * (gv * gv)
        mo_ref[...] = mn
        vo_ref[...] = vn
        d_ref[...] = -ADAM_LR * ((mn / c1) / (jnp.sqrt(vn / c2) + ADAM_EPS) + ADAM_WD * w_ref[...])

    spec = pl.BlockSpec((tr, C), lambda i: (i, 0))
    sh = jax.ShapeDtypeStruct((R, C), f32)
    return pl.pallas_call(
        body, name="adamw", grid=(R // tr,), in_specs=[spec] * 4, out_specs=[spec] * 3, out_shape=[sh] * 3,
        compiler_params=_cparams(dimension_semantics=("arbitrary",)),
    )(w, g, m, v)


_RET_LOG_GAMMA = [float(v) for v in np.log1p(-np.exp2(-5.0 - np.arange(N_HEAD_AB, dtype=np.float32))).astype(np.float32)]


def rot_tables(T):
    pos = jnp.arange(T, dtype=f32) - float(PAD)
    inv_freq = 1.0 / (10000.0 ** jnp.linspace(0.0, 1.0, DH_AB // 2, dtype=f32))
    ang = pos[:, None] * inv_freq[None, :]
    cos, sin = jnp.cos(ang), jnp.sin(ang)
    return jnp.repeat(cos, 2, axis=1), jnp.stack([-sin, sin], axis=-1).reshape(T, DH_AB)


def _swap_pairs(x):
    lane = lax.broadcasted_iota(jnp.int32, x.shape, 1)
    return jnp.where(lane % 2 == 0, pltpu.roll(x, x.shape[1] - 1, 1), pltpu.roll(x, 1, 1))


def _rot(x, c, s):
    return x * c + _swap_pairs(x) * s


def _rot_bwd(d, c, s):
    return d * c + _swap_pairs(d * s)


def _ret_mats(lg):
    i = lax.broadcasted_iota(jnp.int32, (BLK, BLK), 0).astype(f32)
    j = lax.broadcasted_iota(jnp.int32, (BLK, BLK), 1).astype(f32)
    diff = i - j
    decay = jnp.where(diff >= 0, jnp.exp(jnp.maximum(diff, 0.0) * lg), 0.0)
    xi = jnp.exp((i + 1.0) * lg)
    zeta = jnp.exp((BLK - 1.0 - i) * lg)
    return decay, xi, zeta, float(np.exp(np.float32(BLK * lg)))


def _ret_head(q_ref, k_ref, v_ref, cos, sin, h, Sp):
    sl = slice(DH_AB * h, DH_AB * (h + 1))
    decay, xi, zeta, gc = _ret_mats(_RET_LOG_GAMMA[h])
    q = _rot(q_ref[:, sl], cos, sin)
    k = _rot(k_ref[:, sl], cos, sin) * DH_AB ** -0.5
    v = v_ref[:, sl]
    P = _dot_nt(q, k) * decay
    ret = _dot(P, v) + _dot(q * xi, Sp)
    return sl, q, k, v, P, ret, decay, xi, zeta, gc


def _proj_spec(rows, width, col, rev=None):
    if rev is None:
        return pl.BlockSpec((rows, width), lambda n: (n, col))
    return pl.BlockSpec((rows, width), lambda n: (rev - n, col))


def ret_fwd(proj, cos, sin):
    T = proj.shape[0]
    NC = T // BLK
    W = N_HEAD_AB * DH_AB

    def body(q_ref, k_ref, v_ref, g_ref, cos_ref, sin_ref, o_ref, sall_ref, S):
        @pl.when(pl.program_id(0) == 0)
        def _():
            S[...] = jnp.zeros_like(S)
        cos_v, sin_v = cos_ref[...], sin_ref[...]
        for h in range(N_HEAD_AB):
            Sp = S[h]
            sall_ref[0, h] = Sp
            sl, q, k, v, P, ret, decay, xi, zeta, gc = _ret_head(q_ref, k_ref, v_ref, cos_v, sin_v, h, Sp)
            S[h] = Sp * gc + _dot_tn(k * zeta, v)
            mu = jnp.mean(ret, axis=-1, keepdims=True)
            cen = ret - mu
            y = cen * lax.rsqrt(jnp.mean(cen * cen, axis=-1, keepdims=True) + EPS)
            o_ref[:, sl] = (y * _silu(g_ref[:, sl])).astype(o_ref.dtype)

    tab = pl.BlockSpec((BLK, DH_AB), lambda n: (n, 0))
    return pl.pallas_call(
        body, name="ret_fwd", grid=(NC,),
        in_specs=[_proj_spec(BLK, W, 0), _proj_spec(BLK, W, 1), _proj_spec(BLK, W, 2), _proj_spec(BLK, W, 3), tab, tab],
        out_specs=[pl.BlockSpec((BLK, W), lambda n: (n, 0)), pl.BlockSpec((1, N_HEAD_AB, DH_AB, DH_AB), lambda n: (n, 0, 0, 0))],
        out_shape=[jax.ShapeDtypeStruct((T, W), bf16), jax.ShapeDtypeStruct((NC, N_HEAD_AB, DH_AB, DH_AB), f32)],
        scratch_shapes=[pltpu.VMEM((N_HEAD_AB, DH_AB, DH_AB), f32)],
        compiler_params=_cparams(dimension_semantics=("arbitrary",)),
    )(proj, proj, proj, proj, cos, sin)


def ret_bwd(proj, cos, sin, sall, dmix):
    T = proj.shape[0]
    NC = T // BLK
    W = N_HEAD_AB * DH_AB
    L = NC - 1

    def body(q_ref, k_ref, v_ref, g_ref, cos_ref, sin_ref, sall_ref, do_ref, dq_ref, dk_ref, dv_ref, dg_ref, dS):
        @pl.when(pl.program_id(0) == 0)
        def _():
            dS[...] = jnp.zeros_like(dS)
        cos_v, sin_v = cos_ref[...], sin_ref[...]
        for h in range(N_HEAD_AB):
            Sp = sall_ref[0, h]
            sl, q, k, v, P, ret, decay, xi, zeta, gc = _ret_head(q_ref, k_ref, v_ref, cos_v, sin_v, h, Sp)
            mu = jnp.mean(ret, axis=-1, keepdims=True)
            cen = ret - mu
            r = lax.rsqrt(jnp.mean(cen * cen, axis=-1, keepdims=True) + EPS)
            y = cen * r
            gate = g_ref[:, sl]
            sg = _sigmoid(gate)
            dout = do_ref[:, sl]
            dg_ref[:, sl] = dout * y * (sg * (1.0 + gate * (1.0 - sg)))
            dy = dout * (gate * sg)
            dO = r * (dy - jnp.mean(dy, axis=-1, keepdims=True) - y * jnp.mean(dy * y, axis=-1, keepdims=True))
            dSn = dS[h]
            dv_ref[:, sl] = _dot_tn(P, dO) + _dot(k * zeta, dSn)
            dP = _dot_nt(dO, v) * decay
            dq = _dot(dP, k) + _dot_nt(dO, Sp) * xi
            dk = _dot_tn(dP, q) + _dot_nt(v, dSn) * zeta
            dS[h] = dSn * gc + _dot_tn(q * xi, dO)
            dq_ref[:, sl] = _rot_bwd(dq, cos_v, sin_v)
            dk_ref[:, sl] = _rot_bwd(dk * DH_AB ** -0.5, cos_v, sin_v)

    tab = pl.BlockSpec((BLK, DH_AB), lambda n: (L - n, 0))
    out = pl.BlockSpec((BLK, W), lambda n: (L - n, 0))
    sh = jax.ShapeDtypeStruct((T, W), f32)
    return pl.pallas_call(
        body, name="ret_bwd", grid=(NC,),
        in_specs=[_proj_spec(BLK, W, 0, L), _proj_spec(BLK, W, 1, L), _proj_spec(BLK, W, 2, L), _proj_spec(BLK, W, 3, L), tab, tab,
                  pl.BlockSpec((1, N_HEAD_AB, DH_AB, DH_AB), lambda n: (L - n, 0, 0, 0)), _proj_spec(BLK, W, 0, L)],
        out_specs=[out] * 4, out_shape=[sh] * 4,
        scratch_shapes=[pltpu.VMEM((N_HEAD_AB, DH_AB, DH_AB), f32)],
        compiler_params=_cparams(dimension_semantics=("arbitrary",)),
    )(proj, proj, proj, proj, cos, sin, sall, dmix)


HALO = 8
GDN_K = 4
W_AB = N_HEAD_AB * DH_AB


def _gdn_rowwise(cq, ck, cv, gblk, alog, dtb, rmask):
    def l2n(x):
        return [x[:, DH_AB * h:DH_AB * (h + 1)] for h in range(N_HEAD_AB)]

    def norm(x):
        return x * lax.rsqrt(jnp.sum(x * x, axis=-1, keepdims=True) + EPS)

    qs = [norm(x) for x in l2n(_silu(cq))]
    ks = [norm(x) for x in l2n(_silu(ck))]
    lane = lax.broadcasted_iota(jnp.int32, gblk.shape, 1)
    beta = _sigmoid(gblk)
    g = -jnp.exp(alog) * _softplus(gblk + dtb)
    gates = jnp.where(lane < N_HEAD_AB, beta, jnp.where(lane < 2 * N_HEAD_AB, g, 0.0)) * rmask
    return qs, ks, _silu(cv), gates


def _row_mask(i, tr):
    rows = i * tr + lax.broadcasted_iota(jnp.int32, (tr, 1), 0)
    return (rows >= PAD).astype(f32)


def _conv_specs(tr, cols, nt, nxt=False):
    tiles = [pl.BlockSpec((tr, W_AB), functools.partial(lambda i, c: (i, c), c=c)) for c in cols]
    r = tr // HALO
    if nxt:
        halos = [pl.BlockSpec((HALO, W_AB), functools.partial(lambda i, c: (jnp.minimum((i + 1) * r, nt * r - 1), c), c=c)) for c in cols]
    else:
        halos = [pl.BlockSpec((HALO, W_AB), functools.partial(lambda i, c: (jnp.maximum(i * r - 1, 0), c), c=c)) for c in cols]
    return tiles, halos


def gdn_prep_fwd(proj, conv_w, alog, dtb):
    T = proj.shape[0]
    tr = BLK
    NT = T // tr

    def body(xq, xk, xv, hq, hk, hv, gb_ref, w_ref, al_ref, dt_ref, cq_o, ck_o, cv_o, q_o, k_o, v_o, gates_o, buf):
        i = pl.program_id(0)
        cs = []
        for p, (x_ref, h_ref, c_o) in enumerate(((xq, hq, cq_o), (xk, hk, ck_o), (xv, hv, cv_o))):
            buf[0:HALO, :] = jnp.where(i > 0, h_ref[...], 0.0)
            buf[HALO:, :] = x_ref[...]
            c = jnp.zeros((tr, W_AB), f32)
            for k in range(GDN_K):
                c = c + w_ref[k:k + 1, W_AB * p:W_AB * (p + 1)] * buf[pl.ds(HALO - GDN_K + 1 + k, tr), :]
            c_o[...] = c
            cs.append(c)
        qs, ks, v, gates = _gdn_rowwise(cs[0], cs[1], cs[2], gb_ref[...], al_ref[...], dt_ref[...], _row_mask(i, tr))
        for h in range(N_HEAD_AB):
            q_o[:, DH_AB * h:DH_AB * (h + 1)] = qs[h]
            k_o[:, DH_AB * h:DH_AB * (h + 1)] = ks[h]
        v_o[...] = v
        gates_o[...] = gates

    tiles, halos = _conv_specs(tr, (4, 5, 6), NT)
    vec = pl.BlockSpec((1, BLK), lambda i: (0, 0))
    wide = pl.BlockSpec((tr, W_AB), lambda i: (i, 0))
    sh = jax.ShapeDtypeStruct((T, W_AB), f32)
    return pl.pallas_call(
        body, name="gdn_prep_fwd", grid=(NT,),
        in_specs=tiles + halos + [pl.BlockSpec((tr, BLK), lambda i: (i, AB_IN_P // BLK - 1)),
                                  pl.BlockSpec((GDN_K, 3 * W_AB), lambda i: (0, 0)), vec, vec],
        out_specs=[wide] * 6 + [pl.BlockSpec((tr, BLK), lambda i: (i, 0))],
        out_shape=[sh] * 6 + [jax.ShapeDtypeStruct((T, BLK), f32)],
        scratch_shapes=[pltpu.VMEM((tr + HALO, W_AB), f32)],
        compiler_params=_cparams(dimension_semantics=("arbitrary",)),
    )(proj, proj, proj, proj, proj, proj, proj, conv_w, alog, dtb)


def gdn_prep_bwd(cq, ck, cv, proj, alog, dtb, dq, dk, dv, dgates):
    T = cq.shape[0]
    tr = BLK
    NT = T // tr

    def body(cq_r, ck_r, cv_r, gb_ref, al_ref, dt_ref, dq_r, dk_r, dv_r, dg_r, dcq_o, dck_o, dcv_o, dgb_o, dal_o, ddt_o):
        i = pl.program_id(0)
        mask = _row_mask(i, tr)
        _, vjp = jax.vjp(lambda a, b, c, d, e, f: _gdn_rowwise(a, b, c, d, e, f, mask),
                         cq_r[...], ck_r[...], cv_r[...], gb_ref[...], al_ref[...], dt_ref[...])
        heads = lambda r: [r[:, DH_AB * h:DH_AB * (h + 1)] for h in range(N_HEAD_AB)]
        dcq, dck, dcv, dgb, dal, ddt = vjp((heads(dq_r), heads(dk_r), dv_r[...], dg_r[...]))
        dcq_o[...] = dcq
        dck_o[...] = dck
        dcv_o[...] = dcv
        dgb_o[...] = dgb
        _acc8(dal_o, dal, i == 0)
        _acc8(ddt_o, ddt, i == 0)

    vec = pl.BlockSpec((1, BLK), lambda i: (0, 0))
    wide = pl.BlockSpec((tr, W_AB), lambda i: (i, 0))
    narrow = pl.BlockSpec((tr, BLK), lambda i: (i, 0))
    acc = pl.BlockSpec((8, BLK), lambda i: (0, 0))
    sh = jax.ShapeDtypeStruct((T, W_AB), f32)
    return pl.pallas_call(
        body, name="gdn_prep_bwd", grid=(NT,),
        in_specs=[wide] * 3 + [pl.BlockSpec((tr, BLK), lambda i: (i, AB_IN_P // BLK - 1)), vec, vec] + [wide] * 3 + [narrow],
        out_specs=[wide] * 3 + [narrow, acc, acc],
        out_shape=[sh] * 3 + [jax.ShapeDtypeStruct((T, BLK), f32)] + [jax.ShapeDtypeStruct((8, BLK), f32)] * 2,
        compiler_params=_cparams(dimension_semantics=("arbitrary",)),
    )(cq, ck, cv, proj, alog, dtb, dq, dk, dv, dgates)


def gdn_conv_bwd(dcq, dck, dcv, proj, conv_w):
    T = dcq.shape[0]
    tr = BLK
    NT = T // tr

    def body(dq_r, dk_r, dv_r, nq, nk, nv, xq, xk, xv, hq, hk, hv, w_ref, dxq_o, dxk_o, dxv_o, dw_o, bufd, bufx):
        i = pl.program_id(0)

        @pl.when(i == 0)
        def _():
            dw_o[...] = jnp.zeros_like(dw_o)
        parts = ((dq_r, nq, xq, hq, dxq_o), (dk_r, nk, xk, hk, dxk_o), (dv_r, nv, xv, hv, dxv_o))
        for p, (dc_r, n_r, x_r, h_r, dx_o) in enumerate(parts):
            dc = dc_r[...]
            bufd[0:tr, :] = dc
            bufd[tr:, :] = jnp.where(i < NT - 1, n_r[...], 0.0)
            bufx[0:HALO, :] = jnp.where(i > 0, h_r[...], 0.0)
            bufx[HALO:, :] = x_r[...]
            dx = jnp.zeros((tr, W_AB), f32)
            rows = []
            for k in range(GDN_K):
                dx = dx + w_ref[k:k + 1, W_AB * p:W_AB * (p + 1)] * bufd[pl.ds(GDN_K - 1 - k, tr), :]
                rows.append(jnp.sum(dc * bufx[pl.ds(HALO - GDN_K + 1 + k, tr), :], axis=0, keepdims=True))
            dx_o[...] = dx
            dw_o[:, W_AB * p:W_AB * (p + 1)] += jnp.concatenate(rows + [jnp.zeros((8 - GDN_K, W_AB), f32)], axis=0)

    wide = pl.BlockSpec((tr, W_AB), lambda i: (i, 0))
    r = tr // HALO
    nxt = pl.BlockSpec((HALO, W_AB), lambda i: (jnp.minimum((i + 1) * r, NT * r - 1), 0))
    tiles, halos = _conv_specs(tr, (4, 5, 6), NT)
    sh = jax.ShapeDtypeStruct((T, W_AB), f32)
    return pl.pallas_call(
        body, name="gdn_conv_bwd", grid=(NT,),
        in_specs=[wide] * 3 + [nxt] * 3 + tiles + halos + [pl.BlockSpec((GDN_K, 3 * W_AB), lambda i: (0, 0))],
        out_specs=[wide] * 3 + [pl.BlockSpec((8, 3 * W_AB), lambda i: (0, 0))],
        out_shape=[sh] * 3 + [jax.ShapeDtypeStruct((8, 3 * W_AB), f32)],
        scratch_shapes=[pltpu.VMEM((tr + HALO, W_AB), f32), pltpu.VMEM((tr + HALO, W_AB), f32)],
        compiler_params=_cparams(dimension_semantics=("arbitrary",)),
    )(dcq, dck, dcv, dcq, dck, dcv, proj, proj, proj, proj, proj, proj, conv_w)


def _tri_sum(x, upper):
    n = x.shape[0]
    r = lax.broadcasted_iota(jnp.int32, (n, n), 0)
    c = lax.broadcasted_iota(jnp.int32, (n, n), 1)
    tri = ((r <= c) if upper else (r >= c)).astype(bf16)
    hi = x.astype(bf16)
    lo = (x - hi.astype(f32)).astype(bf16)
    return jnp.dot(tri, hi, preferred_element_type=f32) + jnp.dot(tri, lo, preferred_element_type=f32)


@jax.custom_vjp
def _cumsum_rows(x):
    return _tri_sum(x, False)


_cumsum_rows.defvjp(lambda x: (_tri_sum(x, False), None), lambda _, g: (_tri_sum(g, True),))


@jax.custom_vjp
def _unit_lower_inv(a):
    n = a.shape[0]
    eye = (lax.broadcasted_iota(jnp.int32, (n, n), 0) == lax.broadcasted_iota(jnp.int32, (n, n), 1)).astype(f32)
    b = -a
    x = eye + b
    p = b
    for _ in range(int(np.log2(n)) - 1):
        p = _dot(p, p)
        x = x + _dot(x, p)
    return x


def _unit_lower_inv_fwd(a):
    t = _unit_lower_inv(a)
    return t, t


def _unit_lower_inv_bwd(t, dt):
    return (-_dot_nt(_dot_tn(t, dt), t),)


_unit_lower_inv.defvjp(_unit_lower_inv_fwd, _unit_lower_inv_bwd)


def _gdn_chunk(qs, ks, vs, gates, zs, onorm, Ss):
    C = gates.shape[0]
    ri = lax.broadcasted_iota(jnp.int32, (C, C), 0)
    ci = lax.broadcasted_iota(jnp.int32, (C, C), 1)
    incl, strict = ri >= ci, ri > ci
    gcum = _cumsum_rows(gates)
    gcum_t = gcum.T
    lane = lax.broadcasted_iota(jnp.int32, gates.shape, 1)
    sub = lax.broadcasted_iota(jnp.int32, gcum_t.shape, 0)
    last = lax.broadcasted_iota(jnp.int32, (C, 1), 0) == C - 1
    outs, nxt = [], []
    for h in range(N_HEAD_AB):
        bcol = jnp.sum(jnp.where(lane == h, gates, 0.0), axis=1, keepdims=True)
        gcol = jnp.sum(jnp.where(lane == N_HEAD_AB + h, gcum, 0.0), axis=1, keepdims=True)
        grow = jnp.sum(jnp.where(sub == N_HEAD_AB + h, gcum_t, 0.0), axis=0, keepdims=True)
        gl = jnp.sum(jnp.where(last, gcol, 0.0), axis=0, keepdims=True)
        decay = jnp.where(incl, jnp.exp(jnp.where(incl, gcol - grow, 0.0)), 0.0)
        q = qs[h] * DH_AB ** -0.5
        k, v, S = ks[h], vs[h], Ss[h]
        kb = k * bcol
        a = jnp.where(strict, _dot_nt(kb, k) * decay, 0.0)
        t = _unit_lower_inv(a)
        eg = jnp.exp(gcol)
        u = _dot(t, v * bcol)
        w = _dot(t, kb * eg)
        qk = jnp.where(incl, _dot_nt(q, k) * decay, 0.0)
        v_new = u - _dot(w, S)
        o = _dot(q * eg, S) + _dot(qk, v_new)
        nxt.append(S * jnp.exp(gl) + _dot_tn(k * jnp.exp(gl - gcol), v_new))
        outs.append(_rms(o, onorm) * _silu(zs[h]))
    return outs, nxt


def _heads(ref, r0=None):
    rows = slice(None) if r0 is None else slice(r0, r0 + GDN_C)
    return [ref[rows, DH_AB * h:DH_AB * (h + 1)] for h in range(N_HEAD_AB)]


def _gdn_per_step(n_chunks):
    return next(p for p in (5, 2, 1) if n_chunks % p == 0)


def gdn_chunk_fwd(q, k, v, gates, proj, onorm):
    T = q.shape[0]
    P = _gdn_per_step(q.shape[0] // GDN_C)
    C = GDN_C * P
    NC = T // GDN_C

    def body(q_r, k_r, v_r, g_r, z_r, on_r, o_ref, sall_ref, S):
        @pl.when(pl.program_id(0) == 0)
        def _():
            S[...] = jnp.zeros_like(S)
        Ss = [S[h] for h in range(N_HEAD_AB)]
        for j in range(P):
            r0 = GDN_C * j
            for h in range(N_HEAD_AB):
                sall_ref[j, h] = Ss[h]
            outs, Ss = _gdn_chunk(_heads(q_r, r0), _heads(k_r, r0), _heads(v_r, r0), g_r[r0:r0 + GDN_C, :], _heads(z_r, r0), on_r[...], Ss)
            for h in range(N_HEAD_AB):
                o_ref[r0:r0 + GDN_C, DH_AB * h:DH_AB * (h + 1)] = outs[h].astype(o_ref.dtype)
        for h in range(N_HEAD_AB):
            S[h] = Ss[h]

    wide = pl.BlockSpec((C, W_AB), lambda n: (n, 0))
    return pl.pallas_call(
        body, name="gdn_chunk_fwd", grid=(NC // P,),
        in_specs=[wide] * 3 + [pl.BlockSpec((C, BLK), lambda n: (n, 0)), pl.BlockSpec((C, W_AB), lambda n: (n, 7)),
                               pl.BlockSpec((1, DH_AB), lambda n: (0, 0))],
        out_specs=[wide, pl.BlockSpec((P, N_HEAD_AB, DH_AB, DH_AB), lambda n: (n, 0, 0, 0))],
        out_shape=[jax.ShapeDtypeStruct((T, W_AB), bf16), jax.ShapeDtypeStruct((NC, N_HEAD_AB, DH_AB, DH_AB), f32)],
        scratch_shapes=[pltpu.VMEM((N_HEAD_AB, DH_AB, DH_AB), f32)],
        compiler_params=_cparams(dimension_semantics=("arbitrary",)),
    )(q, k, v, gates, proj, onorm)


def gdn_chunk_bwd(q, k, v, gates, proj, onorm, sall, dmix):
    T = q.shape[0]
    P = _gdn_per_step(q.shape[0] // GDN_C)
    C = GDN_C * P
    NC = T // GDN_C
    L = NC // P - 1

    def body(q_r, k_r, v_r, g_r, z_r, on_r, sall_r, do_r, dq_o, dk_o, dv_o, dz_o, dg_o, don_o, dS):
        @pl.when(pl.program_id(0) == 0)
        def _():
            dS[...] = jnp.zeros_like(dS)
        dSs = [dS[h] for h in range(N_HEAD_AB)]
        don_sum = jnp.zeros((1, DH_AB), f32)
        for j in reversed(range(P)):
            r0 = GDN_C * j
            Ss = [sall_r[j, h] for h in range(N_HEAD_AB)]
            _, vjp = jax.vjp(_gdn_chunk, _heads(q_r, r0), _heads(k_r, r0), _heads(v_r, r0), g_r[r0:r0 + GDN_C, :], _heads(z_r, r0),
                             on_r[...], Ss)
            dqs, dks, dvs, dg, dzs, don, dSs = vjp((_heads(do_r, r0), dSs))
            for h in range(N_HEAD_AB):
                sl = slice(DH_AB * h, DH_AB * (h + 1))
                dq_o[r0:r0 + GDN_C, sl] = dqs[h]
                dk_o[r0:r0 + GDN_C, sl] = dks[h]
                dv_o[r0:r0 + GDN_C, sl] = dvs[h]
                dz_o[r0:r0 + GDN_C, sl] = dzs[h]
            dg_o[r0:r0 + GDN_C, :] = dg
            don_sum = don_sum + don
        for h in range(N_HEAD_AB):
            dS[h] = dSs[h]
        _acc8(don_o, don_sum, pl.program_id(0) == 0)

    wide = pl.BlockSpec((C, W_AB), lambda n: (L - n, 0))
    sh = jax.ShapeDtypeStruct((T, W_AB), f32)
    return pl.pallas_call(
        body, name="gdn_chunk_bwd", grid=(NC // P,),
        in_specs=[wide] * 3 + [pl.BlockSpec((C, BLK), lambda n: (L - n, 0)), pl.BlockSpec((C, W_AB), lambda n: (L - n, 7)),
                               pl.BlockSpec((1, DH_AB), lambda n: (0, 0)),
                               pl.BlockSpec((P, N_HEAD_AB, DH_AB, DH_AB), lambda n: (L - n, 0, 0, 0)),
                               pl.BlockSpec((C, W_AB), lambda n: (L - n, 1))],
        out_specs=[wide] * 4 + [pl.BlockSpec((C, BLK), lambda n: (L - n, 0)), pl.BlockSpec((8, DH_AB), lambda n: (0, 0))],
        out_shape=[sh] * 4 + [jax.ShapeDtypeStruct((T, BLK), f32), jax.ShapeDtypeStruct((8, DH_AB), f32)],
        scratch_shapes=[pltpu.VMEM((N_HEAD_AB, DH_AB, DH_AB), f32)],
        compiler_params=_cparams(dimension_semantics=("arbitrary",)),
    )(q, k, v, gates, proj, onorm, sall, dmix)


DH_CD = 64
SWA_G = 4
SWA_KV = 2
W_CD = 512


def _swa_block(q_ref, km, kp, kc, vm, vp, vc, sinks, g, n):
    scale = DH_CD ** -0.5
    ks = slice(DH_CD * g, DH_CD * (g + 1))
    Q = jnp.concatenate([q_ref[:, DH_CD * (SWA_G * g + j):DH_CD * (SWA_G * g + j + 1)] for j in range(SWA_G)], axis=0) * scale
    K3 = jnp.concatenate([km[:, ks], kp[:, ks], kc[:, ks]], axis=0)
    V3 = jnp.concatenate([vm[:, ks], vp[:, ks], vc[:, ks]], axis=0)
    s = _dot_nt(Q, K3)
    shp = s.shape
    row = lax.broadcasted_iota(jnp.int32, shp, 0)
    col = lax.broadcasted_iota(jnp.int32, shp, 1)
    i, part, j = row % BLK, col // BLK, col % BLK
    meta = (part == 0) & (j >= PAD) & ((j <= i) | (n > 0))
    prev = (part == 1) & (j > i) & (n >= 2)
    cur = (part == 2) & (j <= i) & (n >= 1)
    valid = meta | prev | cur
    grp = lax.broadcasted_iota(jnp.int32, (shp[0], 1), 0) // BLK
    sink = jnp.zeros((shp[0], 1), f32)
    for jj in range(SWA_G):
        sink = jnp.where(grp == jj, sinks[SWA_G * g + jj], sink)
    m = jnp.maximum(jnp.max(jnp.where(valid, s, NEG), axis=1, keepdims=True), sink)
    p = jnp.where(valid, jnp.exp(jnp.where(valid, s - m, 0.0)), 0.0)
    es = jnp.exp(sink - m)
    denom = jnp.sum(p, axis=1, keepdims=True) + es
    return Q, K3, V3, p / denom, es / denom, grp


def _swa_in_specs(rev=None):
    row = (lambda n: n) if rev is None else (lambda n: rev - n)
    kcol, vcol = 512 // BLK, 640 // BLK
    specs = [pl.BlockSpec((BLK, W_CD), lambda n: (row(n), 0))]
    for col in (kcol, vcol):
        specs += [pl.BlockSpec((BLK, BLK), functools.partial(lambda n, c: (0, c), c=col)),
                  pl.BlockSpec((BLK, BLK), functools.partial(lambda n, c: (jnp.maximum(row(n) - 1, 0), c), c=col)),
                  pl.BlockSpec((BLK, BLK), functools.partial(lambda n, c: (row(n), c), c=col))]
    return specs + [pl.BlockSpec(memory_space=pltpu.SMEM)]


def swa_fwd(proj, sinks):
    T = proj.shape[0]
    NB = T // BLK

    def body(q_ref, km, kp, kc, vm, vp, vc, sinks_ref, o_ref):
        n = pl.program_id(0)
        for g in range(SWA_KV):
            Q, K3, V3, pn, ps, grp = _swa_block(q_ref, km, kp, kc, vm, vp, vc, sinks_ref, g, n)
            o = _dot(pn, V3)
            for j in range(SWA_G):
                hd = SWA_G * g + j
                o_ref[:, DH_CD * hd:DH_CD * (hd + 1)] = o[BLK * j:BLK * (j + 1), :]

    return pl.pallas_call(
        body, name="swa_fwd", grid=(NB,), in_specs=_swa_in_specs(),
        out_specs=pl.BlockSpec((BLK, W_CD), lambda n: (n, 0)), out_shape=jax.ShapeDtypeStruct((T, W_CD), f32),
        compiler_params=_cparams(dimension_semantics=("arbitrary",)),
    )(proj, proj, proj, proj, proj, proj, proj, sinks)


def swa_bwd(proj, sinks, dmix):
    T = proj.shape[0]
    NB = T // BLK
    KV = 2 * SWA_KV * DH_CD

    def body(q_ref, km, kp, kc, vm, vp, vc, sinks_ref, do_ref, dq_ref, cur_ref, prev_ref, meta_ref, ds_ref):
        n = pl.program_id(0)

        @pl.when(n == 0)
        def _():
            meta_ref[...] = jnp.zeros_like(meta_ref)
            ds_ref[...] = jnp.zeros_like(ds_ref)
        rows = []
        for g in range(SWA_KV):
            Q, K3, V3, pn, ps, grp = _swa_block(q_ref, km, kp, kc, vm, vp, vc, sinks_ref, g, n)
            dO = jnp.concatenate([do_ref[:, DH_CD * (SWA_G * g + j):DH_CD * (SWA_G * g + j + 1)] for j in range(SWA_G)], axis=0)
            dP = _dot_nt(dO, V3)
            delta = jnp.sum(pn * dP, axis=1, keepdims=True)
            dS = pn * (dP - delta)
            dQ = _dot(dS, K3) * DH_CD ** -0.5
            dK3 = _dot_tn(dS, Q)
            dV3 = _dot_tn(pn, dO)
            dsk = -ps * delta
            for j in range(SWA_G):
                hd = SWA_G * g + j
                dq_ref[:, DH_CD * hd:DH_CD * (hd + 1)] = dQ[BLK * j:BLK * (j + 1), :]
                rows.append(jnp.broadcast_to(jnp.sum(jnp.where(grp == j, dsk, 0.0), axis=0, keepdims=True), (1, BLK)))
            kcols = slice(DH_CD * g, DH_CD * (g + 1))
            vcols = slice(SWA_KV * DH_CD + DH_CD * g, SWA_KV * DH_CD + DH_CD * (g + 1))
            meta_ref[:, kcols] += dK3[0:BLK]
            meta_ref[:, vcols] += dV3[0:BLK]
            prev_ref[:, kcols] = dK3[BLK:2 * BLK]
            prev_ref[:, vcols] = dV3[BLK:2 * BLK]
            cur_ref[:, kcols] = dK3[2 * BLK:]
            cur_ref[:, vcols] = dV3[2 * BLK:]
        ds_ref[...] += jnp.concatenate(rows, axis=0)

    kv = pl.BlockSpec((BLK, KV), lambda n: (n, 0))
    return pl.pallas_call(
        body, name="swa_bwd", grid=(NB,),
        in_specs=_swa_in_specs() + [pl.BlockSpec((BLK, W_CD), lambda n: (n, 0))],
        out_specs=[pl.BlockSpec((BLK, W_CD), lambda n: (n, 0)), kv, kv, pl.BlockSpec((BLK, KV), lambda n: (0, 0)),
                   pl.BlockSpec((8, BLK), lambda n: (0, 0))],
        out_shape=[jax.ShapeDtypeStruct((T, W_CD), f32), jax.ShapeDtypeStruct((T, KV), f32), jax.ShapeDtypeStruct((T, KV), f32),
                   jax.ShapeDtypeStruct((BLK, KV), f32), jax.ShapeDtypeStruct((8, BLK), f32)],
        compiler_params=_cparams(dimension_semantics=("arbitrary",)),
    )(proj, proj, proj, proj, proj, proj, proj, sinks, dmix)


SB_PAIR = 2


def _sb_consts():
    r = lax.broadcasted_iota(jnp.int32, (BLK, BLK), 0)
    c = lax.broadcasted_iota(jnp.int32, (BLK, BLK), 1)
    return r, c, (r > c).astype(bf16), (r >= c).astype(bf16)


def _sb_block(q, kb, n, m, r, c):
    z = _dot_nt(q, kb)
    valid = ((m * BLK + c) < (n * BLK + r)) & ((m * BLK + c) >= PAD)
    sp = _softplus(z)
    return z, valid, jnp.where(valid, -sp, 0.0), sp


def _sb_specs(T):
    qcol, kcol, vcol = 768 // BLK, 1280 // BLK, 1792 // BLK
    return [pl.BlockSpec((BLK, BLK), lambda hp, n: (n, qcol + hp)),
            pl.BlockSpec((T, BLK), lambda hp, n: (0, kcol + hp)),
            pl.BlockSpec((T, BLK), lambda hp, n: (0, vcol + hp))]


def sb_fwd(proj):
    T = proj.shape[0]
    NB = T // BLK

    def body(q_ref, k_ref, v_ref, o_ref):
        n = pl.program_id(1)
        r, c, m_gt, _ = _sb_consts()
        heads = [slice(DH_CD * hh, DH_CD * (hh + 1)) for hh in range(SB_PAIR)]
        qs = [q_ref[:, cols] * DH_CD ** -0.5 for cols in heads]

        def cond(carry):
            m, runs, _ = carry
            return jnp.logical_and(m >= 0, jnp.max(jnp.maximum(runs[0], runs[1])) > SB_EXIT)

        def step(carry):
            m, runs, accs = carry
            off = pl.multiple_of(m * BLK, BLK)
            new_runs, new_accs = [], []
            for hh, cols in enumerate(heads):
                kb = k_ref[pl.ds(off, BLK), cols]
                vb = v_ref[pl.ds(off, BLK), cols]
                z, valid, l, sp = _sb_block(qs[hh], kb, n, m, r, c)
                e = (z - sp) + _dot2(l, m_gt) + runs[hh]
                a = jnp.where(valid, jnp.exp(jnp.where(valid, e, 0.0)), 0.0)
                new_runs.append(runs[hh] + jnp.sum(l, axis=1, keepdims=True))
                new_accs.append(accs[hh] + _dot(a, vb))
            return m - 1, tuple(new_runs), tuple(new_accs)

        zero = jnp.zeros((BLK, 1), f32)
        acc0 = jnp.zeros((BLK, DH_CD), f32)
        _, _, accs = lax.while_loop(cond, step, (n, (zero, zero), (acc0, acc0)))
        for hh, cols in enumerate(heads):
            o_ref[:, cols] = accs[hh]

    return pl.pallas_call(
        body, name="sb_fwd", grid=(W_CD // BLK, NB), in_specs=_sb_specs(T),
        out_specs=pl.BlockSpec((BLK, BLK), lambda hp, n: (n, hp)), out_shape=jax.ShapeDtypeStruct((T, W_CD), f32),
        compiler_params=_cparams(dimension_semantics=("arbitrary", "arbitrary")),
    )(proj, proj, proj)


def sb_bwd(proj, o, dmix):
    T = proj.shape[0]
    NB = T // BLK

    def body(q_ref, k_ref, v_ref, o_ref, do_ref, dq_ref, dk_ref, dv_ref):
        n = pl.program_id(1)

        @pl.when(n == 0)
        def _():
            dk_ref[...] = jnp.zeros_like(dk_ref)
            dv_ref[...] = jnp.zeros_like(dv_ref)
        r, c, m_gt, m_ge = _sb_consts()
        heads = [slice(DH_CD * hh, DH_CD * (hh + 1)) for hh in range(SB_PAIR)]
        qs = [q_ref[:, cols] * DH_CD ** -0.5 for cols in heads]
        dOs = [do_ref[:, cols].astype(bf16) for cols in heads]
        deltas = [jnp.sum(dOs[hh].astype(f32) * o_ref[:, cols], axis=1, keepdims=True) for hh, cols in enumerate(heads)]

        def cond(carry):
            m, runs = carry[0], carry[1]
            return jnp.logical_and(m >= 0, jnp.max(jnp.maximum(runs[0], runs[1])) > SB_EXIT)

        def step(carry):
            m, runs, runs_e, dqs = carry
            off = pl.multiple_of(m * BLK, BLK)
            new_runs, new_runs_e, new_dqs, dks, dvs = [], [], [], [], []
            for hh, cols in enumerate(heads):
                kb = k_ref[pl.ds(off, BLK), cols]
                vb = v_ref[pl.ds(off, BLK), cols]
                z, valid, l, sp = _sb_block(qs[hh], kb, n, m, r, c)
                e = (z - sp) + _dot2(l, m_gt) + runs[hh]
                a = jnp.where(valid, jnp.exp(jnp.where(valid, e, 0.0)), 0.0).astype(bf16)
                E = a.astype(f32) * _dot_nt(dOs[hh], vb)
                F = deltas[hh] - runs_e[hh] - _dot2(E, m_ge)
                sig = jnp.exp(z - sp)
                live = jnp.logical_and(valid, jnp.max(runs[hh]) > SB_EXIT)
                dz = jnp.where(live, E * (1.0 - sig) - F * sig, 0.0)
                dks.append(_dot_tn(dz, qs[hh]))
                dvs.append(_dot_tn(a, dOs[hh]))
                new_runs.append(runs[hh] + jnp.sum(l, axis=1, keepdims=True))
                new_runs_e.append(runs_e[hh] + jnp.sum(E, axis=1, keepdims=True))
                new_dqs.append(dqs[hh] + _dot(dz, kb))
            dk_ref[pl.ds(off, BLK), :] += jnp.concatenate(dks, axis=1)
            dv_ref[pl.ds(off, BLK), :] += jnp.concatenate(dvs, axis=1)
            return m - 1, tuple(new_runs), tuple(new_runs_e), tuple(new_dqs)

        zero = jnp.zeros((BLK, 1), f32)
        dq0 = jnp.zeros((BLK, DH_CD), f32)
        res = lax.while_loop(cond, step, (n, (zero, zero), (zero, zero), (dq0, dq0)))
        for hh, cols in enumerate(heads):
            dq_ref[:, cols] = res[3][hh] * DH_CD ** -0.5

    blk = pl.BlockSpec((BLK, BLK), lambda hp, n: (n, hp))
    full = pl.BlockSpec((T, BLK), lambda hp, n: (0, hp))
    sh = jax.ShapeDtypeStruct((T, W_CD), f32)
    return pl.pallas_call(
        body, name="sb_bwd", grid=(W_CD // BLK, NB),
        in_specs=_sb_specs(T) + [blk, pl.BlockSpec((BLK, BLK), lambda hp, n: (n, W_CD // BLK + hp))],
        out_specs=[blk, full, full], out_shape=[sh] * 3,
        compiler_params=_cparams(dimension_semantics=("arbitrary", "arbitrary")),
    )(proj, proj, proj, o, dmix)


def cd_assemble(dcq, cur, prev, meta, dsq, dsk, dsv):
    T = dcq.shape[0]
    NB = T // BLK
    KV = cur.shape[1]

    def body(dcq_r, cur_r, nxt_r, meta_r, dsq_r, dsk_r, dsv_r, o_ref):
        n = pl.program_id(0)
        kv = cur_r[...] + jnp.where(n < NB - 1, nxt_r[...], 0.0) + jnp.where(n == 0, meta_r[...], 0.0)
        o_ref[:, 0:W_CD] = dcq_r[...].astype(o_ref.dtype)
        o_ref[:, W_CD:W_CD + KV] = kv.astype(o_ref.dtype)
        for j, ref in enumerate((dsq_r, dsk_r, dsv_r)):
            o_ref[:, W_CD + KV + W_CD * j:W_CD + KV + W_CD * (j + 1)] = ref[...].astype(o_ref.dtype)

    wide = pl.BlockSpec((BLK, W_CD), lambda n: (n, 0))
    return pl.pallas_call(
        body, name="cd_assemble", grid=(NB,),
        in_specs=[wide, pl.BlockSpec((BLK, KV), lambda n: (n, 0)), pl.BlockSpec((BLK, KV), lambda n: (jnp.minimum(n + 1, NB - 1), 0)),
                  pl.BlockSpec((BLK, KV), lambda n: (0, 0)), wide, wide, wide],
        out_specs=pl.BlockSpec((BLK, CD_IN), lambda n: (n, 0)), out_shape=jax.ShapeDtypeStruct((T, CD_IN), bf16),
        compiler_params=_cparams(dimension_semantics=("arbitrary",)),
    )(dcq, cur, prev, meta, dsq, dsk, dsv)


def loss_grad(h, target):
    T, Dm = h.shape
    NB = T // BLK

    def body(h_ref, t_ref, dh_ref, l_ref):
        n = pl.program_id(0)

        @pl.when(n == 0)
        def _():
            dh_ref[...] = jnp.zeros_like(dh_ref)
            l_ref[...] = jnp.zeros_like(l_ref)

        @pl.when(n > 0)
        def _():
            err = h_ref[...] - t_ref[...]
            dh_ref[...] = err * (1.0 / Dm)
            part = 0.5 * jnp.sum(jnp.mean(err * err, axis=-1, keepdims=True), axis=0, keepdims=True)
            l_ref[...] += jnp.broadcast_to(part, l_ref.shape)

    row = pl.BlockSpec((BLK, Dm), lambda n: (n, 0))
    return pl.pallas_call(
        body, name="loss_grad", grid=(NB,),
        in_specs=[row, pl.BlockSpec((BLK, Dm), lambda n: (jnp.maximum(n - 1, 0), 0))],
        out_specs=[row, pl.BlockSpec((8, BLK), lambda n: (0, 0))],
        out_shape=[jax.ShapeDtypeStruct((T, Dm), f32), jax.ShapeDtypeStruct((8, BLK), f32)],
        compiler_params=_cparams(dimension_semantics=("arbitrary",)),
    )(h, target)


SUM_ROWS = 256
_MESH = pl.DeviceIdType.MESH
_ANY = pl.BlockSpec(memory_space=pl.ANY)


def _place():
    return lax.axis_index("x"), lax.axis_index("y"), lax.axis_index("c")


def _other_chips(x, y):
    return [(1 - x, y, 2 * (1 - x) + y), (x, 1 - y, 2 * x + 1 - y), (1 - x, 1 - y, 2 * (1 - x) + 1 - y)]


def gather_weights(wbuf, sbuf):
    def body(w_ref, s_ref, out_ref, outs_ref, send_sems, recv_sems):
        x, y, c = _place()
        p = 2 * x + y
        chips = _other_chips(x, y)
        sibling = (x, y, 1 - c)

        def copy(k, src, dst, to):
            return pltpu.make_async_remote_copy(src_ref=src, dst_ref=dst, send_sem=send_sems.at[k], recv_sem=recv_sems.at[k],
                                                device_id=to, device_id_type=_MESH)

        sends = [copy(9, w_ref, out_ref.at[p], sibling), copy(10, s_ref, outs_ref.at[p], sibling)]
        for j, (qx, qy, q) in enumerate(chips):
            sends.append(copy(j, w_ref.at[c], out_ref.at[p, c], (qx, qy, c)))
            sends.append(copy(3 + j, s_ref, outs_ref.at[p], (qx, qy, c)))
        for cp in sends:
            cp.start()
        for j, (qx, qy, q) in enumerate(chips):
            copy(j, w_ref.at[c], out_ref.at[q, c], (qx, qy, c)).wait_recv()
            fwd = copy(6 + j, out_ref.at[q, c], out_ref.at[q, c], sibling)
            fwd.start()
            sends.append(fwd)
        for j, (qx, qy, q) in enumerate(chips):
            copy(3 + j, s_ref, outs_ref.at[q], (qx, qy, c)).wait_recv()
            copy(6 + j, out_ref.at[q, 1 - c], out_ref.at[q, 1 - c], sibling).wait_recv()
        copy(9, w_ref, out_ref.at[p], sibling).wait_recv()
        copy(10, s_ref, outs_ref.at[p], sibling).wait_recv()
        for cp in sends:
            cp.wait_send()

    return pl.pallas_call(
        body, name="gather_weights", in_specs=[_ANY, _ANY], out_specs=[_ANY, _ANY],
        out_shape=[jax.ShapeDtypeStruct((4,) + wbuf.shape, wbuf.dtype), jax.ShapeDtypeStruct((4,) + sbuf.shape, sbuf.dtype)],
        scratch_shapes=[pltpu.SemaphoreType.DMA((11,)), pltpu.SemaphoreType.DMA((11,))],
    )(wbuf, sbuf)


def pair_exchange(g):
    S, _, H, Cw = g.shape

    def body(g_ref, out_ref, send_sem, recv_sem):
        x, y, c = _place()
        cp = pltpu.make_async_remote_copy(src_ref=g_ref.at[:, 1 - c], dst_ref=out_ref, send_sem=send_sem, recv_sem=recv_sem,
                                          device_id=(x, y, 1 - c), device_id_type=_MESH)
        cp.start()
        cp.wait()

    return pl.pallas_call(
        body, name="pair_exchange", in_specs=[_ANY], out_specs=_ANY, out_shape=jax.ShapeDtypeStruct((S, H, Cw), g.dtype),
        scratch_shapes=[pltpu.SemaphoreType.DMA, pltpu.SemaphoreType.DMA],
    )(g)


def pair_sum(g, got, c):
    S, _, H, Cw = g.shape
    tb = 3 * SUM_ROWS if H % (3 * SUM_ROWS) == 0 else SUM_ROWS

    def body(c_ref, a_ref, b_ref, o_ref):
        o_ref[...] = (a_ref[...].astype(f32) + b_ref[...].astype(f32)).astype(o_ref.dtype)

    spec = pl.BlockSpec((None, tb, Cw), lambda s, i, c_ref: (s, i, 0))
    return pl.pallas_call(
        body, name="pair_sum",
        grid_spec=pltpu.PrefetchScalarGridSpec(
            num_scalar_prefetch=1, grid=(S, H // tb),
            in_specs=[pl.BlockSpec((None, None, tb, Cw), lambda s, i, c_ref: (s, c_ref[0], i, 0)), spec], out_specs=spec),
        out_shape=jax.ShapeDtypeStruct((S, H, Cw), g.dtype),
        compiler_params=_cparams(dimension_semantics=("arbitrary", "arbitrary")),
    )(c, g, got)


def chip_exchange(hsum):
    S, H, Cw = hsum.shape

    def body(h_ref, out_ref, send_sems, recv_sems):
        x, y, c = _place()
        sends = []
        for j, (qx, qy, q) in enumerate(_other_chips(x, y)):
            cp = pltpu.make_async_remote_copy(src_ref=h_ref.at[q], dst_ref=out_ref.at[j], send_sem=send_sems.at[j],
                                              recv_sem=recv_sems.at[j], device_id=(qx, qy, c), device_id_type=_MESH)
            cp.start()
            sends.append(cp)
        for cp in sends:
            cp.wait()

    return pl.pallas_call(
        body, name="chip_exchange", in_specs=[_ANY], out_specs=_ANY, out_shape=jax.ShapeDtypeStruct((3, H, Cw), hsum.dtype),
        scratch_shapes=[pltpu.SemaphoreType.DMA((3,)), pltpu.SemaphoreType.DMA((3,))],
    )(hsum)


def chip_sum(hsum, parts, p):
    S, H, Cw = parts.shape
    tb = 3 * SUM_ROWS if H % (3 * SUM_ROWS) == 0 else SUM_ROWS

    def body(p_ref, own_ref, parts_ref, o_ref):
        acc = own_ref[...].astype(f32)
        for s in range(S):
            acc = acc + parts_ref[s].astype(f32)
        o_ref[...] = acc

    return pl.pallas_call(
        body, name="chip_sum",
        grid_spec=pltpu.PrefetchScalarGridSpec(
            num_scalar_prefetch=1, grid=(H // tb,),
            in_specs=[pl.BlockSpec((None, tb, Cw), lambda i, p_ref: (p_ref[0], i, 0)), pl.BlockSpec((S, tb, Cw), lambda i, p_ref: (0, i, 0))],
            out_specs=pl.BlockSpec((tb, Cw), lambda i, p_ref: (i, 0))),
        out_shape=jax.ShapeDtypeStruct((H, Cw), f32),
        compiler_params=_cparams(dimension_semantics=("arbitrary",)),
    )(p, hsum, parts)


def pair_gather(rsum):
    def body(r_ref, out_ref, send_sem, recv_sem):
        x, y, c = _place()
        cp = pltpu.make_async_remote_copy(src_ref=r_ref, dst_ref=out_ref, send_sem=send_sem, recv_sem=recv_sem,
                                          device_id=(x, y, 1 - c), device_id_type=_MESH)
        cp.start()
        cp.wait()

    return pl.pallas_call(
        body, name="pair_gather", in_specs=[_ANY], out_specs=_ANY, out_shape=jax.ShapeDtypeStruct(rsum.shape, rsum.dtype),
        scratch_shapes=[pltpu.SemaphoreType.DMA, pltpu.SemaphoreType.DMA],
    )(rsum)


def small_reduce(src):
    S, RS, Cw = src.shape

    def body(src_ref, out_ref, recv, send_sems, recv_sems):
        x, y, c = _place()
        me = 4 * x + 2 * y + c
        p = 2 * x + y
        recv[me] = src_ref[p]
        flips = [(fx, fy, fc) for fx in (0, 1) for fy in (0, 1) for fc in (0, 1)][1:]
        sends = []
        for k, (fx, fy, fc) in enumerate(flips):
            tx, ty, tc = (1 - x if fx else x), (1 - y if fy else y), (1 - c if fc else c)
            cp = pltpu.make_async_remote_copy(src_ref=src_ref.at[2 * tx + ty], dst_ref=recv.at[me], send_sem=send_sems.at[k],
                                              recv_sem=recv_sems.at[me], device_id=(tx, ty, tc), device_id_type=_MESH)
            cp.start()
            sends.append(cp)
        for k, (fx, fy, fc) in enumerate(flips):
            tx, ty, tc = (1 - x if fx else x), (1 - y if fy else y), (1 - c if fc else c)
            frm = 4 * tx + 2 * ty + tc
            pltpu.make_async_remote_copy(src_ref=src_ref.at[p], dst_ref=recv.at[frm], send_sem=send_sems.at[k],
                                         recv_sem=recv_sems.at[frm], device_id=(tx, ty, tc), device_id_type=_MESH).wait_recv()
        for cp in sends:
            cp.wait_send()
        acc = recv[0]
        for d in range(1, 8):
            acc = acc + recv[d]
        out_ref[...] = acc

    vm = pl.BlockSpec(memory_space=pltpu.VMEM)
    return pl.pallas_call(
        body, name="small_reduce", in_specs=[vm], out_specs=vm, out_shape=jax.ShapeDtypeStruct((RS, Cw), f32),
        scratch_shapes=[pltpu.VMEM((8, RS, Cw), f32), pltpu.SemaphoreType.DMA((7,)), pltpu.SemaphoreType.DMA((8,))],
    )(src)


def _row(v):
    return v.reshape(1, -1)


def _ffn_fwd(h, g_pre, g_post, wg, wu, wd):
    u, G, U, a = ffn_up(h, _row(g_pre), wg, wu)
    y, h_new = proj_norm_res(a, wd, h, _row(g_post), 0.5)
    return h_new, (h, u, G, U, a, y)


def _ffn_bwd(dh, saved, g_pre, g_post, wg, wu, wd):
    h, u, G, U, a, y = saved
    F = wg.shape[2]
    dy, dg_post, dG, dU = ffn_bwd_act(y, _row(g_post), dh, 0.5, wd, G, U)
    dwd = mm_tn(a, dy[None], D)
    dwg = mm_tn(u[None], dG, F)
    dwu = mm_tn(u[None], dU, F)
    dh_new, dg_pre = mm_nt_norm_bwd([(dG, wg), (dU, wu)], h, _row(g_pre), dh)
    return dh_new, dwg, dwu, dwd, dg_pre[0], dg_post[0]


def _lane_vec(v, at):
    return jnp.pad(v, (at, BLK - at - v.shape[0])).reshape(1, BLK)


def _ab_fwd(h, g_pre, g_post, w, tabs):
    u, proj = norm_proj(h, _row(g_pre), w["ab_in"], AB_IN_P // 3)
    ret, sall_r = ret_fwd(proj, *tabs)
    alog, dtb = _lane_vec(w["a_log"], N_HEAD_AB), _lane_vec(w["dt_bias"], N_HEAD_AB)
    cq, ck, cv, q, k, v, gates = gdn_prep_fwd(proj, w["conv"], alog, dtb)
    gdn, sall_g = gdn_chunk_fwd(q, k, v, gates, proj, _row(w["out_norm"]))
    mixed = jnp.concatenate([ret, gdn], axis=1)
    y, h_new = proj_norm_res(mixed[None], w["ab_out"][None], h, _row(g_post), 1.0)
    return h_new, (h, u, proj, sall_r, (cq, ck, cv, q, k, v, gates), sall_g, mixed, y, alog, dtb)


def _ab_bwd(dh, saved, g_pre, g_post, w, tabs):
    h, u, proj, sall_r, (cq, ck, cv, q, k, v, gates), sall_g, mixed, y, alog, dtb = saved
    dy, dg_post = post_norm_bwd(y, _row(g_post), dh, 1.0)
    dmix = mm_nt(dy, w["ab_out"])
    dw_out = mm_tn(mixed[None], dy[None], D)[0]
    drq, drk, drv, drg = ret_bwd(proj, *tabs, sall_r, dmix)
    onorm = _row(w["out_norm"])
    dq, dk, dv, dz, dgates, don = gdn_chunk_bwd(q, k, v, gates, proj, onorm, sall_g, dmix)
    dcq, dck, dcv, dgb, dal, ddt = gdn_prep_bwd(cq, ck, cv, proj, alog, dtb, dq, dk, dv, dgates)
    dxq, dxk, dxv, dconv = gdn_conv_bwd(dcq, dck, dcv, proj, w["conv"])
    dproj = jnp.concatenate([t.astype(bf16) for t in (drq, drk, drv, drg, dxq, dxk, dxv, dz, dgb)], axis=1)
    dw_in = mm_tn(u[None], dproj[None], AB_IN_P // 3)[0]
    dh_new, dg_pre = mm_nt_norm_bwd([(dproj[None], w["ab_in"][None])], h, _row(g_pre), dh, ksplit=3)
    small = dict(a_log=dal[0, N_HEAD_AB:2 * N_HEAD_AB], dt_bias=ddt[0, N_HEAD_AB:2 * N_HEAD_AB], out_norm=don[0], conv=dconv[0:GDN_K])
    return dh_new, dw_in, dw_out, dg_pre[0], dg_post[0], small


def _cd_fwd(h, g_pre, g_post, w):
    u, proj = norm_proj(h, _row(g_pre), w["cd_in"], CD_IN // 3)
    swa = swa_fwd(proj, w["sinks"])
    sb = sb_fwd(proj)
    mixed = jnp.concatenate([swa.astype(bf16), sb.astype(bf16)], axis=1)
    y, h_new = proj_norm_res(mixed[None], w["cd_out"][None], h, _row(g_post), 1.0)
    return h_new, (h, u, proj, sb, mixed, y)


def _cd_bwd(dh, saved, g_pre, g_post, w):
    h, u, proj, sb, mixed, y = saved
    dy, dg_post = post_norm_bwd(y, _row(g_post), dh, 1.0)
    dmix = mm_nt(dy, w["cd_out"])
    dw_out = mm_tn(mixed[None], dy[None], D)[0]
    dcq, cur, prev, meta, dsinks = swa_bwd(proj, w["sinks"], dmix)
    dsq, dsk, dsv = sb_bwd(proj, sb, dmix)
    dproj = cd_assemble(dcq, cur, prev, meta, dsq, dsk, dsv)
    dw_in = mm_tn(u[None], dproj[None], CD_IN // 3)[0]
    dh_new, dg_pre = mm_nt_norm_bwd([(dproj[None], w["cd_in"][None])], h, _row(g_pre), dh, ksplit=3)
    return dh_new, dw_in, dw_out, dg_pre[0], dg_post[0], dsinks[:, 0]


def local_step(x, target, w):
    L = x.shape[0]
    T = PAD + N_META + L
    tabs = rot_tables(T)
    h = jnp.concatenate([jnp.zeros((PAD, D), f32), w["meta"], x], axis=0)
    ng = w["norm_gains"]
    saved = []
    for i in range(2):
        g = ng[i]
        h, s1 = _ffn_fwd(h, g[0], g[1], w["wg"][i, 0], w["wu"][i, 0], w["wd"][i, 0])
        if i == 0:
            h, sm = _ab_fwd(h, g[2], g[3], w, tabs)
        else:
            h, sm = _cd_fwd(h, g[2], g[3], w)
        h, s2 = _ffn_fwd(h, g[4], g[5], w["wg"][i, 1], w["wu"][i, 1], w["wd"][i, 1])
        saved.append((s1, sm, s2))
    dh, lpart = loss_grad(h, target)
    grads = {}
    dng = [[None] * 6 for _ in range(2)]
    dwg = [[None, None], [None, None]]
    dwu = [[None, None], [None, None]]
    dwd = [[None, None], [None, None]]
    for i in (1, 0):
        g = ng[i]
        s1, sm, s2 = saved[i]
        dh, dwg[i][1], dwu[i][1], dwd[i][1], dng[i][4], dng[i][5] = _ffn_bwd(dh, s2, g[4], g[5], w["wg"][i, 1], w["wu"][i, 1], w["wd"][i, 1])
        if i == 0:
            dh, grads["ab_in"], grads["ab_out"], dng[i][2], dng[i][3], small = _ab_bwd(dh, sm, g[2], g[3], w, tabs)
            grads.update(small)
        else:
            dh, grads["cd_in"], grads["cd_out"], dng[i][2], dng[i][3], grads["sinks"] = _cd_bwd(dh, sm, g[2], g[3], w)
        dh, dwg[i][0], dwu[i][0], dwd[i][0], dng[i][0], dng[i][1] = _ffn_bwd(dh, s1, g[0], g[1], w["wg"][i, 0], w["wu"][i, 0], w["wd"][i, 0])
    grads["wg"], grads["wu"], grads["wd"] = dwg, dwu, dwd
    grads["norm_gains"] = jnp.stack([jnp.stack(r) for r in dng])
    grads["meta"] = dh[PAD:PAD + N_META]
    return lpart[0, 0], dh[PAD + N_META:], grads


def _r16(n, mult=16):
    return -(-n // mult) * mult


def _big_layout(F):
    halves = (("wg", "wu"), ("wd", "ab_in", "ab_out", "cd_in", "cd_out"))
    rows = dict(wg=4 * F, wu=4 * F, wd=4 * F, ab_in=AB_IN // 4, ab_out=D // 4, cd_in=CD_IN // 4, cd_out=D // 4)
    offs, used = {}, []
    for hf, names in enumerate(halves):
        o = 0
        for n in names:
            offs[n] = (hf, o, rows[n])
            o += _r16(rows[n])
        used.append(o)
    return offs, _r16(max(used), SUM_ROWS), halves


def _cat_rows(parts, total, mult=16):
    out = []
    for p in parts:
        pad = _r16(p.shape[-2], mult) - p.shape[-2]
        out.append(jnp.pad(p, [(0, 0)] * (p.ndim - 2) + [(0, pad), (0, 0)]) if pad else p)
    used = sum(o.shape[-2] for o in out)
    if total > used:
        out.append(jnp.zeros(out[0].shape[:-2] + (total - used, out[0].shape[-1]), out[0].dtype))
    return jnp.concatenate(out, axis=-2)


SMALL_ROWS = 72
REPL_ROWS = 8


def _small_rows(meta, ng, conv):
    lead = meta.shape[:-2]
    return _cat_rows([meta.reshape(lead + (32, BLK)), ng.reshape(lead + (24, BLK)), conv.reshape(lead + (12, BLK))], SMALL_ROWS, 8)


def _small_unrows(buf):
    lead = buf.shape[:-2]
    return buf[..., 0:32, :].reshape(lead + (N_META, D // 4)), buf[..., 32:56, :].reshape(lead + (2, 6, D // 4)), \
        buf[..., 56:68, :].reshape(lead + (GDN_K, 3 * W_AB // 4))


def _shard_cols(a, axis):
    shp = a.shape
    a = a.reshape(shp[:axis] + (4, shp[axis] // 4) + shp[axis + 1:])
    return jnp.moveaxis(a, axis, 0)


def _unshard_cols(a, axis):
    a = jnp.moveaxis(a, 0, axis)
    shp = a.shape
    return a.reshape(shp[:axis] + (4 * shp[axis + 1],) + shp[axis + 2:])


def kernel(x, meta_tokens, norm_gains, ffn_w_gate, ffn_w_up, ffn_w_down, ab_w_in, ab_conv_w, ab_a_log, ab_dt_bias, ab_out_norm, ab_w_out, cd_w_in, cd_sinks, cd_w_out, loss_target, m_meta_tokens, m_norm_gains, m_ffn_w_gate, m_ffn_w_up, m_ffn_w_down, m_ab_w_in, m_ab_conv_w, m_ab_a_log, m_ab_dt_bias, m_ab_out_norm, m_ab_w_out, m_cd_w_in, m_cd_sinks, m_cd_w_out, v_meta_tokens, v_norm_gains, v_ffn_w_gate, v_ffn_w_up, v_ffn_w_down, v_ab_w_in, v_ab_conv_w, v_ab_a_log, v_ab_dt_bias, v_ab_out_norm, v_ab_w_out, v_cd_w_in, v_cd_sinks, v_cd_w_out):
    F = ffn_w_gate.shape[-1]
    offs, H, halves = _big_layout(F)
    shard = dict(wg=ffn_w_gate, wu=ffn_w_up, wd=ffn_w_down, ab_in=ab_w_in, ab_out=ab_w_out, cd_in=cd_w_in, cd_out=cd_w_out)

    wbuf = jnp.stack([_cat_rows([shard[n].reshape(-1, D).astype(bf16) for n in names], H) for names in halves])
    sbuf = _small_rows(meta_tokens, norm_gains, ab_conv_w[0])
    gw, gs = gather_weights(wbuf, sbuf)

    def part(n):
        hf, o, r = offs[n]
        return gw[:, hf, o:o + r]

    meta_s, ng_s, conv_s = _small_unrows(gs)
    w = dict(
        wg=jnp.transpose(part("wg").reshape(4, 2, 2, D, F), (1, 2, 0, 3, 4)),
        wu=jnp.transpose(part("wu").reshape(4, 2, 2, D, F), (1, 2, 0, 3, 4)),
        wd=jnp.transpose(part("wd").reshape(4, 2, 2, F, D), (1, 2, 0, 3, 4)),
        ab_in=jnp.pad(_unshard_cols(part("ab_in").reshape(4, D, AB_IN // 4), 1), ((0, 0), (0, AB_IN_P - AB_IN))),
        ab_out=part("ab_out").reshape(D, D),
        cd_in=_unshard_cols(part("cd_in").reshape(4, D, CD_IN // 4), 1),
        cd_out=part("cd_out").reshape(D, D),
        meta=_unshard_cols(meta_s, 1), norm_gains=_unshard_cols(ng_s, 2), conv=_unshard_cols(conv_s, 1),
        a_log=ab_a_log[0], dt_bias=ab_dt_bias[0], out_norm=ab_out_norm[0], sinks=cd_sinks[0],
    )

    loss_local, dx, g = local_step(x[0], loss_target[0], w)

    def stack22(t):
        return jnp.stack([jnp.stack(r) for r in t])

    gparts = dict(
        wg=jnp.transpose(stack22(g["wg"]), (2, 0, 1, 3, 4)).reshape(4, 4 * F, D),
        wu=jnp.transpose(stack22(g["wu"]), (2, 0, 1, 3, 4)).reshape(4, 4 * F, D),
        wd=jnp.transpose(stack22(g["wd"]), (2, 0, 1, 3, 4)).reshape(4, 4 * F, D),
        ab_in=_shard_cols(g["ab_in"][:, :AB_IN], 1).reshape(4, AB_IN // 4, D),
        ab_out=g["ab_out"].reshape(4, D // 4, D),
        cd_in=_shard_cols(g["cd_in"], 1).reshape(4, CD_IN // 4, D),
        cd_out=g["cd_out"].reshape(4, D // 4, D),
    )
    gbuf = jnp.stack([_cat_rows([gparts[n].astype(bf16) for n in names], H) for names in halves], axis=1)
    core = lax.axis_index("c").astype(jnp.int32)
    chip = (2 * lax.axis_index("x") + lax.axis_index("y")).astype(jnp.int32)
    got = pair_exchange(gbuf)
    hsum = pair_sum(gbuf, got, core.reshape(1))
    parts = chip_exchange(hsum)
    rsum = chip_sum(hsum, parts, chip.reshape(1))
    other = pair_gather(rsum)
    ghalf = (jnp.where(core == 0, rsum, other), jnp.where(core == 0, other, rsum))

    onehot = np.eye(REPL_ROWS, dtype=np.float32)
    repl = sum(onehot[k][:, None] * _lane_vec(g[n], 0) for k, n in enumerate(("a_log", "dt_bias", "out_norm", "sinks")))
    ssrc = jnp.concatenate([_small_rows(_shard_cols(g["meta"], 1), _shard_cols(g["norm_gains"], 2), _shard_cols(g["conv"], 1)),
                            jnp.broadcast_to(repl, (4, REPL_ROWS, BLK))], axis=1)
    sred = small_reduce(ssrc)
    g_meta, g_ng, g_conv = _small_unrows(sred[:SMALL_ROWS])

    def gpart(n, shape):
        hf, o, r = offs[n]
        return ghalf[hf][o:o + r].reshape(shape)

    grad = dict(
        meta_tokens=g_meta, norm_gains=g_ng,
        ffn_w_gate=gpart("wg", ffn_w_gate.shape), ffn_w_up=gpart("wu", ffn_w_up.shape), ffn_w_down=gpart("wd", ffn_w_down.shape),
        ab_w_in=gpart("ab_in", ab_w_in.shape), ab_conv_w=g_conv[None],
        ab_a_log=sred[SMALL_ROWS:SMALL_ROWS + 1, 0:N_HEAD_AB], ab_dt_bias=sred[SMALL_ROWS + 1:SMALL_ROWS + 2, 0:N_HEAD_AB],
        ab_out_norm=sred[SMALL_ROWS + 2:SMALL_ROWS + 3, :], ab_w_out=gpart("ab_out", ab_w_out.shape),
        cd_w_in=gpart("cd_in", cd_w_in.shape), cd_sinks=sred[SMALL_ROWS + 3:SMALL_ROWS + 4, 0:2 * SWA_G], cd_w_out=gpart("cd_out", cd_w_out.shape),
    )

    weights = dict(meta_tokens=meta_tokens, norm_gains=norm_gains, ffn_w_gate=ffn_w_gate, ffn_w_up=ffn_w_up, ffn_w_down=ffn_w_down,
                   ab_w_in=ab_w_in, ab_conv_w=ab_conv_w, ab_a_log=ab_a_log, ab_dt_bias=ab_dt_bias, ab_out_norm=ab_out_norm,
                   ab_w_out=ab_w_out, cd_w_in=cd_w_in, cd_sinks=cd_sinks, cd_w_out=cd_w_out)
    ms = dict(meta_tokens=m_meta_tokens, norm_gains=m_norm_gains, ffn_w_gate=m_ffn_w_gate, ffn_w_up=m_ffn_w_up, ffn_w_down=m_ffn_w_down,
              ab_w_in=m_ab_w_in, ab_conv_w=m_ab_conv_w, ab_a_log=m_ab_a_log, ab_dt_bias=m_ab_dt_bias, ab_out_norm=m_ab_out_norm,
              ab_w_out=m_ab_w_out, cd_w_in=m_cd_w_in, cd_sinks=m_cd_sinks, cd_w_out=m_cd_w_out)
    vs = dict(meta_tokens=v_meta_tokens, norm_gains=v_norm_gains, ffn_w_gate=v_ffn_w_gate, ffn_w_up=v_ffn_w_up, ffn_w_down=v_ffn_w_down,
              ab_w_in=v_ab_w_in, ab_conv_w=v_ab_conv_w, ab_a_log=v_ab_a_log, ab_dt_bias=v_ab_dt_bias, ab_out_norm=v_ab_out_norm,
              ab_w_out=v_ab_w_out, cd_w_in=v_cd_w_in, cd_sinks=v_cd_sinks, cd_w_out=v_cd_w_out)
    order = list(weights)
    delta, new_m, new_v = {}, {}, {}
    for n in order:
        shp = weights[n].shape
        two = (-1, shp[-1])
        d, mn, vn = adamw(weights[n].reshape(two), grad[n].reshape(two), ms[n].reshape(two), vs[n].reshape(two))
        delta[n], new_m[n], new_v[n] = d.reshape(shp), mn.reshape(shp), vn.reshape(shp)

    loss = lax.psum(loss_local, ("x", "y", "c"))
    return (loss, dx[None], *[grad[n].reshape(weights[n].shape) for n in order], *[delta[n] for n in order],
            *[new_m[n] for n in order], *[new_v[n] for n in order])
```
